```python
import math
import jax, jax.numpy as jnp
from jax import lax
import numpy as np

D_MODEL = 2048
BATCH = 2
SEQ = 16384
DEPTH = 2

CHUNK = 64
LEFT_CHUNKS = 8
BAND = (LEFT_CHUNKS + 1) * CHUNK
A_HEADS = 32
A_HEAD_DIM = D_MODEL // A_HEADS
REL_CLIP = 2 * CHUNK
B_HEADS = 16
B_HEAD_DIM = D_MODEL // B_HEADS
Q_BLOCK = 128
N_A_LAYERS = DEPTH // 2
N_EXPERTS = 16
N_GROUPS = 4
EXPERTS_PER_GROUP = N_EXPERTS // N_GROUPS
TOP_K = 2
D_FF = 2048
EXPERT_BLOCK = 256
ALPHA = (2.0 * DEPTH) ** 0.25
BETA = (8.0 * DEPTH) ** -0.25
LN_EPS = 1e-5
NEG = -1e30

kernel_name = "yoco_chunk_relpos_fox_grouped_moe_deepnorm"


def layer_norm(x, gain, bias):
    xf = x.astype(jnp.float32)
    mu = jnp.mean(xf, axis=-1, keepdims=True)
    var = jnp.mean(jnp.square(xf - mu), axis=-1, keepdims=True)
    y = (xf - mu) * lax.rsqrt(var + LN_EPS) * gain.astype(jnp.float32) + bias.astype(jnp.float32)
    return y.astype(x.dtype)


def chunk_relpos_attention(x, w_qkv, rel_bias, w_o):
    b, s, d = x.shape
    nc = s // CHUNK
    pad = LEFT_CHUNKS * CHUNK
    q, k, v = jnp.split(x @ w_qkv, 3, axis=-1)
    q = q.reshape(b, nc, CHUNK, A_HEADS, A_HEAD_DIM).transpose(1, 0, 2, 3, 4)
    k = jnp.pad(k.reshape(b, s, A_HEADS, A_HEAD_DIM), ((0, 0), (pad, 0), (0, 0), (0, 0)))
    v = jnp.pad(v.reshape(b, s, A_HEADS, A_HEAD_DIM), ((0, 0), (pad, 0), (0, 0), (0, 0)))
    rel = jnp.arange(CHUNK)[:, None] - jnp.arange(BAND)[None, :] + pad
    bias = rel_bias[:, jnp.clip(rel, -REL_CLIP, REL_CLIP) + REL_CLIP].astype(jnp.float32)
    scale = 1.0 / math.sqrt(A_HEAD_DIM)

    def one_chunk(args):
        qc, c = args
        kc = lax.dynamic_slice_in_dim(k, c * CHUNK, BAND, axis=1)
        vc = lax.dynamic_slice_in_dim(v, c * CHUNK, BAND, axis=1)
        sc = jnp.einsum('bqhd,bkhd->bhqk', qc, kc).astype(jnp.float32) * scale + bias[None]
        key_pos = c * CHUNK - pad + jnp.arange(BAND)
        sc = jnp.where((key_pos >= 0)[None, None, None, :], sc, NEG)
        p = jax.nn.softmax(sc, axis=-1)
        return jnp.einsum('bhqk,bkhd->bqhd', p.astype(vc.dtype), vc)

    out = lax.map(one_chunk, (q, jnp.arange(nc)))
    out = out.transpose(1, 0, 2, 3, 4).reshape(b, s, d)
    return out @ w_o


def shared_kv_side(x, kv_w, fg_w, fg_b):
    b, s, d = x.shape
    k, v = jnp.split(x @ kv_w, 2, axis=-1)
    k = k.reshape(b, s, B_HEADS, B_HEAD_DIM)
    v = v.reshape(b, s, B_HEADS, B_HEAD_DIM)
    log_f = jax.nn.log_sigmoid((x @ fg_w).astype(jnp.float32) + fg_b.astype(jnp.float32))
    f_cum = jnp.cumsum(log_f, axis=1)
    return k, v, f_cum


def forgetting_attention(x, w_q, w_o, k, v, f_cum):
    b, s, d = x.shape
    nq = s // Q_BLOCK
    q = (x @ w_q).reshape(b, nq, Q_BLOCK, B_HEADS, B_HEAD_DIM).transpose(1, 0, 2, 3, 4)
    fq = f_cum.reshape(b, nq, Q_BLOCK, B_HEADS).transpose(1, 0, 3, 2)
    fk = f_cum.transpose(0, 2, 1)
    key_pos = jnp.arange(s)
    scale = 1.0 / math.sqrt(B_HEAD_DIM)

    def one_block(args):
        qb, fqb, i = args
        sc = jnp.einsum('bqhd,bkhd->bhqk', qb, k).astype(jnp.float32) * scale
        sc = sc + fqb[..., None] - fk[:, :, None, :]
        q_pos = i * Q_BLOCK + jnp.arange(Q_BLOCK)
        sc = jnp.where((key_pos[None, :] <= q_pos[:, None])[None, None], sc, NEG)
        p = jax.nn.softmax(sc, axis=-1)
        return jnp.einsum('bhqk,bkhd->bqhd', p.astype(v.dtype), v)

    out = lax.map(one_block, (q, fq, jnp.arange(nq)))
    out = out.transpose(1, 0, 2, 3, 4).reshape(b, s, d)
    return out @ w_o


def grouped_top2_route(xt, router_w, router_b):
    logits = (xt @ router_w).astype(jnp.float32) + router_b.astype(jnp.float32)
    probs = jax.nn.softmax(logits, axis=-1)
    pg = probs.reshape(-1, N_GROUPS, EXPERTS_PER_GROUP)
    group_score = jnp.sum(lax.top_k(pg, TOP_K)[0], axis=-1)
    g = jnp.argmax(group_score, axis=-1)
    p_in = jnp.take_along_axis(pg, g[:, None, None], axis=1)[:, 0]
    vals, loc = lax.top_k(p_in, TOP_K)
    experts = (g[:, None] * EXPERTS_PER_GROUP + loc).astype(jnp.int32)
    gates = vals / jnp.sum(vals, axis=-1, keepdims=True)
    return experts, gates


def moe_ffn(xt, experts, gates, w_gate, w_up, w_down):
    n, d = xt.shape
    a = n * TOP_K
    e_flat = experts.reshape(-1)
    tok_flat = jnp.repeat(jnp.arange(n, dtype=jnp.int32), TOP_K)
    g_flat = gates.reshape(-1)
    counts = jnp.bincount(e_flat, length=N_EXPERTS)
    starts = jnp.cumsum(counts) - counts
    padded = (counts + EXPERT_BLOCK - 1) // EXPERT_BLOCK * EXPERT_BLOCK
    pend = jnp.cumsum(padded)
    pstart = pend - padded
    order = jnp.argsort(e_flat)
    e_sorted = e_flat[order]
    dest = pstart[e_sorted] + jnp.arange(a) - starts[e_sorted]
    n_blocks = -(-a // EXPERT_BLOCK) + N_EXPERTS
    p = n_blocks * EXPERT_BLOCK
    slot_tok = jnp.full((p,), n, jnp.int32).at[dest].set(tok_flat[order])
    slot_gate = jnp.zeros((p,), jnp.float32).at[dest].set(g_flat[order])
    block_expert = jnp.minimum(
        jnp.searchsorted(pend, jnp.arange(n_blocks) * EXPERT_BLOCK, side='right'), N_EXPERTS - 1)
    x_pad = jnp.concatenate([xt, jnp.zeros((1, d), xt.dtype)], axis=0)
    xb = x_pad[slot_tok].reshape(n_blocks, EXPERT_BLOCK, d)

    def run_block(args):
        xblk, e = args
        hid = jax.nn.silu(xblk @ w_gate[e]) * (xblk @ w_up[e])
        return hid @ w_down[e]

    yb = lax.map(run_block, (xb, block_expert)).reshape(p, d)
    y = jnp.zeros((n + 1, d), yb.dtype).at[slot_tok].add(yb * slot_gate[:, None].astype(yb.dtype))
    return y[:n]


def setup_inputs(seed: int = 0) -> dict:
    key = jax.random.key(seed)
    ks = jax.random.split(key, 20)
    n_a = N_A_LAYERS
    n_b = DEPTH - N_A_LAYERS
    d = D_MODEL
    s = d ** -0.5
    nrm = jax.random.normal
    x = nrm(ks[0], (BATCH, SEQ, d), jnp.float32)
    a_w_qkv = jnp.concatenate([nrm(ks[1], (n_a, d, 2 * d)) * s,
                               nrm(ks[2], (n_a, d, d)) * (s * BETA)], axis=-1)
    a_rel_bias = 0.2 * nrm(ks[3], (n_a, A_HEADS, 2 * REL_CLIP + 1))
    a_w_o = nrm(ks[4], (n_a, d, d)) * (s * BETA)
    kv_w = jnp.concatenate([nrm(ks[5], (d, d)) * s, nrm(ks[6], (d, d)) * (s * BETA)], axis=-1)
    fg_w = nrm(ks[7], (d, B_HEADS)) * s
    fg_b = 2.0 + 2.0 * jax.random.uniform(ks[8], (B_HEADS,))
    b_w_q = nrm(ks[9], (n_b, d, d)) * s
    b_w_o = nrm(ks[10], (n_b, d, d)) * (s * BETA)
    router_w = nrm(ks[11], (d, N_EXPERTS)) * s
    router_b = 0.01 * nrm(ks[12], (N_EXPERTS,))
    moe_w_gate = nrm(ks[13], (DEPTH, N_EXPERTS, d, D_FF)) * s
    moe_w_up = nrm(ks[14], (DEPTH, N_EXPERTS, d, D_FF)) * (s * BETA)
    moe_w_down = nrm(ks[15], (DEPTH, N_EXPERTS, D_FF, d)) * (D_FF ** -0.5 * BETA)
    ln_gain = 1.0 + 0.02 * nrm(ks[16], (DEPTH, 2, d))
    ln_bias = 0.02 * nrm(ks[17], (DEPTH, 2, d))
    return {"x": x, "a_w_qkv": a_w_qkv, "a_rel_bias": a_rel_bias, "a_w_o": a_w_o,
            "kv_w": kv_w, "fg_w": fg_w, "fg_b": fg_b, "b_w_q": b_w_q, "b_w_o": b_w_o,
            "router_w": router_w, "router_b": router_b, "moe_w_gate": moe_w_gate,
            "moe_w_up": moe_w_up, "moe_w_down": moe_w_down, "ln_gain": ln_gain,
            "ln_bias": ln_bias}


def reference(x, a_w_qkv, a_rel_bias, a_w_o, kv_w, fg_w, fg_b, b_w_q, b_w_o,
              router_w, router_b, moe_w_gate, moe_w_up, moe_w_down, ln_gain, ln_bias):
    b, s, d = x.shape
    k_sh = v_sh = f_sh = None
    for layer in range(DEPTH):
        if layer < N_A_LAYERS:
            h = chunk_relpos_attention(x, a_w_qkv[layer], a_rel_bias[layer], a_w_o[layer])
        else:
            lb = layer - N_A_LAYERS
            h = forgetting_attention(x, b_w_q[lb], b_w_o[lb], k_sh, v_sh, f_sh)
        x = layer_norm(ALPHA * x + h, ln_gain[layer, 0], ln_bias[layer, 0])
        xt = x.reshape(b * s, d)
        experts, gates = grouped_top2_route(xt, router_w, router_b)
        y = moe_ffn(xt, experts, gates, moe_w_gate[layer], moe_w_up[layer], moe_w_down[layer])
        x = layer_norm(ALPHA * x + y.reshape(b, s, d), ln_gain[layer, 1], ln_bias[layer, 1])
        if layer == N_A_LAYERS - 1:
            k_sh, v_sh, f_sh = shared_kv_side(x, kv_w, fg_w, fg_b)
    return x
```

```python
import functools
import math

import jax
import jax.numpy as jnp
from jax import lax
from jax.experimental import pallas as pl
from jax.experimental.pallas import tpu as pltpu

F32 = jnp.float32
BF16 = jnp.bfloat16

CHUNK = 64
LEFT_CHUNKS = 8
A_HEADS = 32
A_HEAD_DIM = 64
REL_CLIP = 2 * CHUNK
B_HEADS = 16
B_HEAD_DIM = 128
N_EXPERTS = 16
N_GROUPS = 4
EXPERTS_PER_GROUP = 4
DEPTH = 2
ALPHA = (2.0 * DEPTH) ** 0.25
LN_EPS = 1e-5
NEG = -1e30

LANES = 128
VMEM_LIMIT = 56 * 1024 * 1024

MM_TM = 512
MM_TN = 1024
PAD_ROWS = LEFT_CHUNKS * CHUNK
A_TQ = 256
A_WIN = A_TQ + PAD_ROWS
POST_TM = 256
MOE_BLK = 512
MOE_TF = 512
LN_TM = 512
FG_TM = 1024
FOX_T = 512


def _params(sem):
    return pltpu.CompilerParams(dimension_semantics=sem, vmem_limit_bytes=VMEM_LIMIT)


def _mm_kernel(x_ref, w_ref, o_ref, *, pad_blocks):
    def compute():
        o_ref[...] = jnp.dot(x_ref[...], w_ref[...],
                             preferred_element_type=F32).astype(o_ref.dtype)

    if pad_blocks:
        i = pl.program_id(2)

        @pl.when(i < pad_blocks)
        def _():
            o_ref[...] = jnp.zeros_like(o_ref)

        pl.when(i >= pad_blocks)(compute)
    else:
        compute()


def _mm(x, w, *, pad_blocks=0):
    b, s, k = x.shape
    m = w.shape[1]
    tn = min(MM_TN, m)
    grid = (m // tn, b, s // MM_TM + pad_blocks)
    return pl.pallas_call(
        functools.partial(_mm_kernel, pad_blocks=pad_blocks),
        out_shape=jax.ShapeDtypeStruct((b, s + pad_blocks * MM_TM, m), BF16),
        grid=grid,
        in_specs=[
            pl.BlockSpec((None, MM_TM, k),
                         lambda j, bb, i: (bb, jnp.maximum(i - pad_blocks, 0), 0)),
            pl.BlockSpec((k, tn), lambda j, bb, i: (0, j)),
        ],
        out_specs=pl.BlockSpec((None, MM_TM, tn), lambda j, bb, i: (bb, i, j)),
        compiler_params=_params(("arbitrary", "arbitrary", "arbitrary")),
        name="dense_proj",
    )(x, w)


def _attn_a_kernel(q_ref, k_ref, v_ref, bias_ref, o_ref):
    i = pl.program_id(2)
    start = pl.multiple_of(i * A_TQ, A_TQ)
    kw = k_ref[pl.ds(start, A_WIN), :]
    vw = v_ref[pl.ds(start, A_WIN), :]
    q = q_ref[...]
    lane = lax.broadcasted_iota(jnp.int32, (A_TQ, LANES), 1)
    col = lax.broadcasted_iota(jnp.int32, (1, A_WIN), 1)
    keymask = jnp.where(col >= PAD_ROWS - i * A_TQ, 0.0, NEG).astype(F32)
    outs = []
    for h in range(2):
        sel = (lane < A_HEAD_DIM) if h == 0 else (lane >= A_HEAD_DIM)
        qh = jnp.where(sel, q, jnp.zeros_like(q))
        s = lax.dot_general(qh, kw, (((1,), (1,)), ((), ())), preferred_element_type=F32)
        s = s + bias_ref[h] + keymask
        m = jnp.max(s, axis=-1, keepdims=True)
        p = jnp.exp(s - m)
        l = jnp.sum(p, axis=-1, keepdims=True)
        o = jnp.dot(p.astype(BF16), vw, preferred_element_type=F32)
        outs.append(o / l)
    o_ref[...] = jnp.where(lane < A_HEAD_DIM, outs[0], outs[1]).astype(o_ref.dtype)


def _attn_a(qkv_pad, bias_tab, s):
    b = qkv_pad.shape[0]
    d = A_HEADS * A_HEAD_DIM
    pairs = d // LANES
    sp = s + PAD_ROWS
    qoff = PAD_ROWS // A_TQ
    return pl.pallas_call(
        _attn_a_kernel,
        out_shape=jax.ShapeDtypeStruct((b, s, d), BF16),
        grid=(pairs, b, s // A_TQ),
        in_specs=[
            pl.BlockSpec((None, A_TQ, LANES), lambda j, bb, i: (bb, i + qoff, j)),
            pl.BlockSpec((None, sp, LANES), lambda j, bb, i: (bb, 0, pairs + j)),
            pl.BlockSpec((None, sp, LANES), lambda j, bb, i: (bb, 0, 2 * pairs + j)),
            pl.BlockSpec((2, A_TQ, A_WIN), lambda j, bb, i: (j, 0, 0)),
        ],
        out_specs=pl.BlockSpec((None, A_TQ, LANES), lambda j, bb, i: (bb, i, j)),
        compiler_params=_params(("arbitrary", "arbitrary", "arbitrary")),
        name="mixer_a",
    )(qkv_pad, qkv_pad, qkv_pad, bias_tab)


def _rel_bias_table(rel_bias):
    qi = jnp.arange(A_TQ)[:, None]
    kj = jnp.arange(A_WIN)[None, :]
    rel = qi + PAD_ROWS - kj
    dchunk = kj // CHUNK - qi // CHUNK
    inband = (dchunk >= 0) & (dchunk <= LEFT_CHUNKS)
    idx = jnp.clip(rel, -REL_CLIP, REL_CLIP) + REL_CLIP
    tab = rel_bias.astype(F32)[:, idx]
    return jnp.where(inband[None], tab, NEG)


def _layer_norm(z, gain, bias):
    mu = jnp.mean(z, axis=-1, keepdims=True)
    zc = z - mu
    var = jnp.mean(zc * zc, axis=-1, keepdims=True)
    return zc * lax.rsqrt(var + LN_EPS) * gain + bias


def _small_proj_t(y, yb, wh_ref, wl_ref, n_out):
    ylo = (y - yb.astype(F32)).astype(BF16)
    wh = wh_ref[...]
    acc = jnp.dot(yb, wh, preferred_element_type=F32)
    acc = acc + jnp.dot(ylo, wh, preferred_element_type=F32)
    acc = acc + jnp.dot(yb, wl_ref[...], preferred_element_type=F32)
    return acc.T[:n_out, :]


def _split_small_weight(w):
    k, n = w.shape
    wp = jnp.zeros((k, LANES), F32).at[:, :n].set(w.astype(F32))
    wh = wp.astype(BF16)
    wl = (wp - wh.astype(F32)).astype(BF16)
    return wh, wl


def _route(lt):
    m = jnp.max(lt, axis=0, keepdims=True)
    e = jnp.exp(lt - m)
    p = e / jnp.sum(e, axis=0, keepdims=True)
    rows = [p[k:k + 1, :] for k in range(N_EXPERTS)]
    best = None
    gi = None
    for g in range(N_GROUPS):
        r = rows[EXPERTS_PER_GROUP * g:EXPERTS_PER_GROUP * (g + 1)]
        sc = None
        for a in range(EXPERTS_PER_GROUP):
            for c in range(a + 1, EXPERTS_PER_GROUP):
                pair = r[a] + r[c]
                sc = pair if sc is None else jnp.maximum(sc, pair)
        if g == 0:
            best = sc
            gi = jnp.zeros(sc.shape, jnp.int32)
        else:
            upd = sc > best
            best = jnp.where(upd, sc, best)
            gi = jnp.where(upd, g, gi)
    pin = []
    for k in range(EXPERTS_PER_GROUP):
        v = rows[(N_GROUPS - 1) * EXPERTS_PER_GROUP + k]
        for g in range(N_GROUPS - 2, -1, -1):
            v = jnp.where(gi == g, rows[g * EXPERTS_PER_GROUP + k], v)
        pin.append(v)
    v1 = pin[0]
    i1 = jnp.zeros(v1.shape, jnp.int32)
    for k in range(1, EXPERTS_PER_GROUP):
        upd = pin[k] > v1
        v1 = jnp.where(upd, pin[k], v1)
        i1 = jnp.where(upd, k, i1)
    v2 = jnp.full(v1.shape, -1.0, F32)
    i2 = jnp.zeros(v1.shape, jnp.int32)
    for k in range(EXPERTS_PER_GROUP):
        cand = jnp.where(i1 == k, -1.0, pin[k])
        upd = cand > v2
        v2 = jnp.where(upd, cand, v2)
        i2 = jnp.where(upd, k, i2)
    tot = v1 + v2
    base = gi * EXPERTS_PER_GROUP
    experts = jnp.concatenate([base + i1, base + i2], axis=0)
    gates = jnp.concatenate([v1 / tot, v2 / tot], axis=0)
    return experts, gates


def _post_attn_kernel(a_ref, w_ref, x_ref, g_ref, b_ref, rwh_ref, rwl_ref, rb_ref,
                      x1_ref, x1b_ref, ex_ref, gt_ref):
    h = jnp.dot(a_ref[...], w_ref[...], preferred_element_type=F32)
    y = _layer_norm(ALPHA * x_ref[...] + h, g_ref[...], b_ref[...])
    yb = y.astype(BF16)
    x1_ref[...] = y
    x1b_ref[...] = yb
    lt = _small_proj_t(y, yb, rwh_ref, rwl_ref, N_EXPERTS) + rb_ref[...]
    experts, gates = _route(lt)
    ex_ref[...] = experts
    gt_ref[...] = gates


def _post_attn(attn, w_o, x, gain, bias, rwh, rwl, rb):
    n, d = x.shape
    tm = POST_TM
    row = lambda i: (i, 0)
    const = lambda i: (0, 0)
    return pl.pallas_call(
        _post_attn_kernel,
        out_shape=(jax.ShapeDtypeStruct((n, d), F32),
                   jax.ShapeDtypeStruct((n, d), BF16),
                   jax.ShapeDtypeStruct((2, n), jnp.int32),
                   jax.ShapeDtypeStruct((2, n), F32)),
        grid=(n // tm,),
        in_specs=[
            pl.BlockSpec((tm, d), row),
            pl.BlockSpec((d, d), const),
            pl.BlockSpec((tm, d), row),
            pl.BlockSpec((1, d), const),
            pl.BlockSpec((1, d), const),
            pl.BlockSpec((d, LANES), const),
            pl.BlockSpec((d, LANES), const),
            pl.BlockSpec((N_EXPERTS, 1), const),
        ],
        out_specs=(pl.BlockSpec((tm, d), row),
                   pl.BlockSpec((tm, d), row),
                   pl.BlockSpec((2, tm), lambda i: (0, i)),
                   pl.BlockSpec((2, tm), lambda i: (0, i))),
        compiler_params=_params(("arbitrary",)),
        name="out_proj_ln_router",
    )(attn, w_o, x, gain, bias, rwh, rwl, rb)


def _moe_kernel(iblk, iex, ivalid, ifirst, starts, ends, x_ref, wg_ref, wu_ref, wd_ref, o_ref):
    w = pl.program_id(0)
    f = pl.program_id(1)

    @pl.when((ifirst[w] == 1) & (f == 0))
    def _():
        o_ref[...] = jnp.zeros_like(o_ref)

    @pl.when(ivalid[w] == 1)
    def _():
        x = x_ref[...]
        hg = jnp.dot(x, wg_ref[...], preferred_element_type=F32)
        hu = jnp.dot(x, wu_ref[...], preferred_element_type=F32)
        e = iex[w]
        row = iblk[w] * MOE_BLK + lax.broadcasted_iota(jnp.int32, (MOE_BLK, 1), 0)
        mine = (row >= starts[e]) & (row < ends[e])
        hid = jnp.where(mine, hg * jax.nn.sigmoid(hg) * hu, 0.0)
        o_ref[...] += jnp.dot(hid.astype(BF16), wd_ref[...], preferred_element_type=F32)


def _moe(xs, w_gate, w_up, w_down, meta):
    a, d = xs.shape
    dff = w_gate.shape[-1]
    nf = dff // MOE_TF
    n_items = meta[0].shape[0]

    def fsel(w, f, ivalid):
        return jnp.where(ivalid[w] == 1, f, nf - 1)

    grid_spec = pltpu.PrefetchScalarGridSpec(
        num_scalar_prefetch=6,
        grid=(n_items, nf),
        in_specs=[
            pl.BlockSpec((MOE_BLK, d), lambda w, f, ib, ie, iv, i1, st, en: (ib[w], 0)),
            pl.BlockSpec((None, d, MOE_TF),
                         lambda w, f, ib, ie, iv, i1, st, en: (ie[w], 0, fsel(w, f, iv))),
            pl.BlockSpec((None, d, MOE_TF),
                         lambda w, f, ib, ie, iv, i1, st, en: (ie[w], 0, fsel(w, f, iv))),
            pl.BlockSpec((None, MOE_TF, d),
                         lambda w, f, ib, ie, iv, i1, st, en: (ie[w], fsel(w, f, iv), 0)),
        ],
        out_specs=pl.BlockSpec((MOE_BLK, d), lambda w, f, ib, ie, iv, i1, st, en: (ib[w], 0)),
    )
    return pl.pallas_call(
        _moe_kernel,
        out_shape=jax.ShapeDtypeStruct((a, d), F32),
        grid_spec=grid_spec,
        compiler_params=_params(("arbitrary", "arbitrary")),
        name="moe_experts",
    )(*meta, xs, w_gate, w_up, w_down)


def _moe_plan(experts):
    n = experts.shape[1]
    a = 2 * n
    nblk = a // MOE_BLK
    n_items = nblk + N_EXPERTS - 1
    e_flat = experts.reshape(-1)
    order = jnp.argsort(e_flat, stable=True).astype(jnp.int32)
    e_sorted = e_flat[order]
    tok_sorted = order % n
    pos = jnp.zeros((a,), jnp.int32).at[order].set(jnp.arange(a, dtype=jnp.int32)).reshape(2, n)
    counts = jnp.zeros((N_EXPERTS,), jnp.int32).at[e_flat].add(1)
    ends = jnp.cumsum(counts).astype(jnp.int32)
    starts = ends - counts
    e_lo = e_sorted[0::MOE_BLK]
    e_hi = e_sorted[MOE_BLK - 1::MOE_BLK]
    per_blk = e_hi - e_lo + 1
    iend = jnp.cumsum(per_blk).astype(jnp.int32)
    istart = iend - per_blk
    total = iend[-1]
    w = jnp.arange(n_items, dtype=jnp.int32)
    blk = jnp.clip(jnp.searchsorted(iend, w, side="right"), 0, nblk - 1).astype(jnp.int32)
    ex = e_lo[blk] + (w - istart[blk])
    valid = w < total
    blk = jnp.where(valid, blk, nblk - 1)
    ex = jnp.where(valid, ex, e_hi[nblk - 1])
    first = valid & (w == istart[blk])
    meta = (blk, ex.astype(jnp.int32), valid.astype(jnp.int32), first.astype(jnp.int32),
            starts, ends)
    return meta, tok_sorted, pos


def _combine_ln_kernel(x_ref, y0_ref, y1_ref, gt_ref, g_ref, b_ref, o_ref, ob_ref):
    gt = gt_ref[...]
    y = y0_ref[...] * gt[:, 0:1] + y1_ref[...] * gt[:, 1:2]
    out = _layer_norm(ALPHA * x_ref[...] + y, g_ref[...], b_ref[...])
    o_ref[...] = out
    ob_ref[...] = out.astype(BF16)


def _combine_ln(x, y0, y1, gates_t, gain, bias):
    n, d = x.shape
    tm = LN_TM
    row = lambda i: (i, 0)
    const = lambda i: (0, 0)
    return pl.pallas_call(
        _combine_ln_kernel,
        out_shape=(jax.ShapeDtypeStruct((n, d), F32), jax.ShapeDtypeStruct((n, d), BF16)),
        grid=(n // tm,),
        in_specs=[
            pl.BlockSpec((tm, d), row),
            pl.BlockSpec((tm, d), row),
            pl.BlockSpec((tm, d), row),
            pl.BlockSpec((tm, 2), row),
            pl.BlockSpec((1, d), const),
            pl.BlockSpec((1, d), const),
        ],
        out_specs=(pl.BlockSpec((tm, d), row), pl.BlockSpec((tm, d), row)),
        compiler_params=_params(("arbitrary",)),
        name="moe_combine_ln",
    )(x, y0, y1, gates_t, gain, bias)


def _fg_kernel(x_ref, wh_ref, wl_ref, b_ref, tri_ref, f_ref, carry_ref):
    i = pl.program_id(1)

    @pl.when(i == 0)
    def _():
        carry_ref[...] = jnp.zeros_like(carry_ref)

    y = x_ref[...]
    z = _small_proj_t(y, y.astype(BF16), wh_ref, wl_ref, B_HEADS) + b_ref[...]
    lf = jnp.minimum(z, 0.0) - jnp.log1p(jnp.exp(-jnp.abs(z)))
    hi = lf.astype(BF16)
    r1 = lf - hi.astype(F32)
    mid = r1.astype(BF16)
    lo = (r1 - mid.astype(F32)).astype(BF16)
    parts = jnp.concatenate([hi, mid, lo], axis=0)
    cs = jnp.dot(parts, tri_ref[...], preferred_element_type=F32)
    cum = (cs[0:B_HEADS] + cs[B_HEADS:2 * B_HEADS] + cs[2 * B_HEADS:3 * B_HEADS]) + carry_ref[...]
    f_ref[...] = cum
    carry_ref[...] = cum[:, FG_TM - 1:FG_TM]


def _forget_cumsum(x, wh, wl, fb):
    b, s, d = x.shape
    tri = (jnp.arange(FG_TM)[:, None] <= jnp.arange(FG_TM)[None, :]).astype(BF16)
    const = lambda bb, i: (0, 0)
    return pl.pallas_call(
        _fg_kernel,
        out_shape=jax.ShapeDtypeStruct((b, B_HEADS, s), F32),
        grid=(b, s // FG_TM),
        in_specs=[
            pl.BlockSpec((None, FG_TM, d), lambda bb, i: (bb, i, 0)),
            pl.BlockSpec((d, LANES), const),
            pl.BlockSpec((d, LANES), const),
            pl.BlockSpec((B_HEADS, 1), const),
            pl.BlockSpec((FG_TM, FG_TM), const),
        ],
        out_specs=pl.BlockSpec((None, B_HEADS, FG_TM), lambda bb, i: (bb, 0, i)),
        scratch_shapes=[pltpu.VMEM((B_HEADS, 1), F32)],
        compiler_params=_params(("arbitrary", "arbitrary")),
        name="forget_cumsum",
    )(x, wh, wl, fb, tri)


def _fox_kernel(q_ref, k_ref, v_ref, fq_ref, fk_ref, o_ref):
    i = pl.program_id(2)
    q = q_ref[...]
    fq = fq_ref[...]
    row = lax.broadcasted_iota(jnp.int32, (FOX_T, FOX_T), 0)
    col = lax.broadcasted_iota(jnp.int32, (FOX_T, FOX_T), 1)

    def step(kb, carry, diagonal):
        m, l, acc = carry
        ks = pl.multiple_of(kb * FOX_T, FOX_T)
        k = k_ref[pl.ds(ks, FOX_T), :]
        v = v_ref[pl.ds(ks, FOX_T), :]
        fk = fk_ref[:, pl.ds(ks, FOX_T)]
        s = lax.dot_general(q, k, (((1,), (1,)), ((), ())), preferred_element_type=F32)
        s = s + fq - fk
        if diagonal:
            s = jnp.where(col <= row, s, NEG)
        m_new = jnp.maximum(m, jnp.max(s, axis=-1, keepdims=True))
        scale = jnp.exp(m - m_new)
        p = jnp.exp(s - m_new)
        l = scale * l + jnp.sum(p, axis=-1, keepdims=True)
        acc = scale * acc + jnp.dot(p.astype(BF16), v, preferred_element_type=F32)
        return m_new, l, acc

    init = (jnp.full((FOX_T, 1), NEG, F32), jnp.zeros((FOX_T, 1), F32),
            jnp.zeros((FOX_T, B_HEAD_DIM), F32))
    carry = lax.fori_loop(0, i, lambda kb, c: step(kb, c, False), init)
    _, l, acc = step(i, carry, True)
    o_ref[...] = (acc / l).astype(o_ref.dtype)


def _fox(q, kv, f_col, f_row):
    b, s, d = q.shape
    t = FOX_T
    return pl.pallas_call(
        _fox_kernel,
        out_shape=jax.ShapeDtypeStruct((b, s, d), BF16),
        grid=(b, B_HEADS, s // t),
        in_specs=[
            pl.BlockSpec((None, t, B_HEAD_DIM), lambda bb, h, i: (bb, i, h)),
            pl.BlockSpec((None, s, B_HEAD_DIM), lambda bb, h, i: (bb, 0, h)),
            pl.BlockSpec((None, s, B_HEAD_DIM), lambda bb, h, i: (bb, 0, B_HEADS + h)),
            pl.BlockSpec((None, None, t, 1), lambda bb, h, i: (bb, h, i, 0)),
            pl.BlockSpec((None, None, 1, s), lambda bb, h, i: (bb, h, 0, 0)),
        ],
        out_specs=pl.BlockSpec((None, t, B_HEAD_DIM), lambda bb, h, i: (bb, i, h)),
        compiler_params=_params(("arbitrary", "arbitrary", "arbitrary")),
        name="mixer_b",
    )(q, kv, kv, f_col, f_row)


def _moe_layer(x1, x1b, experts, gates, w_gate, w_up, w_down, gain, bias):
    meta, tok_sorted, pos = _moe_plan(experts)
    xs = x1b[tok_sorted]
    yb = _moe(xs, w_gate, w_up, w_down, meta)
    return _combine_ln(x1, yb[pos[0]], yb[pos[1]], gates.T, gain, bias)


def kernel(x, a_w_qkv, a_rel_bias, a_w_o, kv_w, fg_w, fg_b, b_w_q, b_w_o, router_w, router_b,
           moe_w_gate, moe_w_up, moe_w_down, ln_gain, ln_bias):
    b, s, d = x.shape
    n = b * s
    assert MM_TM == PAD_ROWS and s % MM_TM == 0 and s % FG_TM == 0 and (2 * n) % MOE_BLK == 0

    rwh, rwl = _split_small_weight(router_w)
    rb = router_b.astype(F32).reshape(N_EXPERTS, 1)
    gain = ln_gain.astype(F32).reshape(DEPTH, 2, 1, d)
    bias = ln_bias.astype(F32).reshape(DEPTH, 2, 1, d)
    wg = moe_w_gate.astype(BF16)
    wu = moe_w_up.astype(BF16)
    wd = moe_w_down.astype(BF16)

    qscale = jnp.concatenate([jnp.full((d,), 1.0 / math.sqrt(A_HEAD_DIM), F32),
                              jnp.ones((2 * d,), F32)])
    w_qkv = (a_w_qkv[0] * qscale[None, :]).astype(BF16)
    qkv = _mm(x.astype(BF16), w_qkv, pad_blocks=1)
    attn = _attn_a(qkv, _rel_bias_table(a_rel_bias[0]), s)
    x1, x1b, experts, gates = _post_attn(attn.reshape(n, d), a_w_o[0].astype(BF16),
                                         x.reshape(n, d), gain[0, 0], bias[0, 0], rwh, rwl, rb)
    x2, x2b = _moe_layer(x1, x1b, experts, gates, wg[0], wu[0], wd[0], gain[0, 1], bias[0, 1])

    kv = _mm(x2b.reshape(b, s, d), kv_w.astype(BF16))
    fwh, fwl = _split_small_weight(fg_w)
    f_t = _forget_cumsum(x2.reshape(b, s, d), fwh, fwl, fg_b.astype(F32).reshape(B_HEADS, 1))
    f_row = f_t.reshape(b, B_HEADS, 1, s)
    f_col = f_t.reshape(b, B_HEADS, s, 1)

    w_q = (b_w_q[0] * (1.0 / math.sqrt(B_HEAD_DIM))).astype(BF16)
    q = _mm(x2b.reshape(b, s, d), w_q)
    attn = _fox(q, kv, f_col, f_row)
    x3, x3b, experts, gates = _post_attn(attn.reshape(n, d), b_w_o[0].astype(BF16), x2,
                                         gain[1, 0], bias[1, 0], rwh, rwl, rb)
    x4, _ = _moe_layer(x3, x3b, experts, gates, wg[1], wu[1], wd[1], gain[1, 1], bias[1, 1])
    return x4.reshape(b, s, d)
```

```python
import functools
import math

import jax
import jax.numpy as jnp
from jax import lax
from jax.experimental import pallas as pl
from jax.experimental.pallas import tpu as pltpu

F32 = jnp.float32
BF16 = jnp.bfloat16

CHUNK = 64
LEFT_CHUNKS = 8
A_HEADS = 32
A_HEAD_DIM = 64
REL_CLIP = 2 * CHUNK
B_HEADS = 16
B_HEAD_DIM = 128
N_EXPERTS = 16
N_GROUPS = 4
EXPERTS_PER_GROUP = 4
DEPTH = 2
ALPHA = (2.0 * DEPTH) ** 0.25
LN_EPS = 1e-5
NEG = -1e30

LANES = 128
VMEM_LIMIT = 56 * 1024 * 1024

MM_TM = 512
MM_TN = 1024
PAD_ROWS = LEFT_CHUNKS * CHUNK
A_TQ = 256
A_WIN = A_TQ + PAD_ROWS
POST_TM = 256
MOE_BLK = 512
MOE_TF = 512
LN_TM = 512
FOX_T = 512


def _params(sem):
    return pltpu.CompilerParams(dimension_semantics=sem, vmem_limit_bytes=VMEM_LIMIT)


def _mm_kernel(x_ref, w_ref, o_ref, *, pad_blocks):
    def compute():
        o_ref[...] = jnp.dot(x_ref[...], w_ref[...],
                             preferred_element_type=F32).astype(o_ref.dtype)

    if pad_blocks:
        i = pl.program_id(2)

        @pl.when(i < pad_blocks)
        def _():
            o_ref[...] = jnp.zeros_like(o_ref)

        pl.when(i >= pad_blocks)(compute)
    else:
        compute()


def _mm(x, w, *, pad_blocks=0):
    b, s, k = x.shape
    m = w.shape[1]
    tn = min(MM_TN, m)
    grid = (m // tn, b, s // MM_TM + pad_blocks)
    return pl.pallas_call(
        functools.partial(_mm_kernel, pad_blocks=pad_blocks),
        out_shape=jax.ShapeDtypeStruct((b, s + pad_blocks * MM_TM, m), BF16),
        grid=grid,
        in_specs=[
            pl.BlockSpec((None, MM_TM, k),
                         lambda j, bb, i: (bb, jnp.maximum(i - pad_blocks, 0), 0)),
            pl.BlockSpec((k, tn), lambda j, bb, i: (0, j)),
        ],
        out_specs=pl.BlockSpec((None, MM_TM, tn), lambda j, bb, i: (bb, i, j)),
        compiler_params=_params(("arbitrary", "arbitrary", "arbitrary")),
        name="dense_proj",
    )(x, w)


def _attn_a_kernel(q_ref, k_ref, v_ref, bias_ref, o_ref):
    i = pl.program_id(2)
    start = pl.multiple_of(i * A_TQ, A_TQ)
    kw = k_ref[pl.ds(start, A_WIN), :]
    vw = v_ref[pl.ds(start, A_WIN), :]
    q = q_ref[...]
    lane = lax.broadcasted_iota(jnp.int32, (A_TQ, LANES), 1)
    col = lax.broadcasted_iota(jnp.int32, (1, A_WIN), 1)
    keymask = jnp.where(col >= PAD_ROWS - i * A_TQ, 0.0, NEG).astype(F32)
    outs = []
    for h in range(2):
        sel = (lane < A_HEAD_DIM) if h == 0 else (lane >= A_HEAD_DIM)
        qh = jnp.where(sel, q, jnp.zeros_like(q))
        s = lax.dot_general(qh, kw, (((1,), (1,)), ((), ())), preferred_element_type=F32)
        s = s + bias_ref[h] + keymask
        m = jnp.max(s, axis=-1, keepdims=True)
        p = jnp.exp(s - m)
        l = jnp.sum(p, axis=-1, keepdims=True)
        o = jnp.dot(p.astype(BF16), vw, preferred_element_type=F32)
        outs.append(o / l)
    o_ref[...] = jnp.where(lane < A_HEAD_DIM, outs[0], outs[1]).astype(o_ref.dtype)


def _attn_a(qkv_pad, bias_tab, s):
    b = qkv_pad.shape[0]
    d = A_HEADS * A_HEAD_DIM
    pairs = d // LANES
    sp = s + PAD_ROWS
    qoff = PAD_ROWS // A_TQ
    return pl.pallas_call(
        _attn_a_kernel,
        out_shape=jax.ShapeDtypeStruct((b, s, d), BF16),
        grid=(pairs, b, s // A_TQ),
        in_specs=[
            pl.BlockSpec((None, A_TQ, LANES), lambda j, bb, i: (bb, i + qoff, j)),
            pl.BlockSpec((None, sp, LANES), lambda j, bb, i: (bb, 0, pairs + j)),
            pl.BlockSpec((None, sp, LANES), lambda j, bb, i: (bb, 0, 2 * pairs + j)),
            pl.BlockSpec((2, A_TQ, A_WIN), lambda j, bb, i: (j, 0, 0)),
        ],
        out_specs=pl.BlockSpec((None, A_TQ, LANES), lambda j, bb, i: (bb, i, j)),
        compiler_params=_params(("arbitrary", "arbitrary", "arbitrary")),
        name="mixer_a",
    )(qkv_pad, qkv_pad, qkv_pad, bias_tab)


def _rel_bias_table(rel_bias):
    qi = jnp.arange(A_TQ)[:, None]
    kj = jnp.arange(A_WIN)[None, :]
    rel = qi + PAD_ROWS - kj
    dchunk = kj // CHUNK - qi // CHUNK
    inband = (dchunk >= 0) & (dchunk <= LEFT_CHUNKS)
    idx = jnp.clip(rel, -REL_CLIP, REL_CLIP) + REL_CLIP
    tab = rel_bias.astype(F32)[:, idx]
    return jnp.where(inband[None], tab, NEG)


def _layer_norm(z, gain, bias):
    mu = jnp.mean(z, axis=-1, keepdims=True)
    zc = z - mu
    var = jnp.mean(zc * zc, axis=-1, keepdims=True)
    return zc * lax.rsqrt(var + LN_EPS) * gain + bias


def _small_proj_t(y, yb, w_ref, n_out):
    ylo = (y - yb.astype(F32)).astype(BF16)
    w = w_ref[...]
    wh = w.astype(BF16)
    wl = (w - wh.astype(F32)).astype(BF16)
    acc = jnp.dot(yb, wh, preferred_element_type=F32)
    acc = acc + jnp.dot(ylo, wh, preferred_element_type=F32)
    acc = acc + jnp.dot(yb, wl, preferred_element_type=F32)
    return acc.T[:n_out, :]


def _pad_small_weight(w):
    k, n = w.shape
    return jnp.zeros((k, LANES), F32).at[:, :n].set(w.astype(F32))


def _route(lt):
    m = jnp.max(lt, axis=0, keepdims=True)
    e = jnp.exp(lt - m)
    p = e / jnp.sum(e, axis=0, keepdims=True)
    rows = [p[k:k + 1, :] for k in range(N_EXPERTS)]
    best = None
    gi = None
    for g in range(N_GROUPS):
        r = rows[EXPERTS_PER_GROUP * g:EXPERTS_PER_GROUP * (g + 1)]
        sc = None
        for a in range(EXPERTS_PER_GROUP):
            for c in range(a + 1, EXPERTS_PER_GROUP):
                pair = r[a] + r[c]
                sc = pair if sc is None else jnp.maximum(sc, pair)
        if g == 0:
            best = sc
            gi = jnp.zeros(sc.shape, jnp.int32)
        else:
            upd = sc > best
            best = jnp.where(upd, sc, best)
            gi = jnp.where(upd, g, gi)
    pin = []
    for k in range(EXPERTS_PER_GROUP):
        v = rows[(N_GROUPS - 1) * EXPERTS_PER_GROUP + k]
        for g in range(N_GROUPS - 2, -1, -1):
            v = jnp.where(gi == g, rows[g * EXPERTS_PER_GROUP + k], v)
        pin.append(v)
    v1 = pin[0]
    i1 = jnp.zeros(v1.shape, jnp.int32)
    for k in range(1, EXPERTS_PER_GROUP):
        upd = pin[k] > v1
        v1 = jnp.where(upd, pin[k], v1)
        i1 = jnp.where(upd, k, i1)
    v2 = jnp.full(v1.shape, -1.0, F32)
    i2 = jnp.zeros(v1.shape, jnp.int32)
    for k in range(EXPERTS_PER_GROUP):
        cand = jnp.where(i1 == k, -1.0, pin[k])
        upd = cand > v2
        v2 = jnp.where(upd, cand, v2)
        i2 = jnp.where(upd, k, i2)
    tot = v1 + v2
    base = gi * EXPERTS_PER_GROUP
    experts = jnp.concatenate([base + i1, base + i2], axis=0)
    gates = jnp.concatenate([v1 / tot, v2 / tot], axis=0)
    return experts, gates


def _post_attn_kernel(a_ref, w_ref, x_ref, g_ref, b_ref, rw_ref, rb_ref,
                      x1_ref, x1b_ref, ex_ref, gt_ref):
    h = jnp.dot(a_ref[...], w_ref[...], preferred_element_type=F32)
    y = _layer_norm(ALPHA * x_ref[...] + h, g_ref[...], b_ref[...])
    yb = y.astype(BF16)
    x1_ref[...] = y
    x1b_ref[...] = yb
    lt = _small_proj_t(y, yb, rw_ref, N_EXPERTS) + rb_ref[...]
    experts, gates = _route(lt)
    ex_ref[...] = experts
    gt_ref[...] = gates


def _post_attn(attn, w_o, x, gain, bias, rw, rb):
    n, d = x.shape
    tm = POST_TM
    row = lambda i: (i, 0)
    const = lambda i: (0, 0)
    return pl.pallas_call(
        _post_attn_kernel,
        out_shape=(jax.ShapeDtypeStruct((n, d), F32),
                   jax.ShapeDtypeStruct((n, d), BF16),
                   jax.ShapeDtypeStruct((2, n), jnp.int32),
                   jax.ShapeDtypeStruct((2, n), F32)),
        grid=(n // tm,),
        in_specs=[
            pl.BlockSpec((tm, d), row),
            pl.BlockSpec((d, d), const),
            pl.BlockSpec((tm, d), row),
            pl.BlockSpec((1, d), const),
            pl.BlockSpec((1, d), const),
            pl.BlockSpec((d, LANES), const),
            pl.BlockSpec((N_EXPERTS, 1), const),
        ],
        out_specs=(pl.BlockSpec((tm, d), row),
                   pl.BlockSpec((tm, d), row),
                   pl.BlockSpec((2, tm), lambda i: (0, i)),
                   pl.BlockSpec((2, tm), lambda i: (0, i))),
        compiler_params=_params(("arbitrary",)),
        name="out_proj_ln_router",
    )(attn, w_o, x, gain, bias, rw, rb)


def _moe_kernel(iblk, iex, ivalid, ifirst, starts, ends, x_ref, wg_ref, wu_ref, wd_ref, o_ref):
    w = pl.program_id(0)
    f = pl.program_id(1)

    @pl.when((ifirst[w] == 1) & (f == 0))
    def _():
        o_ref[...] = jnp.zeros_like(o_ref)

    @pl.when(ivalid[w] == 1)
    def _():
        x = x_ref[...]
        hg = jnp.dot(x, wg_ref[...], preferred_element_type=F32)
        hu = jnp.dot(x, wu_ref[...], preferred_element_type=F32)
        e = iex[w]
        row = iblk[w] * MOE_BLK + lax.broadcasted_iota(jnp.int32, (MOE_BLK, 1), 0)
        mine = (row >= starts[e]) & (row < ends[e])
        hid = jnp.where(mine, hg * jax.nn.sigmoid(hg) * hu, 0.0)
        o_ref[...] += jnp.dot(hid.astype(BF16), wd_ref[...], preferred_element_type=F32)


def _moe(xs, w_gate, w_up, w_down, meta):
    a, d = xs.shape
    dff = w_gate.shape[-1]
    nf = dff // MOE_TF
    n_items = meta[0].shape[0]

    def fsel(w, f, ivalid):
        return jnp.where(ivalid[w] == 1, f, nf - 1)

    grid_spec = pltpu.PrefetchScalarGridSpec(
        num_scalar_prefetch=6,
        grid=(n_items, nf),
        in_specs=[
            pl.BlockSpec((MOE_BLK, d), lambda w, f, ib, ie, iv, i1, st, en: (ib[w], 0)),
            pl.BlockSpec((None, d, MOE_TF),
                         lambda w, f, ib, ie, iv, i1, st, en: (ie[w], 0, fsel(w, f, iv))),
            pl.BlockSpec((None, d, MOE_TF),
                         lambda w, f, ib, ie, iv, i1, st, en: (ie[w], 0, fsel(w, f, iv))),
            pl.BlockSpec((None, MOE_TF, d),
                         lambda w, f, ib, ie, iv, i1, st, en: (ie[w], fsel(w, f, iv), 0)),
        ],
        out_specs=pl.BlockSpec((MOE_BLK, d), lambda w, f, ib, ie, iv, i1, st, en: (ib[w], 0)),
    )
    return pl.pallas_call(
        _moe_kernel,
        out_shape=jax.ShapeDtypeStruct((a, d), F32),
        grid_spec=grid_spec,
        compiler_params=_params(("arbitrary", "arbitrary")),
        name="moe_experts",
    )(*meta, xs, w_gate, w_up, w_down)


def _moe_plan(experts):
    n = experts.shape[1]
    a = 2 * n
    nblk = a // MOE_BLK
    n_items = nblk + N_EXPERTS - 1
    e_flat = experts.reshape(-1)
    order = jnp.argsort(e_flat, stable=True).astype(jnp.int32)
    e_sorted = e_flat[order]
    tok_sorted = order % n
    pos = jnp.zeros((a,), jnp.int32).at[order].set(jnp.arange(a, dtype=jnp.int32)).reshape(2, n)
    counts = jnp.zeros((N_EXPERTS,), jnp.int32).at[e_flat].add(1)
    ends = jnp.cumsum(counts).astype(jnp.int32)
    starts = ends - counts
    e_lo = e_sorted[0::MOE_BLK]
    e_hi = e_sorted[MOE_BLK - 1::MOE_BLK]
    per_blk = e_hi - e_lo + 1
    iend = jnp.cumsum(per_blk).astype(jnp.int32)
    istart = iend - per_blk
    total = iend[-1]
    w = jnp.arange(n_items, dtype=jnp.int32)
    blk = jnp.clip(jnp.searchsorted(iend, w, side="right"), 0, nblk - 1).astype(jnp.int32)
    ex = e_lo[blk] + (w - istart[blk])
    valid = w < total
    blk = jnp.where(valid, blk, nblk - 1)
    ex = jnp.where(valid, ex, e_hi[nblk - 1])
    first = valid & (w == istart[blk])
    meta = (blk, ex.astype(jnp.int32), valid.astype(jnp.int32), first.astype(jnp.int32),
            starts, ends)
    return meta, tok_sorted, pos


def _combine_ln_kernel(x_ref, y0_ref, y1_ref, gt_ref, g_ref, b_ref, o_ref, ob_ref):
    gt = gt_ref[...]
    y = y0_ref[...] * gt[:, 0:1] + y1_ref[...] * gt[:, 1:2]
    out = _layer_norm(ALPHA * x_ref[...] + y, g_ref[...], b_ref[...])
    o_ref[...] = out
    ob_ref[...] = out.astype(BF16)


def _combine_ln(x, y0, y1, gates_t, gain, bias):
    n, d = x.shape
    tm = LN_TM
    row = lambda i: (i, 0)
    const = lambda i: (0, 0)
    return pl.pallas_call(
        _combine_ln_kernel,
        out_shape=(jax.ShapeDtypeStruct((n, d), F32), jax.ShapeDtypeStruct((n, d), BF16)),
        grid=(n // tm,),
        in_specs=[
            pl.BlockSpec((tm, d), row),
            pl.BlockSpec((tm, d), row),
            pl.BlockSpec((tm, d), row),
            pl.BlockSpec((tm, 2), row),
            pl.BlockSpec((1, d), const),
            pl.BlockSpec((1, d), const),
        ],
        out_specs=(pl.BlockSpec((tm, d), row), pl.BlockSpec((tm, d), row)),
        compiler_params=_params(("arbitrary",)),
        name="moe_combine_ln",
    )(x, y0, y1, gates_t, gain, bias)


def _fg_kernel(x_ref, w_ref, b_ref, tri_ref, qf_ref, kf_ref, c_ref, carry_ref):
    i = pl.program_id(1)

    @pl.when(i == 0)
    def _():
        carry_ref[...] = jnp.zeros_like(carry_ref)

    def split3(v):
        hi = v.astype(BF16)
        r1 = v - hi.astype(F32)
        mid = r1.astype(BF16)
        lo = (r1 - mid.astype(F32)).astype(BF16)
        return hi, mid, lo

    y = x_ref[...]
    yb = y.astype(BF16)
    ylo = (y - yb.astype(F32)).astype(BF16)
    w = w_ref[...]
    wh = w.astype(BF16)
    wl = (w - wh.astype(F32)).astype(BF16)
    z = (jnp.dot(yb, wh, preferred_element_type=F32) + jnp.dot(ylo, wh, preferred_element_type=F32)
         + jnp.dot(yb, wl, preferred_element_type=F32)) + b_ref[...]
    lf = jnp.minimum(z, 0.0) - jnp.log1p(jnp.exp(-jnp.abs(z)))
    parts = jnp.concatenate(split3(lf), axis=1)
    cs3 = jnp.dot(tri_ref[...], parts, preferred_element_type=F32)
    cs = cs3[:, 0:LANES] + cs3[:, LANES:2 * LANES] + cs3[:, 2 * LANES:3 * LANES]
    c_ref[...] = carry_ref[...]
    carry_ref[...] = carry_ref[...] + cs[FOX_T - 1:FOX_T, :]

    hi, mid, lo = (p.astype(F32) for p in split3(cs))
    lane = lax.broadcasted_iota(jnp.int32, (FOX_T, LANES), 1)
    q_ones = jnp.where((lane >= 3) & (lane < 6), 1.0, 0.0).astype(F32)
    k_ones = jnp.where(lane < 3, 1.0, 0.0).astype(F32)
    for h in range(B_HEADS):
        hc, mc, lc = hi[:, h:h + 1], mid[:, h:h + 1], lo[:, h:h + 1]
        tq = jnp.where(lane == 0, hc, jnp.where(lane == 1, mc, jnp.where(lane == 2, lc, q_ones)))
        tk = jnp.where(lane == 3, -hc, jnp.where(lane == 4, -mc, jnp.where(lane == 5, -lc, k_ones)))
        qf_ref[h] = tq.astype(BF16)
        kf_ref[h] = tk.astype(BF16)


def _forget_columns(x, fw, fb):
    b, s, d = x.shape
    t = FOX_T
    nblk = s // t
    tri = (jnp.arange(t)[:, None] >= jnp.arange(t)[None, :]).astype(BF16)
    const = lambda bb, i: (0, 0)
    cols = jax.ShapeDtypeStruct((b, B_HEADS, s, LANES), BF16)
    col_spec = pl.BlockSpec((None, B_HEADS, t, LANES), lambda bb, i: (bb, 0, i, 0))
    return pl.pallas_call(
        _fg_kernel,
        out_shape=(cols, cols, jax.ShapeDtypeStruct((b, nblk, 1, LANES), F32)),
        grid=(b, nblk),
        in_specs=[
            pl.BlockSpec((None, t, d), lambda bb, i: (bb, i, 0)),
            pl.BlockSpec((d, LANES), const),
            pl.BlockSpec((1, LANES), const),
            pl.BlockSpec((t, t), const),
        ],
        out_specs=(col_spec, col_spec,
                   pl.BlockSpec((None, None, 1, LANES), lambda bb, i: (bb, i, 0, 0))),
        scratch_shapes=[pltpu.VMEM((1, LANES), F32)],
        compiler_params=_params(("arbitrary", "arbitrary")),
        name="forget_cumsum",
    )(x, fw, fb, tri)


def _fox_kernel(c_tab, q_ref, qf_ref, k_ref, kf_ref, v_ref, o_ref,
                qa_ref, sa_ref, sb_ref, m_ref, l_ref, acc_ref):
    i = pl.program_id(2)
    t = FOX_T
    cbase = (pl.program_id(0) * B_HEADS + pl.program_id(1)) * pl.num_programs(2)
    c_q = jnp.full((1, LANES), c_tab[cbase + i], F32)
    qa_ref[:, 0:B_HEAD_DIM] = q_ref[...]
    qa_ref[:, B_HEAD_DIM:2 * B_HEAD_DIM] = qf_ref[...]
    m_ref[...] = jnp.full(m_ref.shape, NEG, F32)
    l_ref[...] = jnp.zeros_like(l_ref)
    acc_ref[...] = jnp.zeros_like(acc_ref)

    def scores(kb, dst_ref):
        ks = pl.multiple_of(kb * t, t)
        ka = jnp.concatenate([k_ref[pl.ds(ks, t), :], kf_ref[pl.ds(ks, t), :]], axis=1)
        dst_ref[...] = lax.dot_general(qa_ref[...], ka, (((1,), (1,)), ((), ())),
                                       preferred_element_type=F32)

    def absorb(src_ref, kb, diagonal):
        ks = pl.multiple_of(kb * t, t)
        s = src_ref[...]
        if diagonal:
            row = lax.broadcasted_iota(jnp.int32, (t, t), 0)
            col = lax.broadcasted_iota(jnp.int32, (t, t), 1)
            s = jnp.where(col <= row, s, NEG)
        shift = c_q - jnp.full((1, LANES), c_tab[cbase + kb], F32)
        m_prev = m_ref[...]
        m_new = jnp.maximum(m_prev, jnp.max(s, axis=-1, keepdims=True) + shift)
        p = jnp.exp(s - pltpu.repeat(m_new - shift, t // LANES, axis=1))
        scale = jnp.exp(m_prev - m_new)
        l_ref[...] = scale * l_ref[...] + jnp.sum(p, axis=-1, keepdims=True)
        acc_ref[...] = scale * acc_ref[...] + jnp.dot(
            p.astype(BF16), v_ref[pl.ds(ks, t), :], preferred_element_type=F32)
        m_ref[...] = m_new

    scores(0, sa_ref)

    def pair(jj, carry):
        scores(2 * jj + 1, sb_ref)
        absorb(sa_ref, 2 * jj, False)
        scores(2 * jj + 2, sa_ref)
        absorb(sb_ref, 2 * jj + 1, False)
        return carry

    lax.fori_loop(0, lax.shift_right_logical(i, 1), pair, 0)

    @pl.when((i & 1) == 1)
    def _():
        scores(i, sb_ref)
        absorb(sa_ref, i - 1, False)
        absorb(sb_ref, i, True)

    @pl.when((i & 1) == 0)
    def _():
        absorb(sa_ref, i, True)

    o_ref[...] = (acc_ref[...] / l_ref[...]).astype(o_ref.dtype)


def _fox(q, kv, qf, kf, c_tab):
    b, s, d = q.shape
    t = FOX_T
    hd = B_HEAD_DIM
    grid_spec = pltpu.PrefetchScalarGridSpec(
        num_scalar_prefetch=1,
        grid=(b, B_HEADS, s // t),
        in_specs=[
            pl.BlockSpec((None, t, hd), lambda bb, h, i, c: (bb, i, h)),
            pl.BlockSpec((None, None, t, LANES), lambda bb, h, i, c: (bb, h, i, 0)),
            pl.BlockSpec((None, s, hd), lambda bb, h, i, c: (bb, 0, h)),
            pl.BlockSpec((None, None, s, LANES), lambda bb, h, i, c: (bb, h, 0, 0)),
            pl.BlockSpec((None, s, hd), lambda bb, h, i, c: (bb, 0, B_HEADS + h)),
        ],
        out_specs=pl.BlockSpec((None, t, hd), lambda bb, h, i, c: (bb, i, h)),
        scratch_shapes=[
            pltpu.VMEM((t, 2 * hd), BF16),
            pltpu.VMEM((t, t), F32),
            pltpu.VMEM((t, t), F32),
            pltpu.VMEM((t, LANES), F32),
            pltpu.VMEM((t, LANES), F32),
            pltpu.VMEM((t, hd), F32),
        ],
    )
    return pl.pallas_call(
        _fox_kernel,
        out_shape=jax.ShapeDtypeStruct((b, s, d), BF16),
        grid_spec=grid_spec,
        compiler_params=_params(("arbitrary", "arbitrary", "arbitrary")),
        name="mixer_b",
    )(c_tab, q, qf, kv, kf, kv)


def _moe_layer(x1, x1b, experts, gates, w_gate, w_up, w_down, gain, bias):
    meta, tok_sorted, pos = _moe_plan(experts)
    xs = x1b[tok_sorted]
    yb = _moe(xs, w_gate, w_up, w_down, meta)
    return _combine_ln(x1, yb[pos[0]], yb[pos[1]], gates.T, gain, bias)


def kernel(x, a_w_qkv, a_rel_bias, a_w_o, kv_w, fg_w, fg_b, b_w_q, b_w_o, router_w, router_b,
           moe_w_gate, moe_w_up, moe_w_down, ln_gain, ln_bias):
    b, s, d = x.shape
    n = b * s
    assert MM_TM == PAD_ROWS and s % MM_TM == 0 and s % FOX_T == 0 and (2 * n) % MOE_BLK == 0

    rw = _pad_small_weight(router_w)
    rb = router_b.astype(F32).reshape(N_EXPERTS, 1)
    gain = ln_gain.astype(F32).reshape(DEPTH, 2, 1, d)
    bias = ln_bias.astype(F32).reshape(DEPTH, 2, 1, d)
    wg = moe_w_gate.astype(BF16)
    wu = moe_w_up.astype(BF16)
    wd = moe_w_down.astype(BF16)

    qscale = jnp.concatenate([jnp.full((d,), 1.0 / math.sqrt(A_HEAD_DIM), F32),
                              jnp.ones((2 * d,), F32)])
    w_qkv = (a_w_qkv[0] * qscale[None, :]).astype(BF16)
    qkv = _mm(x.astype(BF16), w_qkv, pad_blocks=1)
    attn = _attn_a(qkv, _rel_bias_table(a_rel_bias[0]), s)
    x1, x1b, experts, gates = _post_attn(attn.reshape(n, d), a_w_o[0].astype(BF16),
                                         x.reshape(n, d), gain[0, 0], bias[0, 0], rw, rb)
    x2, x2b = _moe_layer(x1, x1b, experts, gates, wg[0], wu[0], wd[0], gain[0, 1], bias[0, 1])

    kv = _mm(x2b.reshape(b, s, d), kv_w.astype(BF16))
    fb = jnp.zeros((1, LANES), F32).at[0, :B_HEADS].set(fg_b.astype(F32))
    qf, kf, carry_in = _forget_columns(x2.reshape(b, s, d), _pad_small_weight(fg_w), fb)
    c_tab = carry_in[:, :, 0, :B_HEADS].transpose(0, 2, 1).reshape(-1)

    w_q = (b_w_q[0] * (1.0 / math.sqrt(B_HEAD_DIM))).astype(BF16)
    q = _mm(x2b.reshape(b, s, d), w_q)
    attn = _fox(q, kv, qf, kf, c_tab)
    x3, x3b, experts, gates = _post_attn(attn.reshape(n, d), b_w_o[0].astype(BF16), x2,
                                         gain[1, 0], bias[1, 0], rw, rb)
    x4, _ = _moe_layer(x3, x3b, experts, gates, wg[1], wu[1], wd[1], gain[1, 1], bias[1, 1])
    return x4.reshape(b, s, d)
```

```python
import functools
import math

import jax
import jax.numpy as jnp
from jax import lax
from jax.experimental import pallas as pl
from jax.experimental.pallas import tpu as pltpu

F32 = jnp.float32
BF16 = jnp.bfloat16

CHUNK = 64
LEFT_CHUNKS = 8
A_HEADS = 32
A_HEAD_DIM = 64
REL_CLIP = 2 * CHUNK
B_HEADS = 16
B_HEAD_DIM = 128
N_EXPERTS = 16
N_GROUPS = 4
EXPERTS_PER_GROUP = 4
DEPTH = 2
ALPHA = (2.0 * DEPTH) ** 0.25
LN_EPS = 1e-5
NEG = -1e30
LOG2E = 1.4426950408889634

LANES = 128
VMEM_LIMIT = 56 * 1024 * 1024

MM_TM = 512
MM_TN = 1024
PAD_ROWS = LEFT_CHUNKS * CHUNK
A_TQ = 256
A_WIN = A_TQ + PAD_ROWS
POST_TM = 256
MOE_BLK = 512
MOE_TF = 512
LN_TM = 512
FOX_T = 512
FOX_TK = 2 * FOX_T


def _params(sem):
    return pltpu.CompilerParams(dimension_semantics=sem, vmem_limit_bytes=VMEM_LIMIT)


def _mm_kernel(x_ref, w_ref, o_ref, *, pad_blocks):
    def compute():
        o_ref[...] = jnp.dot(x_ref[...], w_ref[...],
                             preferred_element_type=F32).astype(o_ref.dtype)

    if pad_blocks:
        i = pl.program_id(2)

        @pl.when(i < pad_blocks)
        def _():
            o_ref[...] = jnp.zeros_like(o_ref)

        pl.when(i >= pad_blocks)(compute)
    else:
        compute()


def _mm(x, w, *, pad_blocks=0):
    b, s, k = x.shape
    m = w.shape[1]
    tn = min(MM_TN, m)
    grid = (m // tn, b, s // MM_TM + pad_blocks)
    return pl.pallas_call(
        functools.partial(_mm_kernel, pad_blocks=pad_blocks),
        out_shape=jax.ShapeDtypeStruct((b, s + pad_blocks * MM_TM, m), BF16),
        grid=grid,
        in_specs=[
            pl.BlockSpec((None, MM_TM, k),
                         lambda j, bb, i: (bb, jnp.maximum(i - pad_blocks, 0), 0)),
            pl.BlockSpec((k, tn), lambda j, bb, i: (0, j)),
        ],
        out_specs=pl.BlockSpec((None, MM_TM, tn), lambda j, bb, i: (bb, i, j)),
        compiler_params=_params(("arbitrary", "arbitrary", "arbitrary")),
        name="dense_proj",
    )(x, w)


def _attn_a_kernel(q_ref, k_ref, v_ref, bias_ref, o_ref):
    i = pl.program_id(2)
    start = pl.multiple_of(i * A_TQ, A_TQ)
    kw = k_ref[pl.ds(start, A_WIN), :]
    vw = v_ref[pl.ds(start, A_WIN), :]
    q = q_ref[...]
    lane = lax.broadcasted_iota(jnp.int32, (A_TQ, LANES), 1)
    col = lax.broadcasted_iota(jnp.int32, (1, A_WIN), 1)
    keymask = jnp.where(col >= PAD_ROWS - i * A_TQ, 0.0, NEG).astype(F32)
    outs = []
    for h in range(2):
        sel = (lane < A_HEAD_DIM) if h == 0 else (lane >= A_HEAD_DIM)
        qh = jnp.where(sel, q, jnp.zeros_like(q))
        s = lax.dot_general(qh, kw, (((1,), (1,)), ((), ())), preferred_element_type=F32)
        s = s + bias_ref[h] + keymask
        m = jnp.max(s, axis=-1, keepdims=True)
        p = jnp.exp(s - m)
        l = jnp.sum(p, axis=-1, keepdims=True)
        o = jnp.dot(p.astype(BF16), vw, preferred_element_type=F32)
        outs.append(o / l)
    o_ref[...] = jnp.where(lane < A_HEAD_DIM, outs[0], outs[1]).astype(o_ref.dtype)


def _attn_a(qkv_pad, bias_tab, s):
    b = qkv_pad.shape[0]
    d = A_HEADS * A_HEAD_DIM
    pairs = d // LANES
    sp = s + PAD_ROWS
    qoff = PAD_ROWS // A_TQ
    return pl.pallas_call(
        _attn_a_kernel,
        out_shape=jax.ShapeDtypeStruct((b, s, d), BF16),
        grid=(pairs, b, s // A_TQ),
        in_specs=[
            pl.BlockSpec((None, A_TQ, LANES), lambda j, bb, i: (bb, i + qoff, j)),
            pl.BlockSpec((None, sp, LANES), lambda j, bb, i: (bb, 0, pairs + j)),
            pl.BlockSpec((None, sp, LANES), lambda j, bb, i: (bb, 0, 2 * pairs + j)),
            pl.BlockSpec((2, A_TQ, A_WIN), lambda j, bb, i: (j, 0, 0)),
        ],
        out_specs=pl.BlockSpec((None, A_TQ, LANES), lambda j, bb, i: (bb, i, j)),
        compiler_params=_params(("arbitrary", "arbitrary", "arbitrary")),
        name="mixer_a",
    )(qkv_pad, qkv_pad, qkv_pad, bias_tab)


def _rel_bias_table(rel_bias):
    qi = jnp.arange(A_TQ)[:, None]
    kj = jnp.arange(A_WIN)[None, :]
    dchunk = kj // CHUNK - qi // CHUNK
    inband = (dchunk >= 0) & (dchunk <= LEFT_CHUNKS)
    period = A_TQ + A_WIN
    u = jnp.arange(period)
    u = jnp.where(u < A_WIN, u, u - period)
    idx = jnp.clip(PAD_ROWS - u, -REL_CLIP, REL_CLIP) + REL_CLIP
    g = rel_bias.astype(F32)[:, idx]
    flat = jnp.tile(g, (1, A_TQ))[:, :A_TQ * (period - 1)]
    tab = flat.reshape(-1, A_TQ, period - 1)[:, :, :A_WIN]
    return jnp.where(inband[None], tab, NEG)


def _layer_norm(z, gain, bias):
    mu = jnp.mean(z, axis=-1, keepdims=True)
    zc = z - mu
    var = jnp.mean(zc * zc, axis=-1, keepdims=True)
    return zc * lax.rsqrt(var + LN_EPS) * gain + bias


def _small_proj_t(y, yb, w_ref, n_out):
    ylo = (y - yb.astype(F32)).astype(BF16)
    w = w_ref[...]
    wh = w.astype(BF16)
    wl = (w - wh.astype(F32)).astype(BF16)
    acc = jnp.dot(yb, wh, preferred_element_type=F32)
    acc = acc + jnp.dot(ylo, wh, preferred_element_type=F32)
    acc = acc + jnp.dot(yb, wl, preferred_element_type=F32)
    return acc.T[:n_out, :]


def _pad_small_weight(w):
    k, n = w.shape
    return jnp.zeros((k, LANES), F32).at[:, :n].set(w.astype(F32))


def _route(lt):
    m = jnp.max(lt, axis=0, keepdims=True)
    e = jnp.exp(lt - m)
    p = e / jnp.sum(e, axis=0, keepdims=True)
    rows = [p[k:k + 1, :] for k in range(N_EXPERTS)]
    best = None
    gi = None
    for g in range(N_GROUPS):
        r = rows[EXPERTS_PER_GROUP * g:EXPERTS_PER_GROUP * (g + 1)]
        sc = None
        for a in range(EXPERTS_PER_GROUP):
            for c in range(a + 1, EXPERTS_PER_GROUP):
                pair = r[a] + r[c]
                sc = pair if sc is None else jnp.maximum(sc, pair)
        if g == 0:
            best = sc
            gi = jnp.zeros(sc.shape, jnp.int32)
        else:
            upd = sc > best
            best = jnp.where(upd, sc, best)
            gi = jnp.where(upd, g, gi)
    pin = []
    for k in range(EXPERTS_PER_GROUP):
        v = rows[(N_GROUPS - 1) * EXPERTS_PER_GROUP + k]
        for g in range(N_GROUPS - 2, -1, -1):
            v = jnp.where(gi == g, rows[g * EXPERTS_PER_GROUP + k], v)
        pin.append(v)
    v1 = pin[0]
    i1 = jnp.zeros(v1.shape, jnp.int32)
    for k in range(1, EXPERTS_PER_GROUP):
        upd = pin[k] > v1
        v1 = jnp.where(upd, pin[k], v1)
        i1 = jnp.where(upd, k, i1)
    v2 = jnp.full(v1.shape, -1.0, F32)
    i2 = jnp.zeros(v1.shape, jnp.int32)
    for k in range(EXPERTS_PER_GROUP):
        cand = jnp.where(i1 == k, -1.0, pin[k])
        upd = cand > v2
        v2 = jnp.where(upd, cand, v2)
        i2 = jnp.where(upd, k, i2)
    tot = v1 + v2
    base = gi * EXPERTS_PER_GROUP
    experts = jnp.concatenate([base + i1, base + i2], axis=0)
    gates = jnp.concatenate([v1 / tot, v2 / tot], axis=0)
    return experts, gates


def _post_attn_kernel(a_ref, w_ref, x_ref, g_ref, b_ref, rw_ref, rb_ref,
                      x1_ref, x1b_ref, ex_ref, gt_ref):
    h = jnp.dot(a_ref[...], w_ref[...], preferred_element_type=F32)
    y = _layer_norm(ALPHA * x_ref[...] + h, g_ref[...], b_ref[...])
    yb = y.astype(BF16)
    x1_ref[...] = y
    x1b_ref[...] = yb
    lt = _small_proj_t(y, yb, rw_ref, N_EXPERTS) + rb_ref[...]
    experts, gates = _route(lt)
    ex_ref[...] = experts
    gt_ref[...] = gates


def _post_attn(attn, w_o, x, gain, bias, rw, rb):
    n, d = x.shape
    tm = POST_TM
    row = lambda i: (i, 0)
    const = lambda i: (0, 0)
    return pl.pallas_call(
        _post_attn_kernel,
        out_shape=(jax.ShapeDtypeStruct((n, d), F32),
                   jax.ShapeDtypeStruct((n, d), BF16),
                   jax.ShapeDtypeStruct((2, n), jnp.int32),
                   jax.ShapeDtypeStruct((2, n), F32)),
        grid=(n // tm,),
        in_specs=[
            pl.BlockSpec((tm, d), row),
            pl.BlockSpec((d, d), const),
            pl.BlockSpec((tm, d), row),
            pl.BlockSpec((1, d), const),
            pl.BlockSpec((1, d), const),
            pl.BlockSpec((d, LANES), const),
            pl.BlockSpec((N_EXPERTS, 1), const),
        ],
        out_specs=(pl.BlockSpec((tm, d), row),
                   pl.BlockSpec((tm, d), row),
                   pl.BlockSpec((2, tm), lambda i: (0, i)),
                   pl.BlockSpec((2, tm), lambda i: (0, i))),
        compiler_params=_params(("arbitrary",)),
        name="out_proj_ln_router",
    )(attn, w_o, x, gain, bias, rw, rb)


def _moe_kernel(iblk, iex, ivalid, ifirst, starts, ends, x_ref, wg_ref, wu_ref, wd_ref, o_ref):
    w = pl.program_id(0)
    f = pl.program_id(1)

    @pl.when((ifirst[w] == 1) & (f == 0))
    def _():
        o_ref[...] = jnp.zeros_like(o_ref)

    @pl.when(ivalid[w] == 1)
    def _():
        x = x_ref[...]
        hg = jnp.dot(x, wg_ref[...], preferred_element_type=F32)
        hu = jnp.dot(x, wu_ref[...], preferred_element_type=F32)
        e = iex[w]
        row = iblk[w] * MOE_BLK + lax.broadcasted_iota(jnp.int32, (MOE_BLK, 1), 0)
        mine = (row >= starts[e]) & (row < ends[e])
        hid = jnp.where(mine, hg * jax.nn.sigmoid(hg) * hu, 0.0)
        o_ref[...] += jnp.dot(hid.astype(BF16), wd_ref[...], preferred_element_type=F32)


def _moe(xs, w_gate, w_up, w_down, meta):
    a, d = xs.shape
    dff = w_gate.shape[-1]
    nf = dff // MOE_TF
    n_items = meta[0].shape[0]

    def fsel(w, f, ivalid):
        return jnp.where(ivalid[w] == 1, f, nf - 1)

    grid_spec = pltpu.PrefetchScalarGridSpec(
        num_scalar_prefetch=6,
        grid=(n_items, nf),
        in_specs=[
            pl.BlockSpec((MOE_BLK, d), lambda w, f, ib, ie, iv, i1, st, en: (ib[w], 0)),
            pl.BlockSpec((None, d, MOE_TF),
                         lambda w, f, ib, ie, iv, i1, st, en: (ie[w], 0, fsel(w, f, iv))),
            pl.BlockSpec((None, d, MOE_TF),
                         lambda w, f, ib, ie, iv, i1, st, en: (ie[w], 0, fsel(w, f, iv))),
            pl.BlockSpec((None, MOE_TF, d),
                         lambda w, f, ib, ie, iv, i1, st, en: (ie[w], fsel(w, f, iv), 0)),
        ],
        out_specs=pl.BlockSpec((MOE_BLK, d), lambda w, f, ib, ie, iv, i1, st, en: (ib[w], 0)),
    )
    return pl.pallas_call(
        _moe_kernel,
        out_shape=jax.ShapeDtypeStruct((a, d), F32),
        grid_spec=grid_spec,
        compiler_params=_params(("arbitrary", "arbitrary")),
        name="moe_experts",
    )(*meta, xs, w_gate, w_up, w_down)


def _moe_plan(experts):
    n = experts.shape[1]
    a = 2 * n
    nblk = a // MOE_BLK
    n_items = nblk + N_EXPERTS - 1
    e_flat = experts.reshape(-1)
    iota = jnp.arange(a, dtype=jnp.int32)
    e_sorted, order = lax.sort((e_flat, iota), num_keys=1, is_stable=True)
    tok_sorted = order % n
    _, pos = lax.sort((order, iota), num_keys=1)
    pos = pos.reshape(2, n)
    eids = jnp.arange(N_EXPERTS, dtype=jnp.int32)
    starts = jnp.searchsorted(e_sorted, eids, side="left").astype(jnp.int32)
    ends = jnp.searchsorted(e_sorted, eids, side="right").astype(jnp.int32)
    e_lo = e_sorted[0::MOE_BLK]
    e_hi = e_sorted[MOE_BLK - 1::MOE_BLK]
    per_blk = e_hi - e_lo + 1
    iend = jnp.cumsum(per_blk).astype(jnp.int32)
    istart = iend - per_blk
    total = iend[-1]
    w = jnp.arange(n_items, dtype=jnp.int32)
    blk = jnp.clip(jnp.searchsorted(iend, w, side="right"), 0, nblk - 1).astype(jnp.int32)
    ex = e_lo[blk] + (w - istart[blk])
    valid = w < total
    blk = jnp.where(valid, blk, nblk - 1)
    ex = jnp.where(valid, ex, e_hi[nblk - 1])
    first = valid & (w == istart[blk])
    meta = (blk, ex.astype(jnp.int32), valid.astype(jnp.int32), first.astype(jnp.int32),
            starts, ends)
    return meta, tok_sorted, pos


def _combine_ln_kernel(x_ref, y0_ref, y1_ref, gt_ref, g_ref, b_ref, o_ref, ob_ref):
    gt = gt_ref[...]
    y = y0_ref[...] * gt[:, 0:1] + y1_ref[...] * gt[:, 1:2]
    out = _layer_norm(ALPHA * x_ref[...] + y, g_ref[...], b_ref[...])
    o_ref[...] = out
    ob_ref[...] = out.astype(BF16)


def _combine_ln(x, y0, y1, gates_t, gain, bias):
    n, d = x.shape
    tm = LN_TM
    row = lambda i: (i, 0)
    const = lambda i: (0, 0)
    return pl.pallas_call(
        _combine_ln_kernel,
        out_shape=(jax.ShapeDtypeStruct((n, d), F32), jax.ShapeDtypeStruct((n, d), BF16)),
        grid=(n // tm,),
        in_specs=[
            pl.BlockSpec((tm, d), row),
            pl.BlockSpec((tm, d), row),
            pl.BlockSpec((tm, d), row),
            pl.BlockSpec((tm, 2), row),
            pl.BlockSpec((1, d), const),
            pl.BlockSpec((1, d), const),
        ],
        out_specs=(pl.BlockSpec((tm, d), row), pl.BlockSpec((tm, d), row)),
        compiler_params=_params(("arbitrary",)),
        name="moe_combine_ln",
    )(x, y0, y1, gates_t, gain, bias)


def _fg_kernel(x_ref, w_ref, b_ref, tri_ref, qf_ref, kf_ref, c_ref, carry_ref, koff_ref):
    i = pl.program_id(1)

    @pl.when(i == 0)
    def _():
        carry_ref[...] = jnp.zeros_like(carry_ref)

    @pl.when(i % (FOX_TK // FOX_T) == 0)
    def _():
        koff_ref[...] = jnp.zeros_like(koff_ref)

    def split3(v):
        hi = v.astype(BF16)
        r1 = v - hi.astype(F32)
        mid = r1.astype(BF16)
        lo = (r1 - mid.astype(F32)).astype(BF16)
        return hi, mid, lo

    y = x_ref[...]
    yb = y.astype(BF16)
    ylo = (y - yb.astype(F32)).astype(BF16)
    w = w_ref[...]
    wh = w.astype(BF16)
    wl = (w - wh.astype(F32)).astype(BF16)
    z = (jnp.dot(yb, wh, preferred_element_type=F32) + jnp.dot(ylo, wh, preferred_element_type=F32)
         + jnp.dot(yb, wl, preferred_element_type=F32)) + b_ref[...]
    lf = (jnp.minimum(z, 0.0) - jnp.log1p(jnp.exp(-jnp.abs(z)))) * LOG2E
    parts = jnp.concatenate(split3(lf), axis=1)
    cs3 = jnp.dot(tri_ref[...], parts, preferred_element_type=F32)
    cs = cs3[:, 0:LANES] + cs3[:, LANES:2 * LANES] + cs3[:, 2 * LANES:3 * LANES]
    c_ref[...] = carry_ref[...]
    carry_ref[...] = carry_ref[...] + cs[FOX_T - 1:FOX_T, :]
    csk = cs + koff_ref[...]
    koff_ref[...] = koff_ref[...] + cs[FOX_T - 1:FOX_T, :]

    hi, mid, lo = (p.astype(F32) for p in split3(cs))
    khi, kmid, klo = (p.astype(F32) for p in split3(csk))
    lane = lax.broadcasted_iota(jnp.int32, (FOX_T, LANES), 1)
    q_ones = jnp.where((lane >= 3) & (lane < 6), 1.0, 0.0).astype(F32)
    k_ones = jnp.where(lane < 3, 1.0, 0.0).astype(F32)
    for h in range(B_HEADS):
        hc, mc, lc = hi[:, h:h + 1], mid[:, h:h + 1], lo[:, h:h + 1]
        tq = jnp.where(lane == 0, hc, jnp.where(lane == 1, mc, jnp.where(lane == 2, lc, q_ones)))
        hc, mc, lc = khi[:, h:h + 1], kmid[:, h:h + 1], klo[:, h:h + 1]
        tk = jnp.where(lane == 3, -hc, jnp.where(lane == 4, -mc, jnp.where(lane == 5, -lc, k_ones)))
        qf_ref[h] = tq.astype(BF16)
        kf_ref[h] = tk.astype(BF16)


def _forget_columns(x, fw, fb):
    b, s, d = x.shape
    t = FOX_T
    nblk = s // t
    tri = (jnp.arange(t)[:, None] >= jnp.arange(t)[None, :]).astype(BF16)
    const = lambda bb, i: (0, 0)
    cols = jax.ShapeDtypeStruct((b, B_HEADS, s, LANES), BF16)
    col_spec = pl.BlockSpec((None, B_HEADS, t, LANES), lambda bb, i: (bb, 0, i, 0))
    return pl.pallas_call(
        _fg_kernel,
        out_shape=(cols, cols, jax.ShapeDtypeStruct((b, nblk, 1, LANES), F32)),
        grid=(b, nblk),
        in_specs=[
            pl.BlockSpec((None, t, d), lambda bb, i: (bb, i, 0)),
            pl.BlockSpec((d, LANES), const),
            pl.BlockSpec((1, LANES), const),
            pl.BlockSpec((t, t), const),
        ],
        out_specs=(col_spec, col_spec,
                   pl.BlockSpec((None, None, 1, LANES), lambda bb, i: (bb, i, 0, 0))),
        scratch_shapes=[pltpu.VMEM((1, LANES), F32), pltpu.VMEM((1, LANES), F32)],
        compiler_params=_params(("arbitrary", "arbitrary")),
        name="forget_cumsum",
    )(x, fw, fb, tri)


def _fox_kernel(c_tab, q_ref, qf_ref, k_ref, kf_ref, v_ref, o_ref,
                qa_ref, sa_ref, sb_ref, m_ref, acc_ref):
    i = pl.program_id(2)
    tq, tk = FOX_T, FOX_TK
    per_tile = tk // tq
    cbase = (pl.program_id(0) * B_HEADS + pl.program_id(1)) * pl.num_programs(2)
    c_q = jnp.full((1, LANES), c_tab[cbase + i], F32)
    qa_ref[:, 0:B_HEAD_DIM] = q_ref[...]
    qa_ref[:, B_HEAD_DIM:2 * B_HEAD_DIM] = qf_ref[...]
    m_ref[...] = jnp.full(m_ref.shape, NEG, F32)
    acc_ref[...] = jnp.zeros_like(acc_ref)
    ones_col = jnp.where(lax.broadcasted_iota(jnp.int32, (tk, LANES), 1) == 0, 1.0, 0.0).astype(BF16)

    def scores(jt, dst_ref):
        ks = pl.multiple_of(jt * tk, tk)
        ka = jnp.concatenate([k_ref[pl.ds(ks, tk), :], kf_ref[pl.ds(ks, tk), :]], axis=1)
        dst_ref[...] = lax.dot_general(qa_ref[...], ka, (((1,), (1,)), ((), ())),
                                       preferred_element_type=F32)

    def absorb(src_ref, jt, last):
        ks = pl.multiple_of(jt * tk, tk)
        s = src_ref[...]
        if last:
            row = lax.broadcasted_iota(jnp.int32, (tq, tk), 0) + (i - per_tile * jt) * tq
            col = lax.broadcasted_iota(jnp.int32, (tq, tk), 1)
            s = jnp.where(col <= row, s, NEG)
        shift = c_q - jnp.full((1, LANES), c_tab[cbase + per_tile * jt], F32)
        m_prev = m_ref[...]
        m_new = jnp.maximum(m_prev, jnp.max(s, axis=-1, keepdims=True) + shift)
        p = jnp.exp2(s - jnp.concatenate([m_new - shift] * (tk // LANES), axis=1))
        scale = jnp.exp2(m_prev - m_new)
        va = jnp.concatenate([v_ref[pl.ds(ks, tk), :], ones_col], axis=1)
        acc_ref[...] = jnp.concatenate([scale, scale], axis=1) * acc_ref[...] + jnp.dot(
            p.astype(BF16), va, preferred_element_type=F32)
        m_ref[...] = m_new

    n_full = lax.shift_right_logical(i, 1)
    scores(0, sa_ref)

    def pair(jj, carry):
        scores(2 * jj + 1, sb_ref)
        absorb(sa_ref, 2 * jj, False)
        scores(2 * jj + 2, sa_ref)
        absorb(sb_ref, 2 * jj + 1, False)
        return carry

    lax.fori_loop(0, lax.shift_right_logical(n_full, 1), pair, 0)

    @pl.when((n_full & 1) == 1)
    def _():
        scores(n_full, sb_ref)
        absorb(sa_ref, n_full - 1, False)
        absorb(sb_ref, n_full, True)

    @pl.when((n_full & 1) == 0)
    def _():
        absorb(sa_ref, n_full, True)

    acc = acc_ref[...]
    o_ref[...] = (acc[:, 0:B_HEAD_DIM] / acc[:, B_HEAD_DIM:B_HEAD_DIM + 1]).astype(o_ref.dtype)


def _fox(q, kv, qf, kf, c_tab):
    b, s, d = q.shape
    t = FOX_T
    hd = B_HEAD_DIM
    grid_spec = pltpu.PrefetchScalarGridSpec(
        num_scalar_prefetch=1,
        grid=(b, B_HEADS, s // t),
        in_specs=[
            pl.BlockSpec((None, t, hd), lambda bb, h, i, c: (bb, i, h)),
            pl.BlockSpec((None, None, t, LANES), lambda bb, h, i, c: (bb, h, i, 0)),
            pl.BlockSpec((None, s, hd), lambda bb, h, i, c: (bb, 0, h)),
            pl.BlockSpec((None, None, s, LANES), lambda bb, h, i, c: (bb, h, 0, 0)),
            pl.BlockSpec((None, s, hd), lambda bb, h, i, c: (bb, 0, B_HEADS + h)),
        ],
        out_specs=pl.BlockSpec((None, t, hd), lambda bb, h, i, c: (bb, i, h)),
        scratch_shapes=[
            pltpu.VMEM((t, 2 * hd), BF16),
            pltpu.VMEM((t, FOX_TK), F32),
            pltpu.VMEM((t, FOX_TK), F32),
            pltpu.VMEM((t, LANES), F32),
            pltpu.VMEM((t, 2 * hd), F32),
        ],
    )
    return pl.pallas_call(
        _fox_kernel,
        out_shape=jax.ShapeDtypeStruct((b, s, d), BF16),
        grid_spec=grid_spec,
        compiler_params=_params(("arbitrary", "arbitrary", "arbitrary")),
        name="mixer_b",
    )(c_tab, q, qf, kv, kf, kv)


def _moe_layer(x1, x1b, experts, gates, w_gate, w_up, w_down, gain, bias):
    meta, tok_sorted, pos = _moe_plan(experts)
    xs = x1b[tok_sorted]
    yb = _moe(xs, w_gate, w_up, w_down, meta)
    return _combine_ln(x1, yb[pos[0]], yb[pos[1]], gates.T, gain, bias)


def kernel(x, a_w_qkv, a_rel_bias, a_w_o, kv_w, fg_w, fg_b, b_w_q, b_w_o, router_w, router_b,
           moe_w_gate, moe_w_up, moe_w_down, ln_gain, ln_bias):
    b, s, d = x.shape
    n = b * s
    assert MM_TM == PAD_ROWS and s % MM_TM == 0 and s % FOX_TK == 0 and FOX_TK == 2 * FOX_T
    assert (2 * n) % MOE_BLK == 0

    rw = _pad_small_weight(router_w)
    rb = router_b.astype(F32).reshape(N_EXPERTS, 1)
    gain = ln_gain.astype(F32).reshape(DEPTH, 2, 1, d)
    bias = ln_bias.astype(F32).reshape(DEPTH, 2, 1, d)

    qscale = jnp.concatenate([jnp.full((d,), 1.0 / math.sqrt(A_HEAD_DIM), F32),
                              jnp.ones((2 * d,), F32)])
    w_qkv = (a_w_qkv[0] * qscale[None, :]).astype(BF16)
    qkv = _mm(x.astype(BF16), w_qkv, pad_blocks=1)
    attn = _attn_a(qkv, _rel_bias_table(a_rel_bias[0]), s)
    x1, x1b, experts, gates = _post_attn(attn.reshape(n, d), a_w_o[0].astype(BF16),
                                         x.reshape(n, d), gain[0, 0], bias[0, 0], rw, rb)
    x2, x2b = _moe_layer(x1, x1b, experts, gates, moe_w_gate[0].astype(BF16),
                         moe_w_up[0].astype(BF16), moe_w_down[0].astype(BF16),
                         gain[0, 1], bias[0, 1])

    kv = _mm(x2b.reshape(b, s, d), kv_w.astype(BF16))
    fb = jnp.zeros((1, LANES), F32).at[0, :B_HEADS].set(fg_b.astype(F32))
    qf, kf, carry_in = _forget_columns(x2.reshape(b, s, d), _pad_small_weight(fg_w), fb)
    c_tab = carry_in[:, :, 0, :B_HEADS].transpose(0, 2, 1).reshape(-1)

    w_q = (b_w_q[0] * (LOG2E / math.sqrt(B_HEAD_DIM))).astype(BF16)
    q = _mm(x2b.reshape(b, s, d), w_q)
    attn = _fox(q, kv, qf, kf, c_tab)
    x3, x3b, experts, gates = _post_attn(attn.reshape(n, d), b_w_o[0].astype(BF16), x2,
                                         gain[1, 0], bias[1, 0], rw, rb)
    x4, _ = _moe_layer(x3, x3b, experts, gates, moe_w_gate[1].astype(BF16),
                       moe_w_up[1].astype(BF16), moe_w_down[1].astype(BF16),
                       gain[1, 1], bias[1, 1])
    return x4.reshape(b, s, d)
```

```python
import functools
import math

import jax
import jax.numpy as jnp
from jax import lax
from jax.experimental import pallas as pl
from jax.experimental.pallas import tpu as pltpu

F32 = jnp.float32
BF16 = jnp.bfloat16

CHUNK = 64
LEFT_CHUNKS = 8
A_HEADS = 32
A_HEAD_DIM = 64
REL_CLIP = 2 * CHUNK
B_HEADS = 16
B_HEAD_DIM = 128
N_EXPERTS = 16
N_GROUPS = 4
EXPERTS_PER_GROUP = 4
DEPTH = 2
ALPHA = (2.0 * DEPTH) ** 0.25
LN_EPS = 1e-5
NEG = -1e30
LOG2E = 1.4426950408889634

LANES = 128
VMEM_LIMIT = 56 * 1024 * 1024

MM_TM = 512
MM_TN = 1024
PAD_ROWS = LEFT_CHUNKS * CHUNK
A_TQ = 256
A_WIN = A_TQ + PAD_ROWS
A_SUB = 16
A_TABLES = PAD_ROWS // A_TQ + 1
POST_TM = 256
MOE_BLK = 512
MOE_TF = 512
LN_TM = 512
FOX_T = 512
FOX_TK = 2 * FOX_T


def _params(sem):
    return pltpu.CompilerParams(dimension_semantics=sem, vmem_limit_bytes=VMEM_LIMIT)


def _mm_kernel(x_ref, w_ref, o_ref, *, pad_blocks):
    def compute():
        o_ref[...] = jnp.dot(x_ref[...], w_ref[...],
                             preferred_element_type=F32).astype(o_ref.dtype)

    if pad_blocks:
        i = pl.program_id(2)

        @pl.when(i < pad_blocks)
        def _():
            o_ref[...] = jnp.zeros_like(o_ref)

        pl.when(i >= pad_blocks)(compute)
    else:
        compute()


def _mm(x, w, *, pad_blocks=0):
    b, s, k = x.shape
    m = w.shape[1]
    tn = min(MM_TN, m)
    grid = (m // tn, b, s // MM_TM + pad_blocks)
    return pl.pallas_call(
        functools.partial(_mm_kernel, pad_blocks=pad_blocks),
        out_shape=jax.ShapeDtypeStruct((b, s + pad_blocks * MM_TM, m), BF16),
        grid=grid,
        in_specs=[
            pl.BlockSpec((None, MM_TM, k),
                         lambda j, bb, i: (bb, jnp.maximum(i - pad_blocks, 0), 0)),
            pl.BlockSpec((k, tn), lambda j, bb, i: (0, j)),
        ],
        out_specs=pl.BlockSpec((None, MM_TM, tn), lambda j, bb, i: (bb, i, j)),
        compiler_params=_params(("arbitrary", "arbitrary", "arbitrary")),
        name="dense_proj",
    )(x, w)


def _attn_a_kernel(q_ref, k_ref, v_ref, bias_ref, o_ref, sa_ref, sb_ref):
    i = pl.program_id(2)
    head0 = lax.broadcasted_iota(jnp.int32, (A_TQ, LANES), 1) < A_HEAD_DIM
    ones_col = jnp.where(lax.broadcasted_iota(jnp.int32, (A_WIN, LANES), 1) == 0, 1.0, 0.0).astype(BF16)

    def window(sb):
        g = i * A_SUB + sb
        return g, pl.multiple_of(g * A_TQ, A_TQ)

    def scores(sb, dst_ref):
        _, start = window(sb)
        q = q_ref[pl.ds(start + PAD_ROWS, A_TQ), :]
        zero = jnp.zeros_like(q)
        q2 = jnp.concatenate([jnp.where(head0, q, zero), jnp.where(head0, zero, q)], axis=0)
        dst_ref[...] = lax.dot_general(q2, k_ref[pl.ds(start, A_WIN), :], (((1,), (1,)), ((), ())),
                                       preferred_element_type=F32)

    def absorb(src_ref, sb):
        g, start = window(sb)
        s = src_ref[...] + bias_ref[jnp.minimum(g, A_TABLES - 1)]
        p = jnp.exp2(s - jnp.max(s, axis=-1, keepdims=True))
        va = jnp.concatenate([v_ref[pl.ds(start, A_WIN), :], ones_col], axis=1)
        o = jnp.dot(p.astype(BF16), va, preferred_element_type=F32)
        o = o[:, 0:LANES] / o[:, LANES:LANES + 1]
        rows = pl.ds(pl.multiple_of(sb * A_TQ, A_TQ), A_TQ)
        o_ref[rows, :] = jnp.where(head0, o[0:A_TQ], o[A_TQ:2 * A_TQ]).astype(o_ref.dtype)

    scores(0, sa_ref)

    def pair(jj, carry):
        scores(2 * jj + 1, sb_ref)
        absorb(sa_ref, 2 * jj)
        scores(2 * jj + 2, sa_ref)
        absorb(sb_ref, 2 * jj + 1)
        return carry

    lax.fori_loop(0, A_SUB // 2 - 1, pair, 0)
    scores(A_SUB - 1, sb_ref)
    absorb(sa_ref, A_SUB - 2)
    absorb(sb_ref, A_SUB - 1)


def _attn_a(qkv_pad, bias_tab, s):
    b = qkv_pad.shape[0]
    d = A_HEADS * A_HEAD_DIM
    pairs = d // LANES
    sp = s + PAD_ROWS
    rows = A_TQ * A_SUB
    return pl.pallas_call(
        _attn_a_kernel,
        out_shape=jax.ShapeDtypeStruct((b, s, d), BF16),
        grid=(pairs, b, s // rows),
        in_specs=[
            pl.BlockSpec((None, sp, LANES), lambda j, bb, i: (bb, 0, j)),
            pl.BlockSpec((None, sp, LANES), lambda j, bb, i: (bb, 0, pairs + j)),
            pl.BlockSpec((None, sp, LANES), lambda j, bb, i: (bb, 0, 2 * pairs + j)),
            pl.BlockSpec((None, A_TABLES, 2 * A_TQ, A_WIN), lambda j, bb, i: (j, 0, 0, 0)),
        ],
        out_specs=pl.BlockSpec((None, rows, LANES), lambda j, bb, i: (bb, i, j)),
        scratch_shapes=[pltpu.VMEM((2 * A_TQ, A_WIN), F32), pltpu.VMEM((2 * A_TQ, A_WIN), F32)],
        compiler_params=_params(("arbitrary", "arbitrary", "arbitrary")),
        name="mixer_a",
    )(qkv_pad, qkv_pad, qkv_pad, bias_tab)


def _rel_bias_table(rel_bias):
    qi = jnp.arange(A_TQ)[:, None]
    kj = jnp.arange(A_WIN)[None, :]
    dchunk = kj // CHUNK - qi // CHUNK
    inband = (dchunk >= 0) & (dchunk <= LEFT_CHUNKS)
    period = A_TQ + A_WIN
    u = jnp.arange(period)
    u = jnp.where(u < A_WIN, u, u - period)
    idx = jnp.clip(PAD_ROWS - u, -REL_CLIP, REL_CLIP) + REL_CLIP
    g = rel_bias.astype(F32)[:, idx] * LOG2E
    flat = jnp.tile(g, (1, A_TQ))[:, :A_TQ * (period - 1)]
    tab = flat.reshape(-1, A_TQ, period - 1)[:, :, :A_WIN]
    first_real = PAD_ROWS - A_TQ * jnp.arange(A_TABLES)[:, None, None]
    keep = inband[None] & (kj[None] >= first_real)
    tabs = jnp.where(keep[None], tab[:, None], NEG)
    tabs = tabs.reshape(-1, 2, A_TABLES, A_TQ, A_WIN).transpose(0, 2, 1, 3, 4)
    return tabs.reshape(-1, A_TABLES, 2 * A_TQ, A_WIN)


def _layer_norm(z, gain, bias):
    mu = jnp.mean(z, axis=-1, keepdims=True)
    zc = z - mu
    var = jnp.mean(zc * zc, axis=-1, keepdims=True)
    return zc * lax.rsqrt(var + LN_EPS) * gain + bias


def _small_proj_t(y, yb, w_ref, n_out):
    ylo = (y - yb.astype(F32)).astype(BF16)
    w = w_ref[...]
    wh = w.astype(BF16)
    wl = (w - wh.astype(F32)).astype(BF16)
    acc = jnp.dot(yb, wh, preferred_element_type=F32)
    acc = acc + jnp.dot(ylo, wh, preferred_element_type=F32)
    acc = acc + jnp.dot(yb, wl, preferred_element_type=F32)
    return acc.T[:n_out, :]


def _pad_small_weight(w):
    k, n = w.shape
    return jnp.zeros((k, LANES), F32).at[:, :n].set(w.astype(F32))


def _route(lt):
    m = jnp.max(lt, axis=0, keepdims=True)
    e = jnp.exp(lt - m)
    p = e / jnp.sum(e, axis=0, keepdims=True)
    rows = [p[k:k + 1, :] for k in range(N_EXPERTS)]
    best = None
    gi = None
    for g in range(N_GROUPS):
        r = rows[EXPERTS_PER_GROUP * g:EXPERTS_PER_GROUP * (g + 1)]
        sc = None
        for a in range(EXPERTS_PER_GROUP):
            for c in range(a + 1, EXPERTS_PER_GROUP):
                pair = r[a] + r[c]
                sc = pair if sc is None else jnp.maximum(sc, pair)
        if g == 0:
            best = sc
            gi = jnp.zeros(sc.shape, jnp.int32)
        else:
            upd = sc > best
            best = jnp.where(upd, sc, best)
            gi = jnp.where(upd, g, gi)
    pin = []
    for k in range(EXPERTS_PER_GROUP):
        v = rows[(N_GROUPS - 1) * EXPERTS_PER_GROUP + k]
        for g in range(N_GROUPS - 2, -1, -1):
            v = jnp.where(gi == g, rows[g * EXPERTS_PER_GROUP + k], v)
        pin.append(v)
    v1 = pin[0]
    i1 = jnp.zeros(v1.shape, jnp.int32)
    for k in range(1, EXPERTS_PER_GROUP):
        upd = pin[k] > v1
        v1 = jnp.where(upd, pin[k], v1)
        i1 = jnp.where(upd, k, i1)
    v2 = jnp.full(v1.shape, -1.0, F32)
    i2 = jnp.zeros(v1.shape, jnp.int32)
    for k in range(EXPERTS_PER_GROUP):
        cand = jnp.where(i1 == k, -1.0, pin[k])
        upd = cand > v2
        v2 = jnp.where(upd, cand, v2)
        i2 = jnp.where(upd, k, i2)
    tot = v1 + v2
    base = gi * EXPERTS_PER_GROUP
    experts = jnp.concatenate([base + i1, base + i2], axis=0)
    gates = jnp.concatenate([v1 / tot, v2 / tot], axis=0)
    return experts, gates


def _post_attn_kernel(a_ref, w_ref, x_ref, g_ref, b_ref, rw_ref, rb_ref,
                      x1_ref, x1b_ref, ex_ref, gt_ref):
    h = jnp.dot(a_ref[...], w_ref[...], preferred_element_type=F32)
    y = _layer_norm(ALPHA * x_ref[...] + h, g_ref[...], b_ref[...])
    yb = y.astype(BF16)
    x1_ref[...] = y
    x1b_ref[...] = yb
    lt = _small_proj_t(y, yb, rw_ref, N_EXPERTS) + rb_ref[...]
    experts, gates = _route(lt)
    ex_ref[...] = experts
    gt_ref[...] = gates


def _post_attn(attn, w_o, x, gain, bias, rw, rb):
    n, d = x.shape
    tm = POST_TM
    row = lambda i: (i, 0)
    const = lambda i: (0, 0)
    return pl.pallas_call(
        _post_attn_kernel,
        out_shape=(jax.ShapeDtypeStruct((n, d), F32),
                   jax.ShapeDtypeStruct((n, d), BF16),
                   jax.ShapeDtypeStruct((2, n), jnp.int32),
                   jax.ShapeDtypeStruct((2, n), F32)),
        grid=(n // tm,),
        in_specs=[
            pl.BlockSpec((tm, d), row),
            pl.BlockSpec((d, d), const),
            pl.BlockSpec((tm, d), row),
            pl.BlockSpec((1, d), const),
            pl.BlockSpec((1, d), const),
            pl.BlockSpec((d, LANES), const),
            pl.BlockSpec((N_EXPERTS, 1), const),
        ],
        out_specs=(pl.BlockSpec((tm, d), row),
                   pl.BlockSpec((tm, d), row),
                   pl.BlockSpec((2, tm), lambda i: (0, i)),
                   pl.BlockSpec((2, tm), lambda i: (0, i))),
        compiler_params=_params(("arbitrary",)),
        name="out_proj_ln_router",
    )(attn, w_o, x, gain, bias, rw, rb)


def _moe_kernel(iblk, iex, ivalid, ifirst, starts, ends, x_ref, wg_ref, wu_ref, wd_ref, o_ref):
    w = pl.program_id(0)
    f = pl.program_id(1)

    @pl.when((ifirst[w] == 1) & (f == 0))
    def _():
        o_ref[...] = jnp.zeros_like(o_ref)

    @pl.when(ivalid[w] == 1)
    def _():
        x = x_ref[...]
        hg = jnp.dot(x, wg_ref[...], preferred_element_type=F32)
        hu = jnp.dot(x, wu_ref[...], preferred_element_type=F32)
        e = iex[w]
        row = iblk[w] * MOE_BLK + lax.broadcasted_iota(jnp.int32, (MOE_BLK, 1), 0)
        mine = (row >= starts[e]) & (row < ends[e])
        hid = jnp.where(mine, hg * jax.nn.sigmoid(hg) * hu, 0.0)
        o_ref[...] += jnp.dot(hid.astype(BF16), wd_ref[...], preferred_element_type=F32)


def _moe(xs, w_gate, w_up, w_down, layer, meta):
    a, d = xs.shape
    dff = w_gate.shape[-1]
    nf = dff // MOE_TF
    n_items = meta[0].shape[0]

    def fsel(w, f, ivalid):
        return jnp.where(ivalid[w] == 1, f, nf - 1)

    grid_spec = pltpu.PrefetchScalarGridSpec(
        num_scalar_prefetch=6,
        grid=(n_items, nf),
        in_specs=[
            pl.BlockSpec((MOE_BLK, d), lambda w, f, ib, ie, iv, i1, st, en: (ib[w], 0)),
            pl.BlockSpec((None, None, d, MOE_TF),
                         lambda w, f, ib, ie, iv, i1, st, en: (layer, ie[w], 0, fsel(w, f, iv))),
            pl.BlockSpec((None, None, d, MOE_TF),
                         lambda w, f, ib, ie, iv, i1, st, en: (layer, ie[w], 0, fsel(w, f, iv))),
            pl.BlockSpec((None, None, MOE_TF, d),
                         lambda w, f, ib, ie, iv, i1, st, en: (layer, ie[w], fsel(w, f, iv), 0)),
        ],
        out_specs=pl.BlockSpec((MOE_BLK, d), lambda w, f, ib, ie, iv, i1, st, en: (ib[w], 0)),
    )
    return pl.pallas_call(
        _moe_kernel,
        out_shape=jax.ShapeDtypeStruct((a, d), F32),
        grid_spec=grid_spec,
        compiler_params=_params(("arbitrary", "arbitrary")),
        name="moe_experts",
    )(*meta, xs, w_gate, w_up, w_down)


def _moe_plan(experts):
    n = experts.shape[1]
    a = 2 * n
    nblk = a // MOE_BLK
    n_items = nblk + N_EXPERTS - 1
    e_flat = experts.reshape(-1)
    iota = jnp.arange(a, dtype=jnp.int32)
    e_sorted, order = lax.sort((e_flat, iota), num_keys=1, is_stable=True)
    tok_sorted = order % n
    _, pos = lax.sort((order, iota), num_keys=1)
    pos = pos.reshape(2, n)
    eids = jnp.arange(N_EXPERTS, dtype=jnp.int32)
    starts = jnp.searchsorted(e_sorted, eids, side="left").astype(jnp.int32)
    ends = jnp.searchsorted(e_sorted, eids, side="right").astype(jnp.int32)
    e_lo = e_sorted[0::MOE_BLK]
    e_hi = e_sorted[MOE_BLK - 1::MOE_BLK]
    per_blk = e_hi - e_lo + 1
    iend = jnp.cumsum(per_blk).astype(jnp.int32)
    istart = iend - per_blk
    total = iend[-1]
    w = jnp.arange(n_items, dtype=jnp.int32)
    blk = jnp.clip(jnp.searchsorted(iend, w, side="right"), 0, nblk - 1).astype(jnp.int32)
    ex = e_lo[blk] + (w - istart[blk])
    valid = w < total
    blk = jnp.where(valid, blk, nblk - 1)
    ex = jnp.where(valid, ex, e_hi[nblk - 1])
    first = valid & (w == istart[blk])
    meta = (blk, ex.astype(jnp.int32), valid.astype(jnp.int32), first.astype(jnp.int32),
            starts, ends)
    return meta, tok_sorted, pos


def _combine_ln_kernel(x_ref, y0_ref, y1_ref, gt_ref, g_ref, b_ref, o_ref, ob_ref):
    gt = gt_ref[...]
    y = y0_ref[...] * gt[:, 0:1] + y1_ref[...] * gt[:, 1:2]
    out = _layer_norm(ALPHA * x_ref[...] + y, g_ref[...], b_ref[...])
    o_ref[...] = out
    ob_ref[...] = out.astype(BF16)


def _combine_ln(x, y0, y1, gates_t, gain, bias):
    n, d = x.shape
    tm = LN_TM
    row = lambda i: (i, 0)
    const = lambda i: (0, 0)
    return pl.pallas_call(
        _combine_ln_kernel,
        out_shape=(jax.ShapeDtypeStruct((n, d), F32), jax.ShapeDtypeStruct((n, d), BF16)),
        grid=(n // tm,),
        in_specs=[
            pl.BlockSpec((tm, d), row),
            pl.BlockSpec((tm, d), row),
            pl.BlockSpec((tm, d), row),
            pl.BlockSpec((tm, 2), row),
            pl.BlockSpec((1, d), const),
            pl.BlockSpec((1, d), const),
        ],
        out_specs=(pl.BlockSpec((tm, d), row), pl.BlockSpec((tm, d), row)),
        compiler_params=_params(("arbitrary",)),
        name="moe_combine_ln",
    )(x, y0, y1, gates_t, gain, bias)


def _fg_kernel(x_ref, w_ref, b_ref, tri_ref, qf_ref, kf_ref, c_ref, carry_ref, koff_ref):
    i = pl.program_id(1)

    @pl.when(i == 0)
    def _():
        carry_ref[...] = jnp.zeros_like(carry_ref)

    @pl.when(i % (FOX_TK // FOX_T) == 0)
    def _():
        koff_ref[...] = jnp.zeros_like(koff_ref)

    def split3(v):
        hi = v.astype(BF16)
        r1 = v - hi.astype(F32)
        mid = r1.astype(BF16)
        lo = (r1 - mid.astype(F32)).astype(BF16)
        return hi, mid, lo

    y = x_ref[...]
    yb = y.astype(BF16)
    ylo = (y - yb.astype(F32)).astype(BF16)
    w = w_ref[...]
    wh = w.astype(BF16)
    wl = (w - wh.astype(F32)).astype(BF16)
    z = (jnp.dot(yb, wh, preferred_element_type=F32) + jnp.dot(ylo, wh, preferred_element_type=F32)
         + jnp.dot(yb, wl, preferred_element_type=F32)) + b_ref[...]
    lf = (jnp.minimum(z, 0.0) - jnp.log1p(jnp.exp(-jnp.abs(z)))) * LOG2E
    parts = jnp.concatenate(split3(lf), axis=1)
    cs3 = jnp.dot(tri_ref[...], parts, preferred_element_type=F32)
    cs = cs3[:, 0:LANES] + cs3[:, LANES:2 * LANES] + cs3[:, 2 * LANES:3 * LANES]
    c_ref[...] = carry_ref[...]
    carry_ref[...] = carry_ref[...] + cs[FOX_T - 1:FOX_T, :]
    csk = cs + koff_ref[...]
    koff_ref[...] = koff_ref[...] + cs[FOX_T - 1:FOX_T, :]

    hi, mid, lo = (p.astype(F32) for p in split3(cs))
    khi, kmid, klo = (p.astype(F32) for p in split3(csk))
    lane = lax.broadcasted_iota(jnp.int32, (FOX_T, LANES), 1)
    q_ones = jnp.where((lane >= 3) & (lane < 6), 1.0, 0.0).astype(F32)
    k_ones = jnp.where(lane < 3, 1.0, 0.0).astype(F32)
    for h in range(B_HEADS):
        hc, mc, lc = hi[:, h:h + 1], mid[:, h:h + 1], lo[:, h:h + 1]
        tq = jnp.where(lane == 0, hc, jnp.where(lane == 1, mc, jnp.where(lane == 2, lc, q_ones)))
        hc, mc, lc = khi[:, h:h + 1], kmid[:, h:h + 1], klo[:, h:h + 1]
        tk = jnp.where(lane == 3, -hc, jnp.where(lane == 4, -mc, jnp.where(lane == 5, -lc, k_ones)))
        qf_ref[h] = tq.astype(BF16)
        kf_ref[h] = tk.astype(BF16)


def _forget_columns(x, fw, fb):
    b, s, d = x.shape
    t = FOX_T
    nblk = s // t
    tri = (jnp.arange(t)[:, None] >= jnp.arange(t)[None, :]).astype(BF16)
    const = lambda bb, i: (0, 0)
    cols = jax.ShapeDtypeStruct((b, B_HEADS, s, LANES), BF16)
    col_spec = pl.BlockSpec((None, B_HEADS, t, LANES), lambda bb, i: (bb, 0, i, 0))
    return pl.pallas_call(
        _fg_kernel,
        out_shape=(cols, cols, jax.ShapeDtypeStruct((b, nblk, 1, LANES), F32)),
        grid=(b, nblk),
        in_specs=[
            pl.BlockSpec((None, t, d), lambda bb, i: (bb, i, 0)),
            pl.BlockSpec((d, LANES), const),
            pl.BlockSpec((1, LANES), const),
            pl.BlockSpec((t, t), const),
        ],
        out_specs=(col_spec, col_spec,
                   pl.BlockSpec((None, None, 1, LANES), lambda bb, i: (bb, i, 0, 0))),
        scratch_shapes=[pltpu.VMEM((1, LANES), F32), pltpu.VMEM((1, LANES), F32)],
        compiler_params=_params(("arbitrary", "arbitrary")),
        name="forget_cumsum",
    )(x, fw, fb, tri)


def _fox_kernel(c_tab, q_ref, qf_ref, k_ref, kf_ref, v_ref, o_ref,
                qa_ref, sa_ref, sb_ref, m_ref, acc_ref):
    i = pl.program_id(2)
    tq, tk = FOX_T, FOX_TK
    per_tile = tk // tq
    cbase = (pl.program_id(0) * B_HEADS + pl.program_id(1)) * pl.num_programs(2)
    c_q = jnp.full((1, LANES), c_tab[cbase + i], F32)
    qa_ref[:, 0:B_HEAD_DIM] = q_ref[...]
    qa_ref[:, B_HEAD_DIM:2 * B_HEAD_DIM] = qf_ref[...]
    m_ref[...] = jnp.full(m_ref.shape, NEG, F32)
    acc_ref[...] = jnp.zeros_like(acc_ref)
    ones_col = jnp.where(lax.broadcasted_iota(jnp.int32, (tk, LANES), 1) == 0, 1.0, 0.0).astype(BF16)

    def scores(jt, dst_ref):
        ks = pl.multiple_of(jt * tk, tk)
        ka = jnp.concatenate([k_ref[pl.ds(ks, tk), :], kf_ref[pl.ds(ks, tk), :]], axis=1)
        dst_ref[...] = lax.dot_general(qa_ref[...], ka, (((1,), (1,)), ((), ())),
                                       preferred_element_type=F32)

    def absorb(src_ref, jt, last):
        ks = pl.multiple_of(jt * tk, tk)
        s = src_ref[...]
        if last:
            row = lax.broadcasted_iota(jnp.int32, (tq, tk), 0) + (i - per_tile * jt) * tq
            col = lax.broadcasted_iota(jnp.int32, (tq, tk), 1)
            s = jnp.where(col <= row, s, NEG)
        shift = c_q - jnp.full((1, LANES), c_tab[cbase + per_tile * jt], F32)
        m_prev = m_ref[...]
        m_new = jnp.maximum(m_prev, jnp.max(s, axis=-1, keepdims=True) + shift)
        p = jnp.exp2(s - jnp.concatenate([m_new - shift] * (tk // LANES), axis=1))
        scale = jnp.exp2(m_prev - m_new)
        va = jnp.concatenate([v_ref[pl.ds(ks, tk), :], ones_col], axis=1)
        acc_ref[...] = jnp.concatenate([scale, scale], axis=1) * acc_ref[...] + jnp.dot(
            p.astype(BF16), va, preferred_element_type=F32)
        m_ref[...] = m_new

    n_full = lax.shift_right_logical(i, 1)
    scores(0, sa_ref)

    def pair(jj, carry):
        scores(2 * jj + 1, sb_ref)
        absorb(sa_ref, 2 * jj, False)
        scores(2 * jj + 2, sa_ref)
        absorb(sb_ref, 2 * jj + 1, False)
        return carry

    lax.fori_loop(0, lax.shift_right_logical(n_full, 1), pair, 0)

    @pl.when((n_full & 1) == 1)
    def _():
        scores(n_full, sb_ref)
        absorb(sa_ref, n_full - 1, False)
        absorb(sb_ref, n_full, True)

    @pl.when((n_full & 1) == 0)
    def _():
        absorb(sa_ref, n_full, True)

    acc = acc_ref[...]
    o_ref[...] = (acc[:, 0:B_HEAD_DIM] / acc[:, B_HEAD_DIM:B_HEAD_DIM + 1]).astype(o_ref.dtype)


def _fox(q, kv, qf, kf, c_tab):
    b, s, d = q.shape
    t = FOX_T
    hd = B_HEAD_DIM
    grid_spec = pltpu.PrefetchScalarGridSpec(
        num_scalar_prefetch=1,
        grid=(b, B_HEADS, s // t),
        in_specs=[
            pl.BlockSpec((None, t, hd), lambda bb, h, i, c: (bb, i, h)),
            pl.BlockSpec((None, None, t, LANES), lambda bb, h, i, c: (bb, h, i, 0)),
            pl.BlockSpec((None, s, hd), lambda bb, h, i, c: (bb, 0, h)),
            pl.BlockSpec((None, None, s, LANES), lambda bb, h, i, c: (bb, h, 0, 0)),
            pl.BlockSpec((None, s, hd), lambda bb, h, i, c: (bb, 0, B_HEADS + h)),
        ],
        out_specs=pl.BlockSpec((None, t, hd), lambda bb, h, i, c: (bb, i, h)),
        scratch_shapes=[
            pltpu.VMEM((t, 2 * hd), BF16),
            pltpu.VMEM((t, FOX_TK), F32),
            pltpu.VMEM((t, FOX_TK), F32),
            pltpu.VMEM((t, LANES), F32),
            pltpu.VMEM((t, 2 * hd), F32),
        ],
    )
    return pl.pallas_call(
        _fox_kernel,
        out_shape=jax.ShapeDtypeStruct((b, s, d), BF16),
        grid_spec=grid_spec,
        compiler_params=_params(("arbitrary", "arbitrary", "arbitrary")),
        name="mixer_b",
    )(c_tab, q, qf, kv, kf, kv)


def _moe_layer(x1, x1b, experts, gates, w_gate, w_up, w_down, layer, gain, bias):
    meta, tok_sorted, pos = _moe_plan(experts)
    xs = x1b[tok_sorted]
    yb = _moe(xs, w_gate, w_up, w_down, layer, meta)
    return _combine_ln(x1, yb[pos[0]], yb[pos[1]], gates.T, gain, bias)


def kernel(x, a_w_qkv, a_rel_bias, a_w_o, kv_w, fg_w, fg_b, b_w_q, b_w_o, router_w, router_b,
           moe_w_gate, moe_w_up, moe_w_down, ln_gain, ln_bias):
    b, s, d = x.shape
    n = b * s
    assert MM_TM == PAD_ROWS and s % MM_TM == 0 and s % FOX_TK == 0 and FOX_TK == 2 * FOX_T
    assert s % (A_TQ * A_SUB) == 0 and PAD_ROWS % A_TQ == 0 and A_SUB % 2 == 0 and A_SUB >= 4
    assert (2 * n) % MOE_BLK == 0

    rw = _pad_small_weight(router_w)
    rb = router_b.astype(F32).reshape(N_EXPERTS, 1)
    gain = ln_gain.astype(F32).reshape(DEPTH, 2, 1, d)
    bias = ln_bias.astype(F32).reshape(DEPTH, 2, 1, d)
    wg = moe_w_gate.astype(BF16)
    wu = moe_w_up.astype(BF16)
    wd = moe_w_down.astype(BF16)

    qscale = jnp.concatenate([jnp.full((d,), LOG2E / math.sqrt(A_HEAD_DIM), F32),
                              jnp.ones((2 * d,), F32)])
    w_qkv = (a_w_qkv[0] * qscale[None, :]).astype(BF16)
    qkv = _mm(x.astype(BF16), w_qkv, pad_blocks=1)
    attn = _attn_a(qkv, _rel_bias_table(a_rel_bias[0]), s)
    x1, x1b, experts, gates = _post_attn(attn.reshape(n, d), a_w_o[0].astype(BF16),
                                         x.reshape(n, d), gain[0, 0], bias[0, 0], rw, rb)
    x2, x2b = _moe_layer(x1, x1b, experts, gates, wg, wu, wd, 0, gain[0, 1], bias[0, 1])

    kv = _mm(x2b.reshape(b, s, d), kv_w.astype(BF16))
    fb = jnp.zeros((1, LANES), F32).at[0, :B_HEADS].set(fg_b.astype(F32))
    qf, kf, carry_in = _forget_columns(x2.reshape(b, s, d), _pad_small_weight(fg_w), fb)
    c_tab = carry_in[:, :, 0, :B_HEADS].transpose(0, 2, 1).reshape(-1)

    w_q = (b_w_q[0] * (LOG2E / math.sqrt(B_HEAD_DIM))).astype(BF16)
    q = _mm(x2b.reshape(b, s, d), w_q)
    attn = _fox(q, kv, qf, kf, c_tab)
    x3, x3b, experts, gates = _post_attn(attn.reshape(n, d), b_w_o[0].astype(BF16), x2,
                                         gain[1, 0], bias[1, 0], rw, rb)
    x4, _ = _moe_layer(x3, x3b, experts, gates, wg, wu, wd, 1, gain[1, 1], bias[1, 1])
    return x4.reshape(b, s, d)
```

```python
import functools
import math

import jax
import jax.numpy as jnp
from jax import lax
from jax.experimental import pallas as pl
from jax.experimental.pallas import tpu as pltpu

F32 = jnp.float32
BF16 = jnp.bfloat16

CHUNK = 64
LEFT_CHUNKS = 8
A_HEADS = 32
A_HEAD_DIM = 64
REL_CLIP = 2 * CHUNK
B_HEADS = 16
B_HEAD_DIM = 128
N_EXPERTS = 16
N_GROUPS = 4
EXPERTS_PER_GROUP = 4
DEPTH = 2
ALPHA = (2.0 * DEPTH) ** 0.25
LN_EPS = 1e-5
NEG = -1e30
LOG2E = 1.4426950408889634

LANES = 128
VMEM_LIMIT = 56 * 1024 * 1024

MM_TM = 512
MM_TN = 1024
PAD_ROWS = LEFT_CHUNKS * CHUNK
A_TQ = 256
A_WIN = A_TQ + PAD_ROWS
A_SUB = 16
A_TABLES = PAD_ROWS // A_TQ + 1
POST_TM = 512
MOE_BLK = 512
MOE_TF = 512
LN_TM = 512
FOX_T = 512
FOX_TK = 2 * FOX_T


def _params(sem):
    return pltpu.CompilerParams(dimension_semantics=sem, vmem_limit_bytes=VMEM_LIMIT)


def _mm_kernel(x_ref, w_ref, o_ref, *, pad_blocks):
    def compute():
        o_ref[...] = jnp.dot(x_ref[...], w_ref[...],
                             preferred_element_type=F32).astype(o_ref.dtype)

    if pad_blocks:
        i = pl.program_id(2)

        @pl.when(i < pad_blocks)
        def _():
            o_ref[...] = jnp.zeros_like(o_ref)

        pl.when(i >= pad_blocks)(compute)
    else:
        compute()


def _mm(x, w, *, pad_blocks=0):
    b, s, k = x.shape
    m = w.shape[1]
    tn = min(MM_TN, m)
    grid = (m // tn, b, s // MM_TM + pad_blocks)
    return pl.pallas_call(
        functools.partial(_mm_kernel, pad_blocks=pad_blocks),
        out_shape=jax.ShapeDtypeStruct((b, s + pad_blocks * MM_TM, m), BF16),
        grid=grid,
        in_specs=[
            pl.BlockSpec((None, MM_TM, k),
                         lambda j, bb, i: (bb, jnp.maximum(i - pad_blocks, 0), 0)),
            pl.BlockSpec((k, tn), lambda j, bb, i: (0, j)),
        ],
        out_specs=pl.BlockSpec((None, MM_TM, tn), lambda j, bb, i: (bb, i, j)),
        compiler_params=_params(("arbitrary", "arbitrary", "arbitrary")),
        name="dense_proj",
    )(x, w)


def _attn_a_kernel(q_ref, k_ref, v_ref, bias_ref, o_ref, sa_ref, sb_ref):
    i = pl.program_id(2)
    head0 = lax.broadcasted_iota(jnp.int32, (A_TQ, LANES), 1) < A_HEAD_DIM
    ones_col = jnp.where(lax.broadcasted_iota(jnp.int32, (A_WIN, LANES), 1) == 0, 1.0, 0.0).astype(BF16)

    def window(sb):
        g = i * A_SUB + sb
        return g, pl.multiple_of(g * A_TQ, A_TQ)

    def scores(sb, dst_ref):
        _, start = window(sb)
        q = q_ref[pl.ds(start + PAD_ROWS, A_TQ), :]
        zero = jnp.zeros_like(q)
        q2 = jnp.concatenate([jnp.where(head0, q, zero), jnp.where(head0, zero, q)], axis=0)
        dst_ref[...] = lax.dot_general(q2, k_ref[pl.ds(start, A_WIN), :], (((1,), (1,)), ((), ())),
                                       preferred_element_type=F32)

    def absorb(src_ref, sb):
        g, start = window(sb)
        s = src_ref[...] + bias_ref[jnp.minimum(g, A_TABLES - 1)]
        p = jnp.exp2(s - jnp.max(s, axis=-1, keepdims=True))
        va = jnp.concatenate([v_ref[pl.ds(start, A_WIN), :], ones_col], axis=1)
        o = jnp.dot(p.astype(BF16), va, preferred_element_type=F32)
        o = o[:, 0:LANES] / o[:, LANES:LANES + 1]
        rows = pl.ds(pl.multiple_of(sb * A_TQ, A_TQ), A_TQ)
        o_ref[rows, :] = jnp.where(head0, o[0:A_TQ], o[A_TQ:2 * A_TQ]).astype(o_ref.dtype)

    scores(0, sa_ref)

    def pair(jj, carry):
        scores(2 * jj + 1, sb_ref)
        absorb(sa_ref, 2 * jj)
        scores(2 * jj + 2, sa_ref)
        absorb(sb_ref, 2 * jj + 1)
        return carry

    lax.fori_loop(0, A_SUB // 2 - 1, pair, 0)
    scores(A_SUB - 1, sb_ref)
    absorb(sa_ref, A_SUB - 2)
    absorb(sb_ref, A_SUB - 1)


def _attn_a(qkv_pad, bias_tab, s):
    b = qkv_pad.shape[0]
    d = A_HEADS * A_HEAD_DIM
    pairs = d // LANES
    sp = s + PAD_ROWS
    rows = A_TQ * A_SUB
    return pl.pallas_call(
        _attn_a_kernel,
        out_shape=jax.ShapeDtypeStruct((b, s, d), BF16),
        grid=(pairs, b, s // rows),
        in_specs=[
            pl.BlockSpec((None, sp, LANES), lambda j, bb, i: (bb, 0, j)),
            pl.BlockSpec((None, sp, LANES), lambda j, bb, i: (bb, 0, pairs + j)),
            pl.BlockSpec((None, sp, LANES), lambda j, bb, i: (bb, 0, 2 * pairs + j)),
            pl.BlockSpec((None, A_TABLES, 2 * A_TQ, A_WIN), lambda j, bb, i: (j, 0, 0, 0)),
        ],
        out_specs=pl.BlockSpec((None, rows, LANES), lambda j, bb, i: (bb, i, j)),
        scratch_shapes=[pltpu.VMEM((2 * A_TQ, A_WIN), F32), pltpu.VMEM((2 * A_TQ, A_WIN), F32)],
        compiler_params=_params(("arbitrary", "arbitrary", "arbitrary")),
        name="mixer_a",
    )(qkv_pad, qkv_pad, qkv_pad, bias_tab)


def _rel_bias_table(rel_bias):
    qi = jnp.arange(A_TQ)[:, None]
    kj = jnp.arange(A_WIN)[None, :]
    dchunk = kj // CHUNK - qi // CHUNK
    inband = (dchunk >= 0) & (dchunk <= LEFT_CHUNKS)
    period = A_TQ + A_WIN
    u = jnp.arange(period)
    u = jnp.where(u < A_WIN, u, u - period)
    idx = jnp.clip(PAD_ROWS - u, -REL_CLIP, REL_CLIP) + REL_CLIP
    g = rel_bias.astype(F32)[:, idx] * LOG2E
    flat = jnp.tile(g, (1, A_TQ))[:, :A_TQ * (period - 1)]
    tab = flat.reshape(-1, A_TQ, period - 1)[:, :, :A_WIN]
    first_real = PAD_ROWS - A_TQ * jnp.arange(A_TABLES)[:, None, None]
    keep = inband[None] & (kj[None] >= first_real)
    tabs = jnp.where(keep[None], tab[:, None], NEG)
    tabs = tabs.reshape(-1, 2, A_TABLES, A_TQ, A_WIN).transpose(0, 2, 1, 3, 4)
    return tabs.reshape(-1, A_TABLES, 2 * A_TQ, A_WIN)


def _layer_norm(z, gain, bias):
    mu = jnp.mean(z, axis=-1, keepdims=True)
    zc = z - mu
    var = jnp.mean(zc * zc, axis=-1, keepdims=True)
    return zc * lax.rsqrt(var + LN_EPS) * gain + bias


def _small_proj_t(y, yb, w_ref, n_out):
    ylo = (y - yb.astype(F32)).astype(BF16)
    w = w_ref[...]
    wh = w.astype(BF16)
    wl = (w - wh.astype(F32)).astype(BF16)
    acc = jnp.dot(yb, wh, preferred_element_type=F32)
    acc = acc + jnp.dot(ylo, wh, preferred_element_type=F32)
    acc = acc + jnp.dot(yb, wl, preferred_element_type=F32)
    return acc.T[:n_out, :]


def _pad_small_weight(w):
    k, n = w.shape
    return jnp.zeros((k, LANES), F32).at[:, :n].set(w.astype(F32))


def _route(lt):
    m = jnp.max(lt, axis=0, keepdims=True)
    e = jnp.exp(lt - m)
    p = e / jnp.sum(e, axis=0, keepdims=True)
    rows = [p[k:k + 1, :] for k in range(N_EXPERTS)]
    best = None
    gi = None
    for g in range(N_GROUPS):
        r = rows[EXPERTS_PER_GROUP * g:EXPERTS_PER_GROUP * (g + 1)]
        sc = None
        for a in range(EXPERTS_PER_GROUP):
            for c in range(a + 1, EXPERTS_PER_GROUP):
                pair = r[a] + r[c]
                sc = pair if sc is None else jnp.maximum(sc, pair)
        if g == 0:
            best = sc
            gi = jnp.zeros(sc.shape, jnp.int32)
        else:
            upd = sc > best
            best = jnp.where(upd, sc, best)
            gi = jnp.where(upd, g, gi)
    pin = []
    for k in range(EXPERTS_PER_GROUP):
        v = rows[(N_GROUPS - 1) * EXPERTS_PER_GROUP + k]
        for g in range(N_GROUPS - 2, -1, -1):
            v = jnp.where(gi == g, rows[g * EXPERTS_PER_GROUP + k], v)
        pin.append(v)
    v1 = pin[0]
    i1 = jnp.zeros(v1.shape, jnp.int32)
    for k in range(1, EXPERTS_PER_GROUP):
        upd = pin[k] > v1
        v1 = jnp.where(upd, pin[k], v1)
        i1 = jnp.where(upd, k, i1)
    v2 = jnp.full(v1.shape, -1.0, F32)
    i2 = jnp.zeros(v1.shape, jnp.int32)
    for k in range(EXPERTS_PER_GROUP):
        cand = jnp.where(i1 == k, -1.0, pin[k])
        upd = cand > v2
        v2 = jnp.where(upd, cand, v2)
        i2 = jnp.where(upd, k, i2)
    tot = v1 + v2
    base = gi * EXPERTS_PER_GROUP
    experts = jnp.concatenate([base + i1, base + i2], axis=0)
    gates = jnp.concatenate([v1 / tot, v2 / tot], axis=0)
    return experts, gates


def _post_attn_kernel(a_ref, w_ref, x_ref, g_ref, b_ref, rw_ref, rb_ref,
                      x1_ref, x1b_ref, ex_ref, gt_ref):
    h = jnp.dot(a_ref[...], w_ref[...], preferred_element_type=F32)
    y = _layer_norm(ALPHA * x_ref[...] + h, g_ref[...], b_ref[...])
    yb = y.astype(BF16)
    x1_ref[...] = y
    x1b_ref[...] = yb
    lt = _small_proj_t(y, yb, rw_ref, N_EXPERTS) + rb_ref[...]
    experts, gates = _route(lt)
    ex_ref[...] = experts
    gt_ref[...] = gates


def _post_attn(attn, w_o, x, gain, bias, rw, rb):
    n, d = x.shape
    tm = POST_TM
    row = lambda i: (i, 0)
    const = lambda i: (0, 0)
    return pl.pallas_call(
        _post_attn_kernel,
        out_shape=(jax.ShapeDtypeStruct((n, d), F32),
                   jax.ShapeDtypeStruct((n, d), BF16),
                   jax.ShapeDtypeStruct((2, n), jnp.int32),
                   jax.ShapeDtypeStruct((2, n), F32)),
        grid=(n // tm,),
        in_specs=[
            pl.BlockSpec((tm, d), row),
            pl.BlockSpec((d, d), const, pipeline_mode=pl.Buffered(1)),
            pl.BlockSpec((tm, d), row),
            pl.BlockSpec((1, d), const),
            pl.BlockSpec((1, d), const),
            pl.BlockSpec((d, LANES), const),
            pl.BlockSpec((N_EXPERTS, 1), const),
        ],
        out_specs=(pl.BlockSpec((tm, d), row),
                   pl.BlockSpec((tm, d), row),
                   pl.BlockSpec((2, tm), lambda i: (0, i)),
                   pl.BlockSpec((2, tm), lambda i: (0, i))),
        compiler_params=_params(("arbitrary",)),
        name="out_proj_ln_router",
    )(attn, w_o, x, gain, bias, rw, rb)


def _moe_kernel(iblk, iex, ivalid, ifirst, starts, ends, x_ref, wg_ref, wu_ref, wd_ref, o_ref,
                acc_ref):
    w = pl.program_id(0)
    f = pl.program_id(1)

    @pl.when((ifirst[w] == 1) & (f == 0))
    def _():
        acc_ref[...] = jnp.zeros_like(acc_ref)

    @pl.when(ivalid[w] == 1)
    def _():
        x = x_ref[...]
        hg = jnp.dot(x, wg_ref[...], preferred_element_type=F32)
        hu = jnp.dot(x, wu_ref[...], preferred_element_type=F32)
        e = iex[w]
        row = iblk[w] * MOE_BLK + lax.broadcasted_iota(jnp.int32, (MOE_BLK, 1), 0)
        mine = (row >= starts[e]) & (row < ends[e])
        hid = jnp.where(mine, hg * jax.nn.sigmoid(hg) * hu, 0.0)
        acc_ref[...] += jnp.dot(hid.astype(BF16), wd_ref[...], preferred_element_type=F32)

        @pl.when(f == pl.num_programs(1) - 1)
        def _():
            o_ref[...] = acc_ref[...].astype(o_ref.dtype)


def _moe(xs, w_gate, w_up, w_down, layer, meta):
    a, d = xs.shape
    dff = w_gate.shape[-1]
    nf = dff // MOE_TF
    n_items = meta[0].shape[0]

    def fsel(w, f, ivalid):
        return jnp.where(ivalid[w] == 1, f, nf - 1)

    grid_spec = pltpu.PrefetchScalarGridSpec(
        num_scalar_prefetch=6,
        grid=(n_items, nf),
        in_specs=[
            pl.BlockSpec((MOE_BLK, d), lambda w, f, ib, ie, iv, i1, st, en: (ib[w], 0)),
            pl.BlockSpec((None, None, d, MOE_TF),
                         lambda w, f, ib, ie, iv, i1, st, en: (layer, ie[w], 0, fsel(w, f, iv))),
            pl.BlockSpec((None, None, d, MOE_TF),
                         lambda w, f, ib, ie, iv, i1, st, en: (layer, ie[w], 0, fsel(w, f, iv))),
            pl.BlockSpec((None, None, MOE_TF, d),
                         lambda w, f, ib, ie, iv, i1, st, en: (layer, ie[w], fsel(w, f, iv), 0)),
        ],
        out_specs=pl.BlockSpec((MOE_BLK, d), lambda w, f, ib, ie, iv, i1, st, en: (ib[w], 0)),
        scratch_shapes=[pltpu.VMEM((MOE_BLK, d), F32)],
    )
    return pl.pallas_call(
        _moe_kernel,
        out_shape=jax.ShapeDtypeStruct((a, d), BF16),
        grid_spec=grid_spec,
        compiler_params=_params(("arbitrary", "arbitrary")),
        name="moe_experts",
    )(*meta, xs, w_gate, w_up, w_down)


def _moe_plan(experts):
    n = experts.shape[1]
    a = 2 * n
    nblk = a // MOE_BLK
    n_items = nblk + N_EXPERTS - 1
    e_flat = experts.reshape(-1)
    iota = jnp.arange(a, dtype=jnp.int32)
    e_sorted, order = lax.sort((e_flat, iota), num_keys=1, is_stable=True)
    tok_sorted = order % n
    _, pos = lax.sort((order, iota), num_keys=1)
    pos = pos.reshape(2, n)
    eids = jnp.arange(N_EXPERTS, dtype=jnp.int32)
    starts = jnp.searchsorted(e_sorted, eids, side="left").astype(jnp.int32)
    ends = jnp.searchsorted(e_sorted, eids, side="right").astype(jnp.int32)
    e_lo = e_sorted[0::MOE_BLK]
    e_hi = e_sorted[MOE_BLK - 1::MOE_BLK]
    per_blk = e_hi - e_lo + 1
    iend = jnp.cumsum(per_blk).astype(jnp.int32)
    istart = iend - per_blk
    total = iend[-1]
    w = jnp.arange(n_items, dtype=jnp.int32)
    blk = jnp.clip(jnp.searchsorted(iend, w, side="right"), 0, nblk - 1).astype(jnp.int32)
    ex = e_lo[blk] + (w - istart[blk])
    valid = w < total
    blk = jnp.where(valid, blk, nblk - 1)
    ex = jnp.where(valid, ex, e_hi[nblk - 1])
    first = valid & (w == istart[blk])
    meta = (blk, ex.astype(jnp.int32), valid.astype(jnp.int32), first.astype(jnp.int32),
            starts, ends)
    return meta, tok_sorted, pos


def _combine_ln_kernel(x_ref, y0_ref, y1_ref, gt_ref, g_ref, b_ref, o_ref, ob_ref):
    gt = gt_ref[...]
    y = y0_ref[...].astype(F32) * gt[:, 0:1] + y1_ref[...].astype(F32) * gt[:, 1:2]
    out = _layer_norm(ALPHA * x_ref[...] + y, g_ref[...], b_ref[...])
    o_ref[...] = out
    ob_ref[...] = out.astype(BF16)


def _combine_ln(x, y0, y1, gates_t, gain, bias):
    n, d = x.shape
    tm = LN_TM
    row = lambda i: (i, 0)
    const = lambda i: (0, 0)
    return pl.pallas_call(
        _combine_ln_kernel,
        out_shape=(jax.ShapeDtypeStruct((n, d), F32), jax.ShapeDtypeStruct((n, d), BF16)),
        grid=(n // tm,),
        in_specs=[
            pl.BlockSpec((tm, d), row),
            pl.BlockSpec((tm, d), row),
            pl.BlockSpec((tm, d), row),
            pl.BlockSpec((tm, 2), row),
            pl.BlockSpec((1, d), const),
            pl.BlockSpec((1, d), const),
        ],
        out_specs=(pl.BlockSpec((tm, d), row), pl.BlockSpec((tm, d), row)),
        compiler_params=_params(("arbitrary",)),
        name="moe_combine_ln",
    )(x, y0, y1, gates_t, gain, bias)


def _fg_kernel(x_ref, w_ref, b_ref, tri_ref, qf_ref, kf_ref, c_ref, carry_ref, koff_ref):
    i = pl.program_id(1)

    @pl.when(i == 0)
    def _():
        carry_ref[...] = jnp.zeros_like(carry_ref)

    @pl.when(i % (FOX_TK // FOX_T) == 0)
    def _():
        koff_ref[...] = jnp.zeros_like(koff_ref)

    def split3(v):
        hi = v.astype(BF16)
        r1 = v - hi.astype(F32)
        mid = r1.astype(BF16)
        lo = (r1 - mid.astype(F32)).astype(BF16)
        return hi, mid, lo

    y = x_ref[...]
    yb = y.astype(BF16)
    ylo = (y - yb.astype(F32)).astype(BF16)
    w = w_ref[...]
    wh = w.astype(BF16)
    wl = (w - wh.astype(F32)).astype(BF16)
    z = (jnp.dot(yb, wh, preferred_element_type=F32) + jnp.dot(ylo, wh, preferred_element_type=F32)
         + jnp.dot(yb, wl, preferred_element_type=F32)) + b_ref[...]
    lf = (jnp.minimum(z, 0.0) - jnp.log1p(jnp.exp(-jnp.abs(z)))) * LOG2E
    parts = jnp.concatenate(split3(lf), axis=1)
    cs3 = jnp.dot(tri_ref[...], parts, preferred_element_type=F32)
    cs = cs3[:, 0:LANES] + cs3[:, LANES:2 * LANES] + cs3[:, 2 * LANES:3 * LANES]
    c_ref[...] = carry_ref[...]
    carry_ref[...] = carry_ref[...] + cs[FOX_T - 1:FOX_T, :]
    csk = cs + koff_ref[...]
    koff_ref[...] = koff_ref[...] + cs[FOX_T - 1:FOX_T, :]

    hi, mid, lo = (p.astype(F32) for p in split3(cs))
    khi, kmid, klo = (p.astype(F32) for p in split3(csk))
    lane = lax.broadcasted_iota(jnp.int32, (FOX_T, LANES), 1)
    q_ones = jnp.where((lane >= 3) & (lane < 6), 1.0, 0.0).astype(F32)
    k_ones = jnp.where(lane < 3, 1.0, 0.0).astype(F32)
    for h in range(B_HEADS):
        hc, mc, lc = hi[:, h:h + 1], mid[:, h:h + 1], lo[:, h:h + 1]
        tq = jnp.where(lane == 0, hc, jnp.where(lane == 1, mc, jnp.where(lane == 2, lc, q_ones)))
        hc, mc, lc = khi[:, h:h + 1], kmid[:, h:h + 1], klo[:, h:h + 1]
        tk = jnp.where(lane == 3, -hc, jnp.where(lane == 4, -mc, jnp.where(lane == 5, -lc, k_ones)))
        qf_ref[h] = tq.astype(BF16)
        kf_ref[h] = tk.astype(BF16)


def _forget_columns(x, fw, fb):
    b, s, d = x.shape
    t = FOX_T
    nblk = s // t
    tri = (jnp.arange(t)[:, None] >= jnp.arange(t)[None, :]).astype(BF16)
    const = lambda bb, i: (0, 0)
    cols = jax.ShapeDtypeStruct((b, B_HEADS, s, LANES), BF16)
    col_spec = pl.BlockSpec((None, B_HEADS, t, LANES), lambda bb, i: (bb, 0, i, 0))
    return pl.pallas_call(
        _fg_kernel,
        out_shape=(cols, cols, jax.ShapeDtypeStruct((b, nblk, 1, LANES), F32)),
        grid=(b, nblk),
        in_specs=[
            pl.BlockSpec((None, t, d), lambda bb, i: (bb, i, 0)),
            pl.BlockSpec((d, LANES), const),
            pl.BlockSpec((1, LANES), const),
            pl.BlockSpec((t, t), const),
        ],
        out_specs=(col_spec, col_spec,
                   pl.BlockSpec((None, None, 1, LANES), lambda bb, i: (bb, i, 0, 0))),
        scratch_shapes=[pltpu.VMEM((1, LANES), F32), pltpu.VMEM((1, LANES), F32)],
        compiler_params=_params(("arbitrary", "arbitrary")),
        name="forget_cumsum",
    )(x, fw, fb, tri)


def _fox_kernel(c_tab, q_ref, qf_ref, k_ref, kf_ref, v_ref, o_ref,
                qa_ref, sa_ref, sb_ref, m_ref, acc_ref):
    i = pl.program_id(2)
    tq, tk = FOX_T, FOX_TK
    per_tile = tk // tq
    cbase = (pl.program_id(0) * B_HEADS + pl.program_id(1)) * pl.num_programs(2)
    c_q = jnp.full((1, LANES), c_tab[cbase + i], F32)
    qa_ref[:, 0:B_HEAD_DIM] = q_ref[...]
    qa_ref[:, B_HEAD_DIM:2 * B_HEAD_DIM] = qf_ref[...]
    m_ref[...] = jnp.full(m_ref.shape, NEG, F32)
    acc_ref[...] = jnp.zeros_like(acc_ref)
    ones_col = jnp.where(lax.broadcasted_iota(jnp.int32, (tk, LANES), 1) == 0, 1.0, 0.0).astype(BF16)

    def scores(jt, dst_ref):
        ks = pl.multiple_of(jt * tk, tk)
        ka = jnp.concatenate([k_ref[pl.ds(ks, tk), :], kf_ref[pl.ds(ks, tk), :]], axis=1)
        dst_ref[...] = lax.dot_general(qa_ref[...], ka, (((1,), (1,)), ((), ())),
                                       preferred_element_type=F32)

    def absorb(src_ref, jt, last):
        ks = pl.multiple_of(jt * tk, tk)
        s = src_ref[...]
        if last:
            row = lax.broadcasted_iota(jnp.int32, (tq, tk), 0) + (i - per_tile * jt) * tq
            col = lax.broadcasted_iota(jnp.int32, (tq, tk), 1)
            s = jnp.where(col <= row, s, NEG)
        shift = c_q - jnp.full((1, LANES), c_tab[cbase + per_tile * jt], F32)
        m_prev = m_ref[...]
        m_new = jnp.maximum(m_prev, jnp.max(s, axis=-1, keepdims=True) + shift)
        p = jnp.exp2(s - jnp.concatenate([m_new - shift] * (tk // LANES), axis=1))
        scale = jnp.exp2(m_prev - m_new)
        va = jnp.concatenate([v_ref[pl.ds(ks, tk), :], ones_col], axis=1)
        acc_ref[...] = jnp.concatenate([scale, scale], axis=1) * acc_ref[...] + jnp.dot(
            p.astype(BF16), va, preferred_element_type=F32)
        m_ref[...] = m_new

    n_full = lax.shift_right_logical(i, 1)
    scores(0, sa_ref)

    def pair(first):
        scores(first + 1, sb_ref)
        absorb(sa_ref, first, False)
        scores(first + 2, sa_ref)
        absorb(sb_ref, first + 1, False)

    def quad(jj, carry):
        pair(4 * jj)
        pair(4 * jj + 2)
        return carry

    n_quads = lax.shift_right_logical(n_full, 2)
    lax.fori_loop(0, n_quads, quad, 0)

    @pl.when((n_full & 2) != 0)
    def _():
        pair(n_quads * 4)

    @pl.when((n_full & 1) == 1)
    def _():
        scores(n_full, sb_ref)
        absorb(sa_ref, n_full - 1, False)
        absorb(sb_ref, n_full, True)

    @pl.when((n_full & 1) == 0)
    def _():
        absorb(sa_ref, n_full, True)

    acc = acc_ref[...]
    o_ref[...] = (acc[:, 0:B_HEAD_DIM] / acc[:, B_HEAD_DIM:B_HEAD_DIM + 1]).astype(o_ref.dtype)


def _fox(q, kv, qf, kf, c_tab):
    b, s, d = q.shape
    t = FOX_T
    hd = B_HEAD_DIM
    grid_spec = pltpu.PrefetchScalarGridSpec(
        num_scalar_prefetch=1,
        grid=(b, B_HEADS, s // t),
        in_specs=[
            pl.BlockSpec((None, t, hd), lambda bb, h, i, c: (bb, i, h)),
            pl.BlockSpec((None, None, t, LANES), lambda bb, h, i, c: (bb, h, i, 0)),
            pl.BlockSpec((None, s, hd), lambda bb, h, i, c: (bb, 0, h)),
            pl.BlockSpec((None, None, s, LANES), lambda bb, h, i, c: (bb, h, 0, 0)),
            pl.BlockSpec((None, s, hd), lambda bb, h, i, c: (bb, 0, B_HEADS + h)),
        ],
        out_specs=pl.BlockSpec((None, t, hd), lambda bb, h, i, c: (bb, i, h)),
        scratch_shapes=[
            pltpu.VMEM((t, 2 * hd), BF16),
            pltpu.VMEM((t, FOX_TK), F32),
            pltpu.VMEM((t, FOX_TK), F32),
            pltpu.VMEM((t, LANES), F32),
            pltpu.VMEM((t, 2 * hd), F32),
        ],
    )
    return pl.pallas_call(
        _fox_kernel,
        out_shape=jax.ShapeDtypeStruct((b, s, d), BF16),
        grid_spec=grid_spec,
        compiler_params=_params(("arbitrary", "arbitrary", "arbitrary")),
        name="mixer_b",
    )(c_tab, q, qf, kv, kf, kv)


def _moe_layer(x1, x1b, experts, gates, w_gate, w_up, w_down, layer, gain, bias):
    meta, tok_sorted, pos = _moe_plan(experts)
    xs = x1b[tok_sorted]
    yb = _moe(xs, w_gate, w_up, w_down, layer, meta)
    return _combine_ln(x1, yb[pos[0]], yb[pos[1]], gates.T, gain, bias)


def kernel(x, a_w_qkv, a_rel_bias, a_w_o, kv_w, fg_w, fg_b, b_w_q, b_w_o, router_w, router_b,
           moe_w_gate, moe_w_up, moe_w_down, ln_gain, ln_bias):
    b, s, d = x.shape
    n = b * s
    assert MM_TM == PAD_ROWS and s % MM_TM == 0 and s % FOX_TK == 0 and FOX_TK == 2 * FOX_T
    assert s % (A_TQ * A_SUB) == 0 and PAD_ROWS % A_TQ == 0 and A_SUB % 2 == 0 and A_SUB >= 4
    assert (2 * n) % MOE_BLK == 0

    rw = _pad_small_weight(router_w)
    rb = router_b.astype(F32).reshape(N_EXPERTS, 1)
    gain = ln_gain.astype(F32).reshape(DEPTH, 2, 1, d)
    bias = ln_bias.astype(F32).reshape(DEPTH, 2, 1, d)
    wg = moe_w_gate.astype(BF16)
    wu = moe_w_up.astype(BF16)
    wd = moe_w_down.astype(BF16)

    qscale = jnp.concatenate([jnp.full((d,), LOG2E / math.sqrt(A_HEAD_DIM), F32),
                              jnp.ones((2 * d,), F32)])
    w_qkv = (a_w_qkv[0] * qscale[None, :]).astype(BF16)
    qkv = _mm(x.astype(BF16), w_qkv, pad_blocks=1)
    attn = _attn_a(qkv, _rel_bias_table(a_rel_bias[0]), s)
    x1, x1b, experts, gates = _post_attn(attn.reshape(n, d), a_w_o[0].astype(BF16),
                                         x.reshape(n, d), gain[0, 0], bias[0, 0], rw, rb)
    x2, x2b = _moe_layer(x1, x1b, experts, gates, wg, wu, wd, 0, gain[0, 1], bias[0, 1])

    kv = _mm(x2b.reshape(b, s, d), kv_w.astype(BF16))
    fb = jnp.zeros((1, LANES), F32).at[0, :B_HEADS].set(fg_b.astype(F32))
    qf, kf, carry_in = _forget_columns(x2.reshape(b, s, d), _pad_small_weight(fg_w), fb)
    c_tab = carry_in[:, :, 0, :B_HEADS].transpose(0, 2, 1).reshape(-1)

    w_q = (b_w_q[0] * (LOG2E / math.sqrt(B_HEAD_DIM))).astype(BF16)
    q = _mm(x2b.reshape(b, s, d), w_q)
    attn = _fox(q, kv, qf, kf, c_tab)
    x3, x3b, experts, gates = _post_attn(attn.reshape(n, d), b_w_o[0].astype(BF16), x2,
                                         gain[1, 0], bias[1, 0], rw, rb)
    x4, _ = _moe_layer(x3, x3b, experts, gates, wg, wu, wd, 1, gain[1, 1], bias[1, 1])
    return x4.reshape(b, s, d)
```

```python
import functools
import math

import jax
import jax.numpy as jnp
from jax import lax
from jax.experimental import pallas as pl
from jax.experimental.pallas import tpu as pltpu

F32 = jnp.float32
BF16 = jnp.bfloat16

CHUNK = 64
LEFT_CHUNKS = 8
A_HEADS = 32
A_HEAD_DIM = 64
REL_CLIP = 2 * CHUNK
B_HEADS = 16
B_HEAD_DIM = 128
N_EXPERTS = 16
N_GROUPS = 4
EXPERTS_PER_GROUP = 4
DEPTH = 2
ALPHA = (2.0 * DEPTH) ** 0.25
LN_EPS = 1e-5
NEG = -1e30
LOG2E = 1.4426950408889634

LANES = 128
VMEM_LIMIT = 56 * 1024 * 1024

MM_TM = 512
MM_TN = 1024
PAD_ROWS = LEFT_CHUNKS * CHUNK
A_TQ = 256
A_WIN = A_TQ + PAD_ROWS
A_SUB = 16
A_TABLES = PAD_ROWS // A_TQ + 1
POST_TM = 256
MOE_BLK = 1024
MOE_CHUNK = 256
MOE_TF = 256
LN_TM = 512
FOX_T = 512
FOX_TK = 2 * FOX_T


def _params(sem):
    return pltpu.CompilerParams(dimension_semantics=sem, vmem_limit_bytes=VMEM_LIMIT)


def _mm_kernel(x_ref, w_ref, o_ref, *, pad_blocks):
    def compute():
        o_ref[...] = jnp.dot(x_ref[...], w_ref[...],
                             preferred_element_type=F32).astype(o_ref.dtype)

    if pad_blocks:
        i = pl.program_id(2)

        @pl.when(i < pad_blocks)
        def _():
            o_ref[...] = jnp.zeros_like(o_ref)

        pl.when(i >= pad_blocks)(compute)
    else:
        compute()


def _mm(x, w, *, pad_blocks=0):
    b, s, k = x.shape
    m = w.shape[1]
    tn = min(MM_TN, m)
    grid = (m // tn, b, s // MM_TM + pad_blocks)
    return pl.pallas_call(
        functools.partial(_mm_kernel, pad_blocks=pad_blocks),
        out_shape=jax.ShapeDtypeStruct((b, s + pad_blocks * MM_TM, m), BF16),
        grid=grid,
        in_specs=[
            pl.BlockSpec((None, MM_TM, k),
                         lambda j, bb, i: (bb, jnp.maximum(i - pad_blocks, 0), 0)),
            pl.BlockSpec((k, tn), lambda j, bb, i: (0, j)),
        ],
        out_specs=pl.BlockSpec((None, MM_TM, tn), lambda j, bb, i: (bb, i, j)),
        compiler_params=_params(("arbitrary", "arbitrary", "arbitrary")),
        name="dense_proj",
    )(x, w)


def _attn_a_kernel(q_ref, k_ref, v_ref, bias_ref, o_ref, sa_ref, sb_ref):
    i = pl.program_id(2)
    head0 = lax.broadcasted_iota(jnp.int32, (A_TQ, LANES), 1) < A_HEAD_DIM
    ones_col = jnp.where(lax.broadcasted_iota(jnp.int32, (A_WIN, LANES), 1) == 0, 1.0, 0.0).astype(BF16)

    def window(sb):
        g = i * A_SUB + sb
        return g, pl.multiple_of(g * A_TQ, A_TQ)

    def scores(sb, dst_ref):
        _, start = window(sb)
        q = q_ref[pl.ds(start + PAD_ROWS, A_TQ), :]
        zero = jnp.zeros_like(q)
        q2 = jnp.concatenate([jnp.where(head0, q, zero), jnp.where(head0, zero, q)], axis=0)
        dst_ref[...] = lax.dot_general(q2, k_ref[pl.ds(start, A_WIN), :], (((1,), (1,)), ((), ())),
                                       preferred_element_type=F32)

    def absorb(src_ref, sb):
        g, start = window(sb)
        s = src_ref[...] + bias_ref[jnp.minimum(g, A_TABLES - 1)]
        p = jnp.exp2(s - jnp.max(s, axis=-1, keepdims=True))
        va = jnp.concatenate([v_ref[pl.ds(start, A_WIN), :], ones_col], axis=1)
        o = jnp.dot(p.astype(BF16), va, preferred_element_type=F32)
        o = o[:, 0:LANES] / o[:, LANES:LANES + 1]
        rows = pl.ds(pl.multiple_of(sb * A_TQ, A_TQ), A_TQ)
        o_ref[rows, :] = jnp.where(head0, o[0:A_TQ], o[A_TQ:2 * A_TQ]).astype(o_ref.dtype)

    scores(0, sa_ref)

    def pair(jj, carry):
        scores(2 * jj + 1, sb_ref)
        absorb(sa_ref, 2 * jj)
        scores(2 * jj + 2, sa_ref)
        absorb(sb_ref, 2 * jj + 1)
        return carry

    lax.fori_loop(0, A_SUB // 2 - 1, pair, 0)
    scores(A_SUB - 1, sb_ref)
    absorb(sa_ref, A_SUB - 2)
    absorb(sb_ref, A_SUB - 1)


def _attn_a(qkv_pad, bias_tab, s):
    b = qkv_pad.shape[0]
    d = A_HEADS * A_HEAD_DIM
    pairs = d // LANES
    sp = s + PAD_ROWS
    rows = A_TQ * A_SUB
    return pl.pallas_call(
        _attn_a_kernel,
        out_shape=jax.ShapeDtypeStruct((b, s, d), BF16),
        grid=(pairs, b, s // rows),
        in_specs=[
            pl.BlockSpec((None, sp, LANES), lambda j, bb, i: (bb, 0, j)),
            pl.BlockSpec((None, sp, LANES), lambda j, bb, i: (bb, 0, pairs + j)),
            pl.BlockSpec((None, sp, LANES), lambda j, bb, i: (bb, 0, 2 * pairs + j)),
            pl.BlockSpec((None, A_TABLES, 2 * A_TQ, A_WIN), lambda j, bb, i: (j, 0, 0, 0)),
        ],
        out_specs=pl.BlockSpec((None, rows, LANES), lambda j, bb, i: (bb, i, j)),
        scratch_shapes=[pltpu.VMEM((2 * A_TQ, A_WIN), F32), pltpu.VMEM((2 * A_TQ, A_WIN), F32)],
        compiler_params=_params(("arbitrary", "arbitrary", "arbitrary")),
        name="mixer_a",
    )(qkv_pad, qkv_pad, qkv_pad, bias_tab)


def _rel_bias_table(rel_bias):
    qi = jnp.arange(A_TQ)[:, None]
    kj = jnp.arange(A_WIN)[None, :]
    dchunk = kj // CHUNK - qi // CHUNK
    inband = (dchunk >= 0) & (dchunk <= LEFT_CHUNKS)
    period = A_TQ + A_WIN
    u = jnp.arange(period)
    u = jnp.where(u < A_WIN, u, u - period)
    idx = jnp.clip(PAD_ROWS - u, -REL_CLIP, REL_CLIP) + REL_CLIP
    g = rel_bias.astype(F32)[:, idx] * LOG2E
    flat = jnp.tile(g, (1, A_TQ))[:, :A_TQ * (period - 1)]
    tab = flat.reshape(-1, A_TQ, period - 1)[:, :, :A_WIN]
    first_real = PAD_ROWS - A_TQ * jnp.arange(A_TABLES)[:, None, None]
    keep = inband[None] & (kj[None] >= first_real)
    tabs = jnp.where(keep[None], tab[:, None], NEG)
    tabs = tabs.reshape(-1, 2, A_TABLES, A_TQ, A_WIN).transpose(0, 2, 1, 3, 4)
    return tabs.reshape(-1, A_TABLES, 2 * A_TQ, A_WIN)


def _layer_norm(z, gain, bias):
    mu = jnp.mean(z, axis=-1, keepdims=True)
    zc = z - mu
    var = jnp.mean(zc * zc, axis=-1, keepdims=True)
    return zc * lax.rsqrt(var + LN_EPS) * gain + bias


def _small_proj_t(y, yb, w_ref, n_out):
    ylo = (y - yb.astype(F32)).astype(BF16)
    w = w_ref[...]
    wh = w.astype(BF16)
    wl = (w - wh.astype(F32)).astype(BF16)
    acc = jnp.dot(yb, wh, preferred_element_type=F32)
    acc = acc + jnp.dot(ylo, wh, preferred_element_type=F32)
    acc = acc + jnp.dot(yb, wl, preferred_element_type=F32)
    return acc.T[:n_out, :]


def _pad_small_weight(w):
    k, n = w.shape
    return jnp.zeros((k, LANES), F32).at[:, :n].set(w.astype(F32))


def _route(lt):
    m = jnp.max(lt, axis=0, keepdims=True)
    e = jnp.exp(lt - m)
    p = e / jnp.sum(e, axis=0, keepdims=True)
    rows = [p[k:k + 1, :] for k in range(N_EXPERTS)]
    best = None
    gi = None
    for g in range(N_GROUPS):
        r = rows[EXPERTS_PER_GROUP * g:EXPERTS_PER_GROUP * (g + 1)]
        sc = None
        for a in range(EXPERTS_PER_GROUP):
            for c in range(a + 1, EXPERTS_PER_GROUP):
                pair = r[a] + r[c]
                sc = pair if sc is None else jnp.maximum(sc, pair)
        if g == 0:
            best = sc
            gi = jnp.zeros(sc.shape, jnp.int32)
        else:
            upd = sc > best
            best = jnp.where(upd, sc, best)
            gi = jnp.where(upd, g, gi)
    pin = []
    for k in range(EXPERTS_PER_GROUP):
        v = rows[(N_GROUPS - 1) * EXPERTS_PER_GROUP + k]
        for g in range(N_GROUPS - 2, -1, -1):
            v = jnp.where(gi == g, rows[g * EXPERTS_PER_GROUP + k], v)
        pin.append(v)
    v1 = pin[0]
    i1 = jnp.zeros(v1.shape, jnp.int32)
    for k in range(1, EXPERTS_PER_GROUP):
        upd = pin[k] > v1
        v1 = jnp.where(upd, pin[k], v1)
        i1 = jnp.where(upd, k, i1)
    v2 = jnp.full(v1.shape, -1.0, F32)
    i2 = jnp.zeros(v1.shape, jnp.int32)
    for k in range(EXPERTS_PER_GROUP):
        cand = jnp.where(i1 == k, -1.0, pin[k])
        upd = cand > v2
        v2 = jnp.where(upd, cand, v2)
        i2 = jnp.where(upd, k, i2)
    tot = v1 + v2
    base = gi * EXPERTS_PER_GROUP
    experts = jnp.concatenate([base + i1, base + i2], axis=0)
    gates = jnp.concatenate([v1 / tot, v2 / tot], axis=0)
    return experts, gates


def _post_attn_kernel(a_ref, w_ref, x_ref, g_ref, b_ref, rw_ref, rb_ref,
                      x1_ref, x1b_ref, ex_ref, gt_ref):
    h = jnp.dot(a_ref[...], w_ref[...], preferred_element_type=F32)
    y = _layer_norm(ALPHA * x_ref[...] + h, g_ref[...], b_ref[...])
    yb = y.astype(BF16)
    x1_ref[...] = y
    x1b_ref[...] = yb
    lt = _small_proj_t(y, yb, rw_ref, N_EXPERTS) + rb_ref[...]
    experts, gates = _route(lt)
    ex_ref[...] = experts
    gt_ref[...] = gates


def _post_attn(attn, w_o, x, gain, bias, rw, rb):
    n, d = x.shape
    tm = POST_TM
    row = lambda i: (i, 0)
    const = lambda i: (0, 0)
    return pl.pallas_call(
        _post_attn_kernel,
        out_shape=(jax.ShapeDtypeStruct((n, d), F32),
                   jax.ShapeDtypeStruct((n, d), BF16),
                   jax.ShapeDtypeStruct((2, n), jnp.int32),
                   jax.ShapeDtypeStruct((2, n), F32)),
        grid=(n // tm,),
        in_specs=[
            pl.BlockSpec((tm, d), row),
            pl.BlockSpec((d, d), const),
            pl.BlockSpec((tm, d), row),
            pl.BlockSpec((1, d), const),
            pl.BlockSpec((1, d), const),
            pl.BlockSpec((d, LANES), const),
            pl.BlockSpec((N_EXPERTS, 1), const),
        ],
        out_specs=(pl.BlockSpec((tm, d), row),
                   pl.BlockSpec((tm, d), row),
                   pl.BlockSpec((2, tm), lambda i: (0, i)),
                   pl.BlockSpec((2, tm), lambda i: (0, i))),
        compiler_params=_params(("arbitrary",)),
        name="out_proj_ln_router",
    )(attn, w_o, x, gain, bias, rw, rb)


def _moe_kernel(iblk, iex, ivalid, ifirst, starts, ends, x_ref, wg_ref, wu_ref, wd_ref, o_ref,
                acc_ref):
    w = pl.program_id(0)
    f = pl.program_id(1)

    @pl.when((ifirst[w] == 1) & (f == 0))
    def _():
        acc_ref[...] = jnp.zeros_like(acc_ref)

    @pl.when(ivalid[w] == 1)
    def _():
        wg = wg_ref[...].astype(BF16)
        wu = wu_ref[...].astype(BF16)
        wd = wd_ref[...].astype(BF16)
        e = iex[w]
        lo = starts[e]
        hi = ends[e]
        base = iblk[w] * MOE_BLK

        def swiglu(x, keep):
            hg = jnp.dot(x, wg, preferred_element_type=F32)
            hu = jnp.dot(x, wu, preferred_element_type=F32)
            hid = hg * jax.nn.sigmoid(hg) * hu
            if keep is not None:
                hid = jnp.where(keep, hid, 0.0)
            return jnp.dot(hid.astype(BF16), wd, preferred_element_type=F32)

        whole = (lo <= base) & (hi >= base + MOE_BLK)

        @pl.when(whole)
        def _():
            acc_ref[...] += swiglu(x_ref[...], None)

        @pl.when(jnp.logical_not(whole))
        def _():
            for c in range(MOE_BLK // MOE_CHUNK):
                r0 = base + c * MOE_CHUNK

                @pl.when((lo < r0 + MOE_CHUNK) & (hi > r0))
                def _():
                    rows = slice(c * MOE_CHUNK, (c + 1) * MOE_CHUNK)
                    row = r0 + lax.broadcasted_iota(jnp.int32, (MOE_CHUNK, 1), 0)
                    acc_ref[rows, :] += swiglu(x_ref[rows, :], (row >= lo) & (row < hi))

        @pl.when(f == pl.num_programs(1) - 1)
        def _():
            o_ref[...] = acc_ref[...].astype(o_ref.dtype)


def _moe(xs, w_gate, w_up, w_down, layer, meta):
    a, d = xs.shape
    dff = w_gate.shape[-1]
    nf = dff // MOE_TF
    n_items = meta[0].shape[0]

    def fsel(w, f, ivalid):
        return jnp.where(ivalid[w] == 1, f, nf - 1)

    grid_spec = pltpu.PrefetchScalarGridSpec(
        num_scalar_prefetch=6,
        grid=(n_items, nf),
        in_specs=[
            pl.BlockSpec((MOE_BLK, d), lambda w, f, ib, ie, iv, i1, st, en: (ib[w], 0)),
            pl.BlockSpec((None, None, d, MOE_TF),
                         lambda w, f, ib, ie, iv, i1, st, en: (layer, ie[w], 0, fsel(w, f, iv))),
            pl.BlockSpec((None, None, d, MOE_TF),
                         lambda w, f, ib, ie, iv, i1, st, en: (layer, ie[w], 0, fsel(w, f, iv))),
            pl.BlockSpec((None, None, MOE_TF, d),
                         lambda w, f, ib, ie, iv, i1, st, en: (layer, ie[w], fsel(w, f, iv), 0)),
        ],
        out_specs=pl.BlockSpec((MOE_BLK, d), lambda w, f, ib, ie, iv, i1, st, en: (ib[w], 0)),
        scratch_shapes=[pltpu.VMEM((MOE_BLK, d), F32)],
    )
    return pl.pallas_call(
        _moe_kernel,
        out_shape=jax.ShapeDtypeStruct((a, d), BF16),
        grid_spec=grid_spec,
        compiler_params=_params(("arbitrary", "arbitrary")),
        name="moe_experts",
    )(*meta, xs, w_gate, w_up, w_down)


def _moe_plan(experts):
    n = experts.shape[1]
    a = 2 * n
    nblk = a // MOE_BLK
    n_items = nblk + N_EXPERTS - 1
    e_flat = experts.reshape(-1)
    iota = jnp.arange(a, dtype=jnp.int32)
    e_sorted, order = lax.sort((e_flat, iota), num_keys=1, is_stable=True)
    tok_sorted = order % n
    _, pos = lax.sort((order, iota), num_keys=1)
    pos = pos.reshape(2, n)
    eids = jnp.arange(N_EXPERTS, dtype=jnp.int32)
    starts = jnp.searchsorted(e_sorted, eids, side="left").astype(jnp.int32)
    ends = jnp.searchsorted(e_sorted, eids, side="right").astype(jnp.int32)
    e_lo = e_sorted[0::MOE_BLK]
    e_hi = e_sorted[MOE_BLK - 1::MOE_BLK]
    per_blk = e_hi - e_lo + 1
    iend = jnp.cumsum(per_blk).astype(jnp.int32)
    istart = iend - per_blk
    total = iend[-1]
    w = jnp.arange(n_items, dtype=jnp.int32)
    blk = jnp.clip(jnp.searchsorted(iend, w, side="right"), 0, nblk - 1).astype(jnp.int32)
    ex = e_lo[blk] + (w - istart[blk])
    valid = w < total
    blk = jnp.where(valid, blk, nblk - 1)
    ex = jnp.where(valid, ex, e_hi[nblk - 1])
    first = valid & (w == istart[blk])
    meta = (blk, ex.astype(jnp.int32), valid.astype(jnp.int32), first.astype(jnp.int32),
            starts, ends)
    return meta, tok_sorted, pos


def _combine_ln_kernel(x_ref, y0_ref, y1_ref, gt_ref, g_ref, b_ref, o_ref, ob_ref):
    gt = gt_ref[...]
    y = y0_ref[...].astype(F32) * gt[:, 0:1] + y1_ref[...].astype(F32) * gt[:, 1:2]
    out = _layer_norm(ALPHA * x_ref[...] + y, g_ref[...], b_ref[...])
    o_ref[...] = out
    ob_ref[...] = out.astype(BF16)


def _combine_ln(x, y0, y1, gates_t, gain, bias):
    n, d = x.shape
    tm = LN_TM
    row = lambda i: (i, 0)
    const = lambda i: (0, 0)
    return pl.pallas_call(
        _combine_ln_kernel,
        out_shape=(jax.ShapeDtypeStruct((n, d), F32), jax.ShapeDtypeStruct((n, d), BF16)),
        grid=(n // tm,),
        in_specs=[
            pl.BlockSpec((tm, d), row),
            pl.BlockSpec((tm, d), row),
            pl.BlockSpec((tm, d), row),
            pl.BlockSpec((tm, 2), row),
            pl.BlockSpec((1, d), const),
            pl.BlockSpec((1, d), const),
        ],
        out_specs=(pl.BlockSpec((tm, d), row), pl.BlockSpec((tm, d), row)),
        compiler_params=_params(("arbitrary",)),
        name="moe_combine_ln",
    )(x, y0, y1, gates_t, gain, bias)


def _fg_kernel(x_ref, w_ref, b_ref, tri_ref, qf_ref, kf_ref, c_ref, carry_ref, koff_ref):
    i = pl.program_id(1)

    @pl.when(i == 0)
    def _():
        carry_ref[...] = jnp.zeros_like(carry_ref)

    @pl.when(i % (FOX_TK // FOX_T) == 0)
    def _():
        koff_ref[...] = jnp.zeros_like(koff_ref)

    def split3(v):
        hi = v.astype(BF16)
        r1 = v - hi.astype(F32)
        mid = r1.astype(BF16)
        lo = (r1 - mid.astype(F32)).astype(BF16)
        return hi, mid, lo

    y = x_ref[...]
    yb = y.astype(BF16)
    ylo = (y - yb.astype(F32)).astype(BF16)
    w = w_ref[...]
    wh = w.astype(BF16)
    wl = (w - wh.astype(F32)).astype(BF16)
    z = (jnp.dot(yb, wh, preferred_element_type=F32) + jnp.dot(ylo, wh, preferred_element_type=F32)
         + jnp.dot(yb, wl, preferred_element_type=F32)) + b_ref[...]
    lf = (jnp.minimum(z, 0.0) - jnp.log1p(jnp.exp(-jnp.abs(z)))) * LOG2E
    parts = jnp.concatenate(split3(lf), axis=1)
    cs3 = jnp.dot(tri_ref[...], parts, preferred_element_type=F32)
    cs = cs3[:, 0:LANES] + cs3[:, LANES:2 * LANES] + cs3[:, 2 * LANES:3 * LANES]
    c_ref[...] = carry_ref[...]
    carry_ref[...] = carry_ref[...] + cs[FOX_T - 1:FOX_T, :]
    csk = cs + koff_ref[...]
    koff_ref[...] = koff_ref[...] + cs[FOX_T - 1:FOX_T, :]

    hi, mid, lo = (p.astype(F32) for p in split3(cs))
    khi, kmid, klo = (p.astype(F32) for p in split3(csk))
    lane = lax.broadcasted_iota(jnp.int32, (FOX_T, LANES), 1)
    q_ones = jnp.where((lane >= 3) & (lane < 6), 1.0, 0.0).astype(F32)
    k_ones = jnp.where(lane < 3, 1.0, 0.0).astype(F32)
    for h in range(B_HEADS):
        hc, mc, lc = hi[:, h:h + 1], mid[:, h:h + 1], lo[:, h:h + 1]
        tq = jnp.where(lane == 0, hc, jnp.where(lane == 1, mc, jnp.where(lane == 2, lc, q_ones)))
        hc, mc, lc = khi[:, h:h + 1], kmid[:, h:h + 1], klo[:, h:h + 1]
        tk = jnp.where(lane == 3, -hc, jnp.where(lane == 4, -mc, jnp.where(lane == 5, -lc, k_ones)))
        qf_ref[h] = tq.astype(BF16)
        kf_ref[h] = tk.astype(BF16)


def _forget_columns(x, fw, fb):
    b, s, d = x.shape
    t = FOX_T
    nblk = s // t
    tri = (jnp.arange(t)[:, None] >= jnp.arange(t)[None, :]).astype(BF16)
    const = lambda bb, i: (0, 0)
    cols = jax.ShapeDtypeStruct((b, B_HEADS, s, LANES), BF16)
    col_spec = pl.BlockSpec((None, B_HEADS, t, LANES), lambda bb, i: (bb, 0, i, 0))
    return pl.pallas_call(
        _fg_kernel,
        out_shape=(cols, cols, jax.ShapeDtypeStruct((b, nblk, 1, LANES), F32)),
        grid=(b, nblk),
        in_specs=[
            pl.BlockSpec((None, t, d), lambda bb, i: (bb, i, 0)),
            pl.BlockSpec((d, LANES), const),
            pl.BlockSpec((1, LANES), const),
            pl.BlockSpec((t, t), const),
        ],
        out_specs=(col_spec, col_spec,
                   pl.BlockSpec((None, None, 1, LANES), lambda bb, i: (bb, i, 0, 0))),
        scratch_shapes=[pltpu.VMEM((1, LANES), F32), pltpu.VMEM((1, LANES), F32)],
        compiler_params=_params(("arbitrary", "arbitrary")),
        name="forget_cumsum",
    )(x, fw, fb, tri)


def _fox_kernel(c_tab, q_ref, qf_ref, k_ref, kf_ref, v_ref, o_ref,
                qa_ref, sa_ref, sb_ref, m_ref, acc_ref):
    i = pl.program_id(2)
    tq, tk = FOX_T, FOX_TK
    per_tile = tk // tq
    cbase = (pl.program_id(0) * B_HEADS + pl.program_id(1)) * pl.num_programs(2)
    c_q = jnp.full((1, LANES), c_tab[cbase + i], F32)
    qa_ref[:, 0:B_HEAD_DIM] = q_ref[...]
    qa_ref[:, B_HEAD_DIM:2 * B_HEAD_DIM] = qf_ref[...]
    m_ref[...] = jnp.full(m_ref.shape, NEG, F32)
    acc_ref[...] = jnp.zeros_like(acc_ref)
    ones_col = jnp.where(lax.broadcasted_iota(jnp.int32, (tk, LANES), 1) == 0, 1.0, 0.0).astype(BF16)

    def scores(jt, dst_ref):
        ks = pl.multiple_of(jt * tk, tk)
        ka = jnp.concatenate([k_ref[pl.ds(ks, tk), :], kf_ref[pl.ds(ks, tk), :]], axis=1)
        dst_ref[...] = lax.dot_general(qa_ref[...], ka, (((1,), (1,)), ((), ())),
                                       preferred_element_type=F32)

    def absorb(src_ref, jt, last):
        ks = pl.multiple_of(jt * tk, tk)
        s = src_ref[...]
        if last:
            row = lax.broadcasted_iota(jnp.int32, (tq, tk), 0) + (i - per_tile * jt) * tq
            col = lax.broadcasted_iota(jnp.int32, (tq, tk), 1)
            s = jnp.where(col <= row, s, NEG)
        shift = c_q - jnp.full((1, LANES), c_tab[cbase + per_tile * jt], F32)
        m_prev = m_ref[...]
        m_new = jnp.maximum(m_prev, jnp.max(s, axis=-1, keepdims=True) + shift)
        p = jnp.exp2(s - jnp.concatenate([m_new - shift] * (tk // LANES), axis=1))
        scale = jnp.exp2(m_prev - m_new)
        va = jnp.concatenate([v_ref[pl.ds(ks, tk), :], ones_col], axis=1)
        acc_ref[...] = jnp.concatenate([scale, scale], axis=1) * acc_ref[...] + jnp.dot(
            p.astype(BF16), va, preferred_element_type=F32)
        m_ref[...] = m_new

    n_full = lax.shift_right_logical(i, 1)
    scores(0, sa_ref)

    def pair(first):
        scores(first + 1, sb_ref)
        absorb(sa_ref, first, False)
        scores(first + 2, sa_ref)
        absorb(sb_ref, first + 1, False)

    def quad(jj, carry):
        pair(4 * jj)
        pair(4 * jj + 2)
        return carry

    n_quads = lax.shift_right_logical(n_full, 2)
    lax.fori_loop(0, n_quads, quad, 0)

    @pl.when((n_full & 2) != 0)
    def _():
        pair(n_quads * 4)

    @pl.when((n_full & 1) == 1)
    def _():
        scores(n_full, sb_ref)
        absorb(sa_ref, n_full - 1, False)
        absorb(sb_ref, n_full, True)

    @pl.when((n_full & 1) == 0)
    def _():
        absorb(sa_ref, n_full, True)

    acc = acc_ref[...]
    o_ref[...] = (acc[:, 0:B_HEAD_DIM] / acc[:, B_HEAD_DIM:B_HEAD_DIM + 1]).astype(o_ref.dtype)


def _fox(q, kv, qf, kf, c_tab):
    b, s, d = q.shape
    t = FOX_T
    hd = B_HEAD_DIM
    grid_spec = pltpu.PrefetchScalarGridSpec(
        num_scalar_prefetch=1,
        grid=(b, B_HEADS, s // t),
        in_specs=[
            pl.BlockSpec((None, t, hd), lambda bb, h, i, c: (bb, i, h)),
            pl.BlockSpec((None, None, t, LANES), lambda bb, h, i, c: (bb, h, i, 0)),
            pl.BlockSpec((None, s, hd), lambda bb, h, i, c: (bb, 0, h)),
            pl.BlockSpec((None, None, s, LANES), lambda bb, h, i, c: (bb, h, 0, 0)),
            pl.BlockSpec((None, s, hd), lambda bb, h, i, c: (bb, 0, B_HEADS + h)),
        ],
        out_specs=pl.BlockSpec((None, t, hd), lambda bb, h, i, c: (bb, i, h)),
        scratch_shapes=[
            pltpu.VMEM((t, 2 * hd), BF16),
            pltpu.VMEM((t, FOX_TK), F32),
            pltpu.VMEM((t, FOX_TK), F32),
            pltpu.VMEM((t, LANES), F32),
            pltpu.VMEM((t, 2 * hd), F32),
        ],
    )
    return pl.pallas_call(
        _fox_kernel,
        out_shape=jax.ShapeDtypeStruct((b, s, d), BF16),
        grid_spec=grid_spec,
        compiler_params=_params(("arbitrary", "arbitrary", "arbitrary")),
        name="mixer_b",
    )(c_tab, q, qf, kv, kf, kv)


def _moe_layer(x1, x1b, experts, gates, w_gate, w_up, w_down, layer, gain, bias):
    meta, tok_sorted, pos = _moe_plan(experts)
    xs = x1b[tok_sorted]
    yb = _moe(xs, w_gate, w_up, w_down, layer, meta)
    return _combine_ln(x1, yb[pos[0]], yb[pos[1]], gates.T, gain, bias)


def kernel(x, a_w_qkv, a_rel_bias, a_w_o, kv_w, fg_w, fg_b, b_w_q, b_w_o, router_w, router_b,
           moe_w_gate, moe_w_up, moe_w_down, ln_gain, ln_bias):
    b, s, d = x.shape
    n = b * s
    assert MM_TM == PAD_ROWS and s % MM_TM == 0 and s % FOX_TK == 0 and FOX_TK == 2 * FOX_T
    assert s % (A_TQ * A_SUB) == 0 and PAD_ROWS % A_TQ == 0 and A_SUB % 2 == 0 and A_SUB >= 4
    assert (2 * n) % MOE_BLK == 0

    rw = _pad_small_weight(router_w)
    rb = router_b.astype(F32).reshape(N_EXPERTS, 1)
    gain = ln_gain.astype(F32).reshape(DEPTH, 2, 1, d)
    bias = ln_bias.astype(F32).reshape(DEPTH, 2, 1, d)
    wg, wu, wd = moe_w_gate, moe_w_up, moe_w_down

    qscale = jnp.concatenate([jnp.full((d,), LOG2E / math.sqrt(A_HEAD_DIM), F32),
                              jnp.ones((2 * d,), F32)])
    w_qkv = (a_w_qkv[0] * qscale[None, :]).astype(BF16)
    qkv = _mm(x.astype(BF16), w_qkv, pad_blocks=1)
    attn = _attn_a(qkv, _rel_bias_table(a_rel_bias[0]), s)
    x1, x1b, experts, gates = _post_attn(attn.reshape(n, d), a_w_o[0].astype(BF16),
                                         x.reshape(n, d), gain[0, 0], bias[0, 0], rw, rb)
    x2, x2b = _moe_layer(x1, x1b, experts, gates, wg, wu, wd, 0, gain[0, 1], bias[0, 1])

    kv = _mm(x2b.reshape(b, s, d), kv_w.astype(BF16))
    fb = jnp.zeros((1, LANES), F32).at[0, :B_HEADS].set(fg_b.astype(F32))
    qf, kf, carry_in = _forget_columns(x2.reshape(b, s, d), _pad_small_weight(fg_w), fb)
    c_tab = carry_in[:, :, 0, :B_HEADS].transpose(0, 2, 1).reshape(-1)

    w_q = (b_w_q[0] * (LOG2E / math.sqrt(B_HEAD_DIM))).astype(BF16)
    q = _mm(x2b.reshape(b, s, d), w_q)
    attn = _fox(q, kv, qf, kf, c_tab)
    x3, x3b, experts, gates = _post_attn(attn.reshape(n, d), b_w_o[0].astype(BF16), x2,
                                         gain[1, 0], bias[1, 0], rw, rb)
    x4, _ = _moe_layer(x3, x3b, experts, gates, wg, wu, wd, 1, gain[1, 1], bias[1, 1])
    return x4.reshape(b, s, d)
```

```python
import functools
import math

import jax
import jax.numpy as jnp
from jax import lax
from jax.experimental import pallas as pl
from jax.experimental.pallas import tpu as pltpu

F32 = jnp.float32
BF16 = jnp.bfloat16

CHUNK = 64
LEFT_CHUNKS = 8
A_HEADS = 32
A_HEAD_DIM = 64
REL_CLIP = 2 * CHUNK
B_HEADS = 16
B_HEAD_DIM = 128
N_EXPERTS = 16
N_GROUPS = 4
EXPERTS_PER_GROUP = 4
DEPTH = 2
ALPHA = (2.0 * DEPTH) ** 0.25
LN_EPS = 1e-5
NEG = -1e30
LOG2E = 1.4426950408889634

LANES = 128
VMEM_LIMIT = 56 * 1024 * 1024

MM_TM = 512
MM_TN = 2048
PAD_ROWS = LEFT_CHUNKS * CHUNK
A_TQ = 256
A_WIN = A_TQ + PAD_ROWS
A_SUB = 16
A_TABLES = PAD_ROWS // A_TQ + 1
POST_TM = 256
MOE_BLK = 1024
MOE_CHUNK = 256
MOE_TF = 256
LN_TM = 512
FOX_T = 512
FOX_TK = 2 * FOX_T


def _params(sem):
    return pltpu.CompilerParams(dimension_semantics=sem, vmem_limit_bytes=VMEM_LIMIT)


def _mm_kernel(x_ref, w_ref, o_ref, *, pad_blocks):
    def compute():
        o_ref[...] = jnp.dot(x_ref[...], w_ref[...],
                             preferred_element_type=F32).astype(o_ref.dtype)

    if pad_blocks:
        i = pl.program_id(2)

        @pl.when(i < pad_blocks)
        def _():
            o_ref[...] = jnp.zeros_like(o_ref)

        pl.when(i >= pad_blocks)(compute)
    else:
        compute()


def _mm(x, w, *, pad_blocks=0):
    b, s, k = x.shape
    m = w.shape[1]
    tn = min(MM_TN, m)
    grid = (m // tn, b, s // MM_TM + pad_blocks)
    return pl.pallas_call(
        functools.partial(_mm_kernel, pad_blocks=pad_blocks),
        out_shape=jax.ShapeDtypeStruct((b, s + pad_blocks * MM_TM, m), BF16),
        grid=grid,
        in_specs=[
            pl.BlockSpec((None, MM_TM, k),
                         lambda j, bb, i: (bb, jnp.maximum(i - pad_blocks, 0), 0)),
            pl.BlockSpec((k, tn), lambda j, bb, i: (0, j)),
        ],
        out_specs=pl.BlockSpec((None, MM_TM, tn), lambda j, bb, i: (bb, i, j)),
        compiler_params=_params(("arbitrary", "arbitrary", "arbitrary")),
        name="dense_proj",
    )(x, w)


def _attn_a_kernel(q_ref, k_ref, v_ref, bias_ref, o_ref, sa_ref, sb_ref):
    i = pl.program_id(2)
    head0 = lax.broadcasted_iota(jnp.int32, (A_TQ, LANES), 1) < A_HEAD_DIM
    ones_col = jnp.where(lax.broadcasted_iota(jnp.int32, (A_WIN, LANES), 1) == 0, 1.0, 0.0).astype(BF16)

    def window(sb):
        g = i * A_SUB + sb
        return g, pl.multiple_of(g * A_TQ, A_TQ)

    def scores(sb, dst_ref):
        _, start = window(sb)
        q = q_ref[pl.ds(start + PAD_ROWS, A_TQ), :]
        zero = jnp.zeros_like(q)
        q2 = jnp.concatenate([jnp.where(head0, q, zero), jnp.where(head0, zero, q)], axis=0)
        dst_ref[...] = lax.dot_general(q2, k_ref[pl.ds(start, A_WIN), :], (((1,), (1,)), ((), ())),
                                       preferred_element_type=F32)

    def absorb(src_ref, sb):
        g, start = window(sb)
        s = src_ref[...] + bias_ref[jnp.minimum(g, A_TABLES - 1)]
        p = jnp.exp2(s - jnp.max(s, axis=-1, keepdims=True))
        va = jnp.concatenate([v_ref[pl.ds(start, A_WIN), :], ones_col], axis=1)
        o = jnp.dot(p.astype(BF16), va, preferred_element_type=F32)
        o = o[:, 0:LANES] / o[:, LANES:LANES + 1]
        rows = pl.ds(pl.multiple_of(sb * A_TQ, A_TQ), A_TQ)
        o_ref[rows, :] = jnp.where(head0, o[0:A_TQ], o[A_TQ:2 * A_TQ]).astype(o_ref.dtype)

    scores(0, sa_ref)

    def pair(jj, carry):
        scores(2 * jj + 1, sb_ref)
        absorb(sa_ref, 2 * jj)
        scores(2 * jj + 2, sa_ref)
        absorb(sb_ref, 2 * jj + 1)
        return carry

    lax.fori_loop(0, A_SUB // 2 - 1, pair, 0)
    scores(A_SUB - 1, sb_ref)
    absorb(sa_ref, A_SUB - 2)
    absorb(sb_ref, A_SUB - 1)


def _attn_a(qkv_pad, bias_tab, s):
    b = qkv_pad.shape[0]
    d = A_HEADS * A_HEAD_DIM
    pairs = d // LANES
    sp = s + PAD_ROWS
    rows = A_TQ * A_SUB
    return pl.pallas_call(
        _attn_a_kernel,
        out_shape=jax.ShapeDtypeStruct((b, s, d), BF16),
        grid=(pairs, b, s // rows),
        in_specs=[
            pl.BlockSpec((None, sp, LANES), lambda j, bb, i: (bb, 0, j)),
            pl.BlockSpec((None, sp, LANES), lambda j, bb, i: (bb, 0, pairs + j)),
            pl.BlockSpec((None, sp, LANES), lambda j, bb, i: (bb, 0, 2 * pairs + j)),
            pl.BlockSpec((None, A_TABLES, 2 * A_TQ, A_WIN), lambda j, bb, i: (j, 0, 0, 0)),
        ],
        out_specs=pl.BlockSpec((None, rows, LANES), lambda j, bb, i: (bb, i, j)),
        scratch_shapes=[pltpu.VMEM((2 * A_TQ, A_WIN), F32), pltpu.VMEM((2 * A_TQ, A_WIN), F32)],
        compiler_params=_params(("arbitrary", "arbitrary", "arbitrary")),
        name="mixer_a",
    )(qkv_pad, qkv_pad, qkv_pad, bias_tab)


def _rel_bias_table(rel_bias):
    qi = jnp.arange(A_TQ)[:, None]
    kj = jnp.arange(A_WIN)[None, :]
    dchunk = kj // CHUNK - qi // CHUNK
    inband = (dchunk >= 0) & (dchunk <= LEFT_CHUNKS)
    period = A_TQ + A_WIN
    u = jnp.arange(period)
    u = jnp.where(u < A_WIN, u, u - period)
    idx = jnp.clip(PAD_ROWS - u, -REL_CLIP, REL_CLIP) + REL_CLIP
    g = rel_bias.astype(F32)[:, idx] * LOG2E
    flat = jnp.tile(g, (1, A_TQ))[:, :A_TQ * (period - 1)]
    tab = flat.reshape(-1, A_TQ, period - 1)[:, :, :A_WIN]
    first_real = PAD_ROWS - A_TQ * jnp.arange(A_TABLES)[:, None, None]
    keep = inband[None] & (kj[None] >= first_real)
    tabs = jnp.where(keep[None], tab[:, None], NEG)
    tabs = tabs.reshape(-1, 2, A_TABLES, A_TQ, A_WIN).transpose(0, 2, 1, 3, 4)
    return tabs.reshape(-1, A_TABLES, 2 * A_TQ, A_WIN)


def _layer_norm(z, gain, bias):
    mu = jnp.mean(z, axis=-1, keepdims=True)
    zc = z - mu
    var = jnp.mean(zc * zc, axis=-1, keepdims=True)
    return zc * lax.rsqrt(var + LN_EPS) * gain + bias


def _small_proj_t(y, yb, w_ref, n_out):
    ylo = (y - yb.astype(F32)).astype(BF16)
    w = w_ref[...]
    wh = w.astype(BF16)
    wl = (w - wh.astype(F32)).astype(BF16)
    acc = jnp.dot(yb, wh, preferred_element_type=F32)
    acc = acc + jnp.dot(ylo, wh, preferred_element_type=F32)
    acc = acc + jnp.dot(yb, wl, preferred_element_type=F32)
    return acc.T[:n_out, :]


def _pad_small_weight(w):
    k, n = w.shape
    return jnp.zeros((k, LANES), F32).at[:, :n].set(w.astype(F32))


def _route(lt):
    m = jnp.max(lt, axis=0, keepdims=True)
    e = jnp.exp(lt - m)
    p = e / jnp.sum(e, axis=0, keepdims=True)
    rows = [p[k:k + 1, :] for k in range(N_EXPERTS)]
    best = None
    gi = None
    for g in range(N_GROUPS):
        r = rows[EXPERTS_PER_GROUP * g:EXPERTS_PER_GROUP * (g + 1)]
        sc = None
        for a in range(EXPERTS_PER_GROUP):
            for c in range(a + 1, EXPERTS_PER_GROUP):
                pair = r[a] + r[c]
                sc = pair if sc is None else jnp.maximum(sc, pair)
        if g == 0:
            best = sc
            gi = jnp.zeros(sc.shape, jnp.int32)
        else:
            upd = sc > best
            best = jnp.where(upd, sc, best)
            gi = jnp.where(upd, g, gi)
    pin = []
    for k in range(EXPERTS_PER_GROUP):
        v = rows[(N_GROUPS - 1) * EXPERTS_PER_GROUP + k]
        for g in range(N_GROUPS - 2, -1, -1):
            v = jnp.where(gi == g, rows[g * EXPERTS_PER_GROUP + k], v)
        pin.append(v)
    v1 = pin[0]
    i1 = jnp.zeros(v1.shape, jnp.int32)
    for k in range(1, EXPERTS_PER_GROUP):
        upd = pin[k] > v1
        v1 = jnp.where(upd, pin[k], v1)
        i1 = jnp.where(upd, k, i1)
    v2 = jnp.full(v1.shape, -1.0, F32)
    i2 = jnp.zeros(v1.shape, jnp.int32)
    for k in range(EXPERTS_PER_GROUP):
        cand = jnp.where(i1 == k, -1.0, pin[k])
        upd = cand > v2
        v2 = jnp.where(upd, cand, v2)
        i2 = jnp.where(upd, k, i2)
    tot = v1 + v2
    base = gi * EXPERTS_PER_GROUP
    experts = jnp.concatenate([base + i1, base + i2], axis=0)
    gates = jnp.concatenate([v1 / tot, v2 / tot], axis=0)
    return experts, gates


def _post_attn_kernel(a0_ref, alo_ref, ahi_ref, w_ref, x_ref, g_ref, b_ref, rw_ref, rb_ref,
                      x1_ref, x1b_ref, ex_ref, gt_ref, h0_ref, h1_ref):
    t = POST_TM

    def project(a_ref, h_ref):
        h_ref[...] = jnp.dot(a_ref[...], w_ref[...], preferred_element_type=F32)

    def finish(h_ref, half):
        rows = slice(half * t, (half + 1) * t)
        y = _layer_norm(ALPHA * x_ref[rows, :] + h_ref[...], g_ref[...], b_ref[...])
        yb = y.astype(BF16)
        x1_ref[rows, :] = y
        x1b_ref[rows, :] = yb
        lt = _small_proj_t(y, yb, rw_ref, N_EXPERTS) + rb_ref[...]
        experts, gates = _route(lt)
        ex_ref[:, rows] = experts
        gt_ref[:, rows] = gates

    @pl.when(pl.program_id(0) == 0)
    def _():
        project(a0_ref, h0_ref)

    project(alo_ref, h1_ref)
    finish(h0_ref, 0)
    project(ahi_ref, h0_ref)
    finish(h1_ref, 1)


def _post_attn(attn, w_o, x, gain, bias, rw, rb):
    n, d = x.shape
    t = POST_TM
    last_tile = n // t - 1
    row = lambda i: (i, 0)
    const = lambda i: (0, 0)
    return pl.pallas_call(
        _post_attn_kernel,
        out_shape=(jax.ShapeDtypeStruct((n, d), F32),
                   jax.ShapeDtypeStruct((n, d), BF16),
                   jax.ShapeDtypeStruct((2, n), jnp.int32),
                   jax.ShapeDtypeStruct((2, n), F32)),
        grid=(n // (2 * t),),
        in_specs=[
            pl.BlockSpec((t, d), const),
            pl.BlockSpec((t, d), lambda i: (2 * i + 1, 0)),
            pl.BlockSpec((t, d), lambda i: (jnp.minimum(2 * i + 2, last_tile), 0)),
            pl.BlockSpec((d, d), const),
            pl.BlockSpec((2 * t, d), row),
            pl.BlockSpec((1, d), const),
            pl.BlockSpec((1, d), const),
            pl.BlockSpec((d, LANES), const),
            pl.BlockSpec((N_EXPERTS, 1), const),
        ],
        out_specs=(pl.BlockSpec((2 * t, d), row),
                   pl.BlockSpec((2 * t, d), row),
                   pl.BlockSpec((2, 2 * t), lambda i: (0, i)),
                   pl.BlockSpec((2, 2 * t), lambda i: (0, i))),
        scratch_shapes=[pltpu.VMEM((t, d), F32), pltpu.VMEM((t, d), F32)],
        compiler_params=_params(("arbitrary",)),
        name="out_proj_ln_router",
    )(attn, attn, attn, w_o, x, gain, bias, rw, rb)


def _moe_kernel(iblk, iex, ivalid, ifirst, starts, ends, x_ref, wg_ref, wu_ref, wd_ref, o_ref,
                acc_ref):
    w = pl.program_id(0)
    f = pl.program_id(1)

    @pl.when((ifirst[w] == 1) & (f == 0))
    def _():
        acc_ref[...] = jnp.zeros_like(acc_ref)

    @pl.when(ivalid[w] == 1)
    def _():
        wg = wg_ref[...].astype(BF16)
        wu = wu_ref[...].astype(BF16)
        wd = wd_ref[...].astype(BF16)
        e = iex[w]
        lo = starts[e]
        hi = ends[e]
        base = iblk[w] * MOE_BLK

        def swiglu(x, keep):
            hg = jnp.dot(x, wg, preferred_element_type=F32)
            hu = jnp.dot(x, wu, preferred_element_type=F32)
            hid = hg * jax.nn.sigmoid(hg) * hu
            if keep is not None:
                hid = jnp.where(keep, hid, 0.0)
            return jnp.dot(hid.astype(BF16), wd, preferred_element_type=F32)

        whole = (lo <= base) & (hi >= base + MOE_BLK)

        @pl.when(whole)
        def _():
            acc_ref[...] += swiglu(x_ref[...], None)

        @pl.when(jnp.logical_not(whole))
        def _():
            for c in range(MOE_BLK // MOE_CHUNK):
                r0 = base + c * MOE_CHUNK

                @pl.when((lo < r0 + MOE_CHUNK) & (hi > r0))
                def _():
                    rows = slice(c * MOE_CHUNK, (c + 1) * MOE_CHUNK)
                    row = r0 + lax.broadcasted_iota(jnp.int32, (MOE_CHUNK, 1), 0)
                    acc_ref[rows, :] += swiglu(x_ref[rows, :], (row >= lo) & (row < hi))

        @pl.when(f == pl.num_programs(1) - 1)
        def _():
            o_ref[...] = acc_ref[...].astype(o_ref.dtype)


def _moe(xs, w_gate, w_up, w_down, layer, meta):
    a, d = xs.shape
    dff = w_gate.shape[-1]
    nf = dff // MOE_TF
    n_items = meta[0].shape[0]

    def fsel(w, f, ivalid):
        return jnp.where(ivalid[w] == 1, f, nf - 1)

    grid_spec = pltpu.PrefetchScalarGridSpec(
        num_scalar_prefetch=6,
        grid=(n_items, nf),
        in_specs=[
            pl.BlockSpec((MOE_BLK, d), lambda w, f, ib, ie, iv, i1, st, en: (ib[w], 0)),
            pl.BlockSpec((None, None, d, MOE_TF),
                         lambda w, f, ib, ie, iv, i1, st, en: (layer, ie[w], 0, fsel(w, f, iv))),
            pl.BlockSpec((None, None, d, MOE_TF),
                         lambda w, f, ib, ie, iv, i1, st, en: (layer, ie[w], 0, fsel(w, f, iv))),
            pl.BlockSpec((None, None, MOE_TF, d),
                         lambda w, f, ib, ie, iv, i1, st, en: (layer, ie[w], fsel(w, f, iv), 0)),
        ],
        out_specs=pl.BlockSpec((MOE_BLK, d), lambda w, f, ib, ie, iv, i1, st, en: (ib[w], 0)),
        scratch_shapes=[pltpu.VMEM((MOE_BLK, d), F32)],
    )
    return pl.pallas_call(
        _moe_kernel,
        out_shape=jax.ShapeDtypeStruct((a, d), BF16),
        grid_spec=grid_spec,
        compiler_params=_params(("arbitrary", "arbitrary")),
        name="moe_experts",
    )(*meta, xs, w_gate, w_up, w_down)


def _moe_plan(experts):
    n = experts.shape[1]
    a = 2 * n
    nblk = a // MOE_BLK
    n_items = nblk + N_EXPERTS - 1
    e_flat = experts.reshape(-1)
    iota = jnp.arange(a, dtype=jnp.int32)
    e_sorted, order = lax.sort((e_flat, iota), num_keys=1, is_stable=True)
    tok_sorted = order % n
    _, pos = lax.sort((order, iota), num_keys=1)
    pos = pos.reshape(2, n)
    eids = jnp.arange(N_EXPERTS, dtype=jnp.int32)
    starts = jnp.searchsorted(e_sorted, eids, side="left").astype(jnp.int32)
    ends = jnp.searchsorted(e_sorted, eids, side="right").astype(jnp.int32)
    e_lo = e_sorted[0::MOE_BLK]
    e_hi = e_sorted[MOE_BLK - 1::MOE_BLK]
    per_blk = e_hi - e_lo + 1
    iend = jnp.cumsum(per_blk).astype(jnp.int32)
    istart = iend - per_blk
    total = iend[-1]
    w = jnp.arange(n_items, dtype=jnp.int32)
    blk = jnp.clip(jnp.searchsorted(iend, w, side="right"), 0, nblk - 1).astype(jnp.int32)
    ex = e_lo[blk] + (w - istart[blk])
    valid = w < total
    blk = jnp.where(valid, blk, nblk - 1)
    ex = jnp.where(valid, ex, e_hi[nblk - 1])
    first = valid & (w == istart[blk])
    meta = (blk, ex.astype(jnp.int32), valid.astype(jnp.int32), first.astype(jnp.int32),
            starts, ends)
    return meta, tok_sorted, pos


def _combine_ln_kernel(x_ref, y0_ref, y1_ref, gt_ref, g_ref, b_ref, o_ref, ob_ref):
    gt = gt_ref[...]
    y = y0_ref[...].astype(F32) * gt[:, 0:1] + y1_ref[...].astype(F32) * gt[:, 1:2]
    out = _layer_norm(ALPHA * x_ref[...] + y, g_ref[...], b_ref[...])
    o_ref[...] = out
    ob_ref[...] = out.astype(BF16)


def _combine_ln(x, y0, y1, gates_t, gain, bias):
    n, d = x.shape
    tm = LN_TM
    row = lambda i: (i, 0)
    const = lambda i: (0, 0)
    return pl.pallas_call(
        _combine_ln_kernel,
        out_shape=(jax.ShapeDtypeStruct((n, d), F32), jax.ShapeDtypeStruct((n, d), BF16)),
        grid=(n // tm,),
        in_specs=[
            pl.BlockSpec((tm, d), row),
            pl.BlockSpec((tm, d), row),
            pl.BlockSpec((tm, d), row),
            pl.BlockSpec((tm, 2), row),
            pl.BlockSpec((1, d), const),
            pl.BlockSpec((1, d), const),
        ],
        out_specs=(pl.BlockSpec((tm, d), row), pl.BlockSpec((tm, d), row)),
        compiler_params=_params(("arbitrary",)),
        name="moe_combine_ln",
    )(x, y0, y1, gates_t, gain, bias)


def _fg_kernel(x_ref, w_ref, b_ref, tri_ref, qf_ref, kf_ref, c_ref, carry_ref, koff_ref):
    i = pl.program_id(1)

    @pl.when(i == 0)
    def _():
        carry_ref[...] = jnp.zeros_like(carry_ref)

    @pl.when(i % (FOX_TK // FOX_T) == 0)
    def _():
        koff_ref[...] = jnp.zeros_like(koff_ref)

    def split3(v):
        hi = v.astype(BF16)
        r1 = v - hi.astype(F32)
        mid = r1.astype(BF16)
        lo = (r1 - mid.astype(F32)).astype(BF16)
        return hi, mid, lo

    y = x_ref[...]
    yb = y.astype(BF16)
    ylo = (y - yb.astype(F32)).astype(BF16)
    w = w_ref[...]
    wh = w.astype(BF16)
    wl = (w - wh.astype(F32)).astype(BF16)
    z = (jnp.dot(yb, wh, preferred_element_type=F32) + jnp.dot(ylo, wh, preferred_element_type=F32)
         + jnp.dot(yb, wl, preferred_element_type=F32)) + b_ref[...]
    lf = (jnp.minimum(z, 0.0) - jnp.log1p(jnp.exp(-jnp.abs(z)))) * LOG2E
    parts = jnp.concatenate(split3(lf), axis=1)
    cs3 = jnp.dot(tri_ref[...], parts, preferred_element_type=F32)
    cs = cs3[:, 0:LANES] + cs3[:, LANES:2 * LANES] + cs3[:, 2 * LANES:3 * LANES]
    c_ref[...] = carry_ref[...]
    carry_ref[...] = carry_ref[...] + cs[FOX_T - 1:FOX_T, :]
    csk = cs + koff_ref[...]
    koff_ref[...] = koff_ref[...] + cs[FOX_T - 1:FOX_T, :]

    hi, mid, lo = (p.astype(F32) for p in split3(cs))
    khi, kmid, klo = (p.astype(F32) for p in split3(csk))
    lane = lax.broadcasted_iota(jnp.int32, (FOX_T, LANES), 1)
    q_ones = jnp.where((lane >= 3) & (lane < 6), 1.0, 0.0).astype(F32)
    k_ones = jnp.where(lane < 3, 1.0, 0.0).astype(F32)
    for h in range(B_HEADS):
        hc, mc, lc = hi[:, h:h + 1], mid[:, h:h + 1], lo[:, h:h + 1]
        tq = jnp.where(lane == 0, hc, jnp.where(lane == 1, mc, jnp.where(lane == 2, lc, q_ones)))
        hc, mc, lc = khi[:, h:h + 1], kmid[:, h:h + 1], klo[:, h:h + 1]
        tk = jnp.where(lane == 3, -hc, jnp.where(lane == 4, -mc, jnp.where(lane == 5, -lc, k_ones)))
        qf_ref[h] = tq.astype(BF16)
        kf_ref[h] = tk.astype(BF16)


def _forget_columns(x, fw, fb):
    b, s, d = x.shape
    t = FOX_T
    nblk = s // t
    tri = (jnp.arange(t)[:, None] >= jnp.arange(t)[None, :]).astype(BF16)
    const = lambda bb, i: (0, 0)
    cols = jax.ShapeDtypeStruct((b, B_HEADS, s, LANES), BF16)
    col_spec = pl.BlockSpec((None, B_HEADS, t, LANES), lambda bb, i: (bb, 0, i, 0))
    return pl.pallas_call(
        _fg_kernel,
        out_shape=(cols, cols, jax.ShapeDtypeStruct((b, nblk, 1, LANES), F32)),
        grid=(b, nblk),
        in_specs=[
            pl.BlockSpec((None, t, d), lambda bb, i: (bb, i, 0)),
            pl.BlockSpec((d, LANES), const),
            pl.BlockSpec((1, LANES), const),
            pl.BlockSpec((t, t), const),
        ],
        out_specs=(col_spec, col_spec,
                   pl.BlockSpec((None, None, 1, LANES), lambda bb, i: (bb, i, 0, 0))),
        scratch_shapes=[pltpu.VMEM((1, LANES), F32), pltpu.VMEM((1, LANES), F32)],
        compiler_params=_params(("arbitrary", "arbitrary")),
        name="forget_cumsum",
    )(x, fw, fb, tri)


def _fox_kernel(c_tab, q_ref, qf_ref, k_ref, kf_ref, v_ref, o_ref,
                qa_ref, sa_ref, sb_ref, m_ref, acc_ref):
    i = pl.program_id(2)
    tq, tk = FOX_T, FOX_TK
    per_tile = tk // tq
    cbase = (pl.program_id(0) * B_HEADS + pl.program_id(1)) * pl.num_programs(2)
    c_q = jnp.full((1, LANES), c_tab[cbase + i], F32)
    qa_ref[:, 0:B_HEAD_DIM] = q_ref[...]
    qa_ref[:, B_HEAD_DIM:2 * B_HEAD_DIM] = qf_ref[...]
    m_ref[...] = jnp.full(m_ref.shape, NEG, F32)
    acc_ref[...] = jnp.zeros_like(acc_ref)
    ones_col = jnp.where(lax.broadcasted_iota(jnp.int32, (tk, LANES), 1) == 0, 1.0, 0.0).astype(BF16)

    def scores(jt, dst_ref):
        ks = pl.multiple_of(jt * tk, tk)
        ka = jnp.concatenate([k_ref[pl.ds(ks, tk), :], kf_ref[pl.ds(ks, tk), :]], axis=1)
        dst_ref[...] = lax.dot_general(qa_ref[...], ka, (((1,), (1,)), ((), ())),
                                       preferred_element_type=F32)

    def absorb(src_ref, jt, last):
        ks = pl.multiple_of(jt * tk, tk)
        s = src_ref[...]
        if last:
            row = lax.broadcasted_iota(jnp.int32, (tq, tk), 0) + (i - per_tile * jt) * tq
            col = lax.broadcasted_iota(jnp.int32, (tq, tk), 1)
            s = jnp.where(col <= row, s, NEG)
        shift = c_q - jnp.full((1, LANES), c_tab[cbase + per_tile * jt], F32)
        m_prev = m_ref[...]
        m_new = jnp.maximum(m_prev, jnp.max(s, axis=-1, keepdims=True) + shift)
        p = jnp.exp2(s - jnp.concatenate([m_new - shift] * (tk // LANES), axis=1))
        scale = jnp.exp2(m_prev - m_new)
        va = jnp.concatenate([v_ref[pl.ds(ks, tk), :], ones_col], axis=1)
        acc_ref[...] = jnp.concatenate([scale, scale], axis=1) * acc_ref[...] + jnp.dot(
            p.astype(BF16), va, preferred_element_type=F32)
        m_ref[...] = m_new

    n_full = lax.shift_right_logical(i, 1)
    scores(0, sa_ref)

    def pair(first):
        scores(first + 1, sb_ref)
        absorb(sa_ref, first, False)
        scores(first + 2, sa_ref)
        absorb(sb_ref, first + 1, False)

    def quad(jj, carry):
        pair(4 * jj)
        pair(4 * jj + 2)
        return carry

    n_quads = lax.shift_right_logical(n_full, 2)
    lax.fori_loop(0, n_quads, quad, 0)

    @pl.when((n_full & 2) != 0)
    def _():
        pair(n_quads * 4)

    @pl.when((n_full & 1) == 1)
    def _():
        scores(n_full, sb_ref)
        absorb(sa_ref, n_full - 1, False)
        absorb(sb_ref, n_full, True)

    @pl.when((n_full & 1) == 0)
    def _():
        absorb(sa_ref, n_full, True)

    acc = acc_ref[...]
    o_ref[...] = (acc[:, 0:B_HEAD_DIM] / acc[:, B_HEAD_DIM:B_HEAD_DIM + 1]).astype(o_ref.dtype)


def _fox(q, kv, qf, kf, c_tab):
    b, s, d = q.shape
    t = FOX_T
    hd = B_HEAD_DIM
    grid_spec = pltpu.PrefetchScalarGridSpec(
        num_scalar_prefetch=1,
        grid=(b, B_HEADS, s // t),
        in_specs=[
            pl.BlockSpec((None, t, hd), lambda bb, h, i, c: (bb, i, h)),
            pl.BlockSpec((None, None, t, LANES), lambda bb, h, i, c: (bb, h, i, 0)),
            pl.BlockSpec((None, s, hd), lambda bb, h, i, c: (bb, 0, h)),
            pl.BlockSpec((None, None, s, LANES), lambda bb, h, i, c: (bb, h, 0, 0)),
            pl.BlockSpec((None, s, hd), lambda bb, h, i, c: (bb, 0, B_HEADS + h)),
        ],
        out_specs=pl.BlockSpec((None, t, hd), lambda bb, h, i, c: (bb, i, h)),
        scratch_shapes=[
            pltpu.VMEM((t, 2 * hd), BF16),
            pltpu.VMEM((t, FOX_TK), F32),
            pltpu.VMEM((t, FOX_TK), F32),
            pltpu.VMEM((t, LANES), F32),
            pltpu.VMEM((t, 2 * hd), F32),
        ],
    )
    return pl.pallas_call(
        _fox_kernel,
        out_shape=jax.ShapeDtypeStruct((b, s, d), BF16),
        grid_spec=grid_spec,
        compiler_params=_params(("arbitrary", "arbitrary", "arbitrary")),
        name="mixer_b",
    )(c_tab, q, qf, kv, kf, kv)


def _moe_layer(x1, x1b, experts, gates, w_gate, w_up, w_down, layer, gain, bias):
    meta, tok_sorted, pos = _moe_plan(experts)
    xs = x1b[tok_sorted]
    yb = _moe(xs, w_gate, w_up, w_down, layer, meta)
    return _combine_ln(x1, yb[pos[0]], yb[pos[1]], gates.T, gain, bias)


def kernel(x, a_w_qkv, a_rel_bias, a_w_o, kv_w, fg_w, fg_b, b_w_q, b_w_o, router_w, router_b,
           moe_w_gate, moe_w_up, moe_w_down, ln_gain, ln_bias):
    b, s, d = x.shape
    n = b * s
    assert MM_TM == PAD_ROWS and s % MM_TM == 0 and s % FOX_TK == 0 and FOX_TK == 2 * FOX_T
    assert s % (A_TQ * A_SUB) == 0 and PAD_ROWS % A_TQ == 0 and A_SUB % 2 == 0 and A_SUB >= 4
    assert (2 * n) % MOE_BLK == 0

    rw = _pad_small_weight(router_w)
    rb = router_b.astype(F32).reshape(N_EXPERTS, 1)
    gain = ln_gain.astype(F32).reshape(DEPTH, 2, 1, d)
    bias = ln_bias.astype(F32).reshape(DEPTH, 2, 1, d)
    wg, wu, wd = moe_w_gate, moe_w_up, moe_w_down

    qscale = jnp.concatenate([jnp.full((d,), LOG2E / math.sqrt(A_HEAD_DIM), F32),
                              jnp.ones((2 * d,), F32)])
    w_qkv = (a_w_qkv[0] * qscale[None, :]).astype(BF16)
    qkv = _mm(x.astype(BF16), w_qkv, pad_blocks=1)
    attn = _attn_a(qkv, _rel_bias_table(a_rel_bias[0]), s)
    x1, x1b, experts, gates = _post_attn(attn.reshape(n, d), a_w_o[0].astype(BF16),
                                         x.reshape(n, d), gain[0, 0], bias[0, 0], rw, rb)
    x2, x2b = _moe_layer(x1, x1b, experts, gates, wg, wu, wd, 0, gain[0, 1], bias[0, 1])

    kv = _mm(x2b.reshape(b, s, d), kv_w.astype(BF16))
    fb = jnp.zeros((1, LANES), F32).at[0, :B_HEADS].set(fg_b.astype(F32))
    qf, kf, carry_in = _forget_columns(x2.reshape(b, s, d), _pad_small_weight(fg_w), fb)
    c_tab = carry_in[:, :, 0, :B_HEADS].transpose(0, 2, 1).reshape(-1)

    w_q = (b_w_q[0] * (LOG2E / math.sqrt(B_HEAD_DIM))).astype(BF16)
    q = _mm(x2b.reshape(b, s, d), w_q)
    attn = _fox(q, kv, qf, kf, c_tab)
    x3, x3b, experts, gates = _post_attn(attn.reshape(n, d), b_w_o[0].astype(BF16), x2,
                                         gain[1, 0], bias[1, 0], rw, rb)
    x4, _ = _moe_layer(x3, x3b, experts, gates, wg, wu, wd, 1, gain[1, 1], bias[1, 1])
    return x4.reshape(b, s, d)
```

```python
import functools
import math

import jax
import jax.numpy as jnp
from jax import lax
from jax.experimental import pallas as pl
from jax.experimental.pallas import tpu as pltpu

F32 = jnp.float32
BF16 = jnp.bfloat16

CHUNK = 64
LEFT_CHUNKS = 8
A_HEADS = 32
A_HEAD_DIM = 64
REL_CLIP = 2 * CHUNK
B_HEADS = 16
B_HEAD_DIM = 128
N_EXPERTS = 16
N_GROUPS = 4
EXPERTS_PER_GROUP = 4
DEPTH = 2
ALPHA = (2.0 * DEPTH) ** 0.25
LN_EPS = 1e-5
NEG = -1e30
LOG2E = 1.4426950408889634

LANES = 128
VMEM_LIMIT = 56 * 1024 * 1024

MM_TM = 512
MM_TN = 2048
PAD_ROWS = LEFT_CHUNKS * CHUNK
A_TQ = 256
A_WIN = A_TQ + PAD_ROWS
A_SUB = 32
A_TABLES = PAD_ROWS // A_TQ + 1
POST_TM = 256
MOE_BLK = 1024
MOE_CHUNK = 256
MOE_TF = 256
LN_TM = 512
FOX_T = 512
FOX_TK = 1024
FOX_KLANE = 64


def _params(sem):
    return pltpu.CompilerParams(dimension_semantics=sem, vmem_limit_bytes=VMEM_LIMIT)


def _mm_kernel(x_ref, w_ref, o_ref, *, pad_blocks):
    def compute():
        o_ref[...] = jnp.dot(x_ref[...].astype(BF16), w_ref[...],
                             preferred_element_type=F32).astype(o_ref.dtype)

    if pad_blocks:
        i = pl.program_id(1)

        @pl.when(i < pad_blocks)
        def _():
            o_ref[...] = jnp.zeros_like(o_ref)

        pl.when(i >= pad_blocks)(compute)
    else:
        compute()


def _mm(x, w, *, pad_blocks=0):
    b, s, k = x.shape
    m = w.shape[1]
    tn = min(MM_TN, m)
    grid = (b, s // MM_TM + pad_blocks, m // tn)
    return pl.pallas_call(
        functools.partial(_mm_kernel, pad_blocks=pad_blocks),
        out_shape=jax.ShapeDtypeStruct((b, s + pad_blocks * MM_TM, m), BF16),
        grid=grid,
        in_specs=[
            pl.BlockSpec((None, MM_TM, k),
                         lambda bb, i, j: (bb, jnp.maximum(i - pad_blocks, 0), 0)),
            pl.BlockSpec((k, tn), lambda bb, i, j: (0, j)),
        ],
        out_specs=pl.BlockSpec((None, MM_TM, tn), lambda bb, i, j: (bb, i, j)),
        compiler_params=_params(("arbitrary", "arbitrary", "arbitrary")),
        name="dense_proj",
    )(x, w)


def _attn_a_kernel(q_ref, k_ref, v_ref, bias_ref, o_ref, sa_ref, sb_ref):
    i = pl.program_id(2)
    head0 = lax.broadcasted_iota(jnp.int32, (A_TQ, LANES), 1) < A_HEAD_DIM
    ones_col = jnp.where(lax.broadcasted_iota(jnp.int32, (A_WIN, LANES), 1) == 0, 1.0, 0.0).astype(BF16)

    def window(sb):
        g = i * A_SUB + sb
        return g, pl.multiple_of(g * A_TQ, A_TQ)

    def scores(sb, dst_ref):
        _, start = window(sb)
        q = q_ref[pl.ds(start + PAD_ROWS, A_TQ), :]
        zero = jnp.zeros_like(q)
        q2 = jnp.concatenate([jnp.where(head0, q, zero), jnp.where(head0, zero, q)], axis=0)
        dst_ref[...] = lax.dot_general(q2, k_ref[pl.ds(start, A_WIN), :], (((1,), (1,)), ((), ())),
                                       preferred_element_type=F32)

    def absorb(src_ref, sb):
        g, start = window(sb)
        s = src_ref[...] + bias_ref[jnp.minimum(g, A_TABLES - 1)]
        p = jnp.exp2(s - jnp.max(s, axis=-1, keepdims=True))
        va = jnp.concatenate([v_ref[pl.ds(start, A_WIN), :], ones_col], axis=1)
        o = jnp.dot(p.astype(BF16), va, preferred_element_type=F32)
        o = o[:, 0:LANES] / o[:, LANES:LANES + 1]
        rows = pl.ds(pl.multiple_of(sb * A_TQ, A_TQ), A_TQ)
        o_ref[rows, :] = jnp.where(head0, o[0:A_TQ], o[A_TQ:2 * A_TQ]).astype(o_ref.dtype)

    scores(0, sa_ref)

    def pair(first):
        scores(first + 1, sb_ref)
        absorb(sa_ref, first)
        scores(first + 2, sa_ref)
        absorb(sb_ref, first + 1)

    def quad(jj, carry):
        pair(4 * jj)
        pair(4 * jj + 2)
        return carry

    lax.fori_loop(0, A_SUB // 4 - 1, quad, 0)
    pair(A_SUB - 4)
    scores(A_SUB - 1, sb_ref)
    absorb(sa_ref, A_SUB - 2)
    absorb(sb_ref, A_SUB - 1)


def _attn_a(qkv_pad, bias_tab, s):
    b = qkv_pad.shape[0]
    d = A_HEADS * A_HEAD_DIM
    pairs = d // LANES
    sp = s + PAD_ROWS
    rows = A_TQ * A_SUB
    return pl.pallas_call(
        _attn_a_kernel,
        out_shape=jax.ShapeDtypeStruct((b, s, d), BF16),
        grid=(pairs, b, s // rows),
        in_specs=[
            pl.BlockSpec((None, sp, LANES), lambda j, bb, i: (bb, 0, j)),
            pl.BlockSpec((None, sp, LANES), lambda j, bb, i: (bb, 0, pairs + j)),
            pl.BlockSpec((None, sp, LANES), lambda j, bb, i: (bb, 0, 2 * pairs + j)),
            pl.BlockSpec((None, A_TABLES, 2 * A_TQ, A_WIN), lambda j, bb, i: (j, 0, 0, 0)),
        ],
        out_specs=pl.BlockSpec((None, rows, LANES), lambda j, bb, i: (bb, i, j)),
        scratch_shapes=[pltpu.VMEM((2 * A_TQ, A_WIN), F32), pltpu.VMEM((2 * A_TQ, A_WIN), F32)],
        compiler_params=_params(("arbitrary", "arbitrary", "arbitrary")),
        name="mixer_a",
    )(qkv_pad, qkv_pad, qkv_pad, bias_tab)


def _rel_bias_table(rel_bias):
    qi = jnp.arange(A_TQ)[:, None]
    kj = jnp.arange(A_WIN)[None, :]
    dchunk = kj // CHUNK - qi // CHUNK
    inband = (dchunk >= 0) & (dchunk <= LEFT_CHUNKS)
    period = A_TQ + A_WIN
    u = jnp.arange(period)
    u = jnp.where(u < A_WIN, u, u - period)
    idx = jnp.clip(PAD_ROWS - u, -REL_CLIP, REL_CLIP) + REL_CLIP
    g = rel_bias.astype(F32)[:, idx] * LOG2E
    flat = jnp.tile(g, (1, A_TQ))[:, :A_TQ * (period - 1)]
    tab = flat.reshape(-1, A_TQ, period - 1)[:, :, :A_WIN]
    first_real = PAD_ROWS - A_TQ * jnp.arange(A_TABLES)[:, None, None]
    keep = inband[None] & (kj[None] >= first_real)
    tabs = jnp.where(keep[None], tab[:, None], NEG)
    tabs = tabs.reshape(-1, 2, A_TABLES, A_TQ, A_WIN).transpose(0, 2, 1, 3, 4)
    return tabs.reshape(-1, A_TABLES, 2 * A_TQ, A_WIN)


def _layer_norm(z, gain, bias):
    mu = jnp.mean(z, axis=-1, keepdims=True)
    zc = z - mu
    var = jnp.mean(zc * zc, axis=-1, keepdims=True)
    return zc * lax.rsqrt(var + LN_EPS) * gain + bias


def _small_proj_t(y, yb, w_ref, n_out):
    ylo = (y - yb.astype(F32)).astype(BF16)
    w = w_ref[...]
    wh = w.astype(BF16)
    wl = (w - wh.astype(F32)).astype(BF16)
    both = jnp.dot(yb, jnp.concatenate([wh, wl], axis=1), preferred_element_type=F32)
    acc = both[:, 0:LANES] + jnp.dot(ylo, wh, preferred_element_type=F32) + both[:, LANES:2 * LANES]
    return acc.T[:n_out, :]


def _pad_small_weight(w):
    k, n = w.shape
    return jnp.zeros((k, LANES), F32).at[:, :n].set(w.astype(F32))


def _route(lt):
    m = jnp.max(lt, axis=0, keepdims=True)
    e = jnp.exp(lt - m)
    p = e / jnp.sum(e, axis=0, keepdims=True)
    rows = [p[k:k + 1, :] for k in range(N_EXPERTS)]
    best = None
    gi = None
    for g in range(N_GROUPS):
        r = rows[EXPERTS_PER_GROUP * g:EXPERTS_PER_GROUP * (g + 1)]
        sc = None
        for a in range(EXPERTS_PER_GROUP):
            for c in range(a + 1, EXPERTS_PER_GROUP):
                pair = r[a] + r[c]
                sc = pair if sc is None else jnp.maximum(sc, pair)
        if g == 0:
            best = sc
            gi = jnp.zeros(sc.shape, jnp.int32)
        else:
            upd = sc > best
            best = jnp.where(upd, sc, best)
            gi = jnp.where(upd, g, gi)
    pin = []
    for k in range(EXPERTS_PER_GROUP):
        v = rows[(N_GROUPS - 1) * EXPERTS_PER_GROUP + k]
        for g in range(N_GROUPS - 2, -1, -1):
            v = jnp.where(gi == g, rows[g * EXPERTS_PER_GROUP + k], v)
        pin.append(v)
    v1 = pin[0]
    i1 = jnp.zeros(v1.shape, jnp.int32)
    for k in range(1, EXPERTS_PER_GROUP):
        upd = pin[k] > v1
        v1 = jnp.where(upd, pin[k], v1)
        i1 = jnp.where(upd, k, i1)
    v2 = jnp.full(v1.shape, -1.0, F32)
    i2 = jnp.zeros(v1.shape, jnp.int32)
    for k in range(EXPERTS_PER_GROUP):
        cand = jnp.where(i1 == k, -1.0, pin[k])
        upd = cand > v2
        v2 = jnp.where(upd, cand, v2)
        i2 = jnp.where(upd, k, i2)
    tot = v1 + v2
    base = gi * EXPERTS_PER_GROUP
    experts = jnp.concatenate([base + i1, base + i2], axis=0)
    gates = jnp.concatenate([v1 / tot, v2 / tot], axis=0)
    return experts, gates


def _post_attn_kernel(a0_ref, alo_ref, ahi_ref, w_ref, x_ref, g_ref, b_ref, rw_ref, rb_ref,
                      x1_ref, x1b_ref, ex_ref, gt_ref, h0_ref, h1_ref):
    t = POST_TM

    def project(a_ref, h_ref):
        h_ref[...] = jnp.dot(a_ref[...], w_ref[...], preferred_element_type=F32)

    def finish(h_ref, half):
        rows = slice(half * t, (half + 1) * t)
        y = _layer_norm(ALPHA * x_ref[rows, :] + h_ref[...], g_ref[...], b_ref[...])
        yb = y.astype(BF16)
        x1_ref[rows, :] = y
        x1b_ref[rows, :] = yb
        lt = _small_proj_t(y, yb, rw_ref, N_EXPERTS) + rb_ref[...]
        experts, gates = _route(lt)
        ex_ref[:, rows] = experts
        gt_ref[:, rows] = gates

    @pl.when(pl.program_id(0) == 0)
    def _():
        project(a0_ref, h0_ref)

    project(alo_ref, h1_ref)
    finish(h0_ref, 0)
    project(ahi_ref, h0_ref)
    finish(h1_ref, 1)


def _post_attn(attn, w_o, x, gain, bias, rw, rb):
    n, d = x.shape
    t = POST_TM
    last_tile = n // t - 1
    row = lambda i: (i, 0)
    const = lambda i: (0, 0)
    return pl.pallas_call(
        _post_attn_kernel,
        out_shape=(jax.ShapeDtypeStruct((n, d), F32),
                   jax.ShapeDtypeStruct((n, d), BF16),
                   jax.ShapeDtypeStruct((2, n), jnp.int32),
                   jax.ShapeDtypeStruct((2, n), F32)),
        grid=(n // (2 * t),),
        in_specs=[
            pl.BlockSpec((t, d), const),
            pl.BlockSpec((t, d), lambda i: (2 * i + 1, 0)),
            pl.BlockSpec((t, d), lambda i: (jnp.minimum(2 * i + 2, last_tile), 0)),
            pl.BlockSpec((d, d), const),
            pl.BlockSpec((2 * t, d), row),
            pl.BlockSpec((1, d), const),
            pl.BlockSpec((1, d), const),
            pl.BlockSpec((d, LANES), const),
            pl.BlockSpec((N_EXPERTS, 1), const),
        ],
        out_specs=(pl.BlockSpec((2 * t, d), row),
                   pl.BlockSpec((2 * t, d), row),
                   pl.BlockSpec((2, 2 * t), lambda i: (0, i)),
                   pl.BlockSpec((2, 2 * t), lambda i: (0, i))),
        scratch_shapes=[pltpu.VMEM((t, d), F32), pltpu.VMEM((t, d), F32)],
        compiler_params=_params(("arbitrary",)),
        name="out_proj_ln_router",
    )(attn, attn, attn, w_o, x, gain, bias, rw, rb)


def _moe_kernel(iblk, iex, ivalid, ifirst, starts, ends, x_ref, wg_ref, wu_ref, wd_ref, o_ref,
                acc_ref):
    w = pl.program_id(0)
    f = pl.program_id(1)

    @pl.when((ifirst[w] == 1) & (f == 0))
    def _():
        acc_ref[...] = jnp.zeros_like(acc_ref)

    @pl.when(ivalid[w] == 1)
    def _():
        wg = wg_ref[...].astype(BF16)
        wu = wu_ref[...].astype(BF16)
        wd = wd_ref[...].astype(BF16)
        e = iex[w]
        lo = starts[e]
        hi = ends[e]
        base = iblk[w] * MOE_BLK

        def swiglu(x, keep):
            hg = jnp.dot(x, wg, preferred_element_type=F32)
            hu = jnp.dot(x, wu, preferred_element_type=F32)
            hid = hg * jax.nn.sigmoid(hg) * hu
            if keep is not None:
                hid = jnp.where(keep, hid, 0.0)
            return jnp.dot(hid.astype(BF16), wd, preferred_element_type=F32)

        whole = (lo <= base) & (hi >= base + MOE_BLK)

        @pl.when(whole)
        def _():
            acc_ref[...] += swiglu(x_ref[...], None)

        @pl.when(jnp.logical_not(whole))
        def _():
            for c in range(MOE_BLK // MOE_CHUNK):
                r0 = base + c * MOE_CHUNK

                @pl.when((lo < r0 + MOE_CHUNK) & (hi > r0))
                def _():
                    rows = slice(c * MOE_CHUNK, (c + 1) * MOE_CHUNK)
                    row = r0 + lax.broadcasted_iota(jnp.int32, (MOE_CHUNK, 1), 0)
                    acc_ref[rows, :] += swiglu(x_ref[rows, :], (row >= lo) & (row < hi))

        @pl.when(f == pl.num_programs(1) - 1)
        def _():
            o_ref[...] = acc_ref[...].astype(o_ref.dtype)


def _moe(xs, w_gate, w_up, w_down, layer, meta):
    a, d = xs.shape
    dff = w_gate.shape[-1]
    nf = dff // MOE_TF
    n_items = meta[0].shape[0]

    def fsel(w, f, ivalid):
        return jnp.where(ivalid[w] == 1, f, nf - 1)

    grid_spec = pltpu.PrefetchScalarGridSpec(
        num_scalar_prefetch=6,
        grid=(n_items, nf),
        in_specs=[
            pl.BlockSpec((MOE_BLK, d), lambda w, f, ib, ie, iv, i1, st, en: (ib[w], 0)),
            pl.BlockSpec((None, None, d, MOE_TF),
                         lambda w, f, ib, ie, iv, i1, st, en: (layer, ie[w], 0, fsel(w, f, iv))),
            pl.BlockSpec((None, None, d, MOE_TF),
                         lambda w, f, ib, ie, iv, i1, st, en: (layer, ie[w], 0, fsel(w, f, iv))),
            pl.BlockSpec((None, None, MOE_TF, d),
                         lambda w, f, ib, ie, iv, i1, st, en: (layer, ie[w], fsel(w, f, iv), 0)),
        ],
        out_specs=pl.BlockSpec((MOE_BLK, d), lambda w, f, ib, ie, iv, i1, st, en: (ib[w], 0)),
        scratch_shapes=[pltpu.VMEM((MOE_BLK, d), F32)],
    )
    return pl.pallas_call(
        _moe_kernel,
        out_shape=jax.ShapeDtypeStruct((a, d), BF16),
        grid_spec=grid_spec,
        compiler_params=_params(("arbitrary", "arbitrary")),
        name="moe_experts",
    )(*meta, xs, w_gate, w_up, w_down)


def _moe_plan(experts):
    n = experts.shape[1]
    a = 2 * n
    nblk = a // MOE_BLK
    n_items = nblk + N_EXPERTS - 1
    e_flat = experts.reshape(-1)
    iota = jnp.arange(a, dtype=jnp.int32)
    e_sorted, order = lax.sort((e_flat, iota), num_keys=1, is_stable=True)
    tok_sorted = order % n
    _, pos = lax.sort((order, iota), num_keys=1)
    pos = pos.reshape(2, n)
    eids = jnp.arange(N_EXPERTS, dtype=jnp.int32)
    starts = jnp.searchsorted(e_sorted, eids, side="left").astype(jnp.int32)
    ends = jnp.searchsorted(e_sorted, eids, side="right").astype(jnp.int32)
    e_lo = e_sorted[0::MOE_BLK]
    e_hi = e_sorted[MOE_BLK - 1::MOE_BLK]
    per_blk = e_hi - e_lo + 1
    iend = jnp.cumsum(per_blk).astype(jnp.int32)
    istart = iend - per_blk
    total = iend[-1]
    w = jnp.arange(n_items, dtype=jnp.int32)
    blk = jnp.clip(jnp.searchsorted(iend, w, side="right"), 0, nblk - 1).astype(jnp.int32)
    ex = e_lo[blk] + (w - istart[blk])
    valid = w < total
    blk = jnp.where(valid, blk, nblk - 1)
    ex = jnp.where(valid, ex, e_hi[nblk - 1])
    first = valid & (w == istart[blk])
    meta = (blk, ex.astype(jnp.int32), valid.astype(jnp.int32), first.astype(jnp.int32),
            starts, ends)
    return meta, tok_sorted, pos


def _combine_ln_kernel(x_ref, y0_ref, y1_ref, gt_ref, g_ref, b_ref, o_ref, ob_ref):
    gt = gt_ref[...]
    y = y0_ref[...].astype(F32) * gt[:, 0:1] + y1_ref[...].astype(F32) * gt[:, 1:2]
    out = _layer_norm(ALPHA * x_ref[...] + y, g_ref[...], b_ref[...])
    o_ref[...] = out
    ob_ref[...] = out.astype(BF16)


def _combine_ln(x, y0, y1, gates_t, gain, bias):
    n, d = x.shape
    tm = LN_TM
    row = lambda i: (i, 0)
    const = lambda i: (0, 0)
    return pl.pallas_call(
        _combine_ln_kernel,
        out_shape=(jax.ShapeDtypeStruct((n, d), F32), jax.ShapeDtypeStruct((n, d), BF16)),
        grid=(n // tm,),
        in_specs=[
            pl.BlockSpec((tm, d), row),
            pl.BlockSpec((tm, d), row),
            pl.BlockSpec((tm, d), row),
            pl.BlockSpec((tm, 2), row),
            pl.BlockSpec((1, d), const),
            pl.BlockSpec((1, d), const),
        ],
        out_specs=(pl.BlockSpec((tm, d), row), pl.BlockSpec((tm, d), row)),
        compiler_params=_params(("arbitrary",)),
        name="moe_combine_ln",
    )(x, y0, y1, gates_t, gain, bias)


def _fg_kernel(x_ref, w_ref, b_ref, tri_ref, qf_ref, kf_ref, c_ref, carry_ref, koff_ref):
    i = pl.program_id(1)

    @pl.when(i == 0)
    def _():
        carry_ref[...] = jnp.zeros_like(carry_ref)

    @pl.when(i % (FOX_TK // FOX_T) == 0)
    def _():
        koff_ref[...] = jnp.zeros_like(koff_ref)

    def split3(v):
        hi = v.astype(BF16)
        r1 = v - hi.astype(F32)
        mid = r1.astype(BF16)
        lo = (r1 - mid.astype(F32)).astype(BF16)
        return hi, mid, lo

    def columns(v, first_lane):
        hi, mid, lo = (p.astype(F32) for p in split3(v))
        return (pltpu.roll(hi, first_lane, 1) + pltpu.roll(mid, first_lane + B_HEADS, 1)
                + pltpu.roll(lo, first_lane + 2 * B_HEADS, 1))

    y = x_ref[...]
    yb = y.astype(BF16)
    ylo = (y - yb.astype(F32)).astype(BF16)
    w = w_ref[...]
    wh = w.astype(BF16)
    wl = (w - wh.astype(F32)).astype(BF16)
    z = (jnp.dot(yb, wh, preferred_element_type=F32) + jnp.dot(ylo, wh, preferred_element_type=F32)
         + jnp.dot(yb, wl, preferred_element_type=F32)) + b_ref[...]
    head_lane = lax.broadcasted_iota(jnp.int32, (FOX_T, LANES), 1) < B_HEADS
    lf = jnp.where(head_lane, (jnp.minimum(z, 0.0) - jnp.log1p(jnp.exp(-jnp.abs(z)))) * LOG2E, 0.0)
    parts = jnp.concatenate(split3(lf), axis=1)
    cs3 = jnp.dot(tri_ref[...], parts, preferred_element_type=F32)
    cs = cs3[:, 0:LANES] + cs3[:, LANES:2 * LANES] + cs3[:, 2 * LANES:3 * LANES]
    c_ref[...] = carry_ref[...]
    carry_ref[...] = carry_ref[...] + cs[FOX_T - 1:FOX_T, :]
    csk = cs + koff_ref[...]
    koff_ref[...] = koff_ref[...] + cs[FOX_T - 1:FOX_T, :]
    qf_ref[...] = columns(cs, 0).astype(BF16)
    kf_ref[...] = columns(-csk, FOX_KLANE).astype(BF16)


def _forget_columns(x, fw, fb):
    b, s, d = x.shape
    t = FOX_T
    nblk = s // t
    tri = (jnp.arange(t)[:, None] >= jnp.arange(t)[None, :]).astype(BF16)
    const = lambda bb, i: (0, 0)
    cols = jax.ShapeDtypeStruct((b, s, LANES), BF16)
    col_spec = pl.BlockSpec((None, t, LANES), lambda bb, i: (bb, i, 0))
    return pl.pallas_call(
        _fg_kernel,
        out_shape=(cols, cols, jax.ShapeDtypeStruct((b, nblk, 1, LANES), F32)),
        grid=(b, nblk),
        in_specs=[
            pl.BlockSpec((None, t, d), lambda bb, i: (bb, i, 0)),
            pl.BlockSpec((d, LANES), const),
            pl.BlockSpec((1, LANES), const),
            pl.BlockSpec((t, t), const),
        ],
        out_specs=(col_spec, col_spec,
                   pl.BlockSpec((None, None, 1, LANES), lambda bb, i: (bb, i, 0, 0))),
        scratch_shapes=[pltpu.VMEM((1, LANES), F32), pltpu.VMEM((1, LANES), F32)],
        compiler_params=_params(("arbitrary", "arbitrary")),
        name="forget_cumsum",
    )(x, fw, fb, tri)


def _fox_kernel(c_tab, q_ref, qf_ref, k_ref, kf_ref, v_ref, o_ref,
                qa_ref, kfh_ref, sa_ref, sb_ref, m_ref, acc_ref):
    h = pl.program_id(1)
    i = pl.program_id(2)
    tq, tk = FOX_T, FOX_TK
    per_tile = tk // tq
    lane = lax.broadcasted_iota(jnp.int32, (tq, LANES), 1)
    mine = (lane & (B_HEADS - 1)) == h
    q_side = lane < FOX_KLANE
    part_lane = (lane & (FOX_KLANE - 1)) < 3 * B_HEADS

    @pl.when(i == 0)
    def _():
        k_ones = jnp.where(mine & part_lane, 1.0, 0.0).astype(BF16)

        def fill(r, carry):
            rows = pl.ds(pl.multiple_of(r * tq, tq), tq)
            kfh_ref[rows, :] = jnp.where(q_side, k_ones, kf_ref[rows, :])
            return carry

        lax.fori_loop(0, kf_ref.shape[0] // tq, fill, 0)
    cbase = (pl.program_id(0) * B_HEADS + pl.program_id(1)) * pl.num_programs(2)
    c_q = jnp.full((1, LANES), c_tab[cbase + i], F32)
    qa_ref[:, 0:B_HEAD_DIM] = q_ref[...]
    q_ones = jnp.where(mine & part_lane, 1.0, 0.0).astype(BF16)
    qa_ref[:, B_HEAD_DIM:2 * B_HEAD_DIM] = jnp.where(q_side, qf_ref[...], q_ones)
    m_ref[...] = jnp.full(m_ref.shape, NEG, F32)
    acc_ref[...] = jnp.zeros_like(acc_ref)
    ones_col = jnp.where(lax.broadcasted_iota(jnp.int32, (tk, LANES), 1) == 0, 1.0, 0.0).astype(BF16)

    def scores(jt, dst_ref):
        ks = pl.multiple_of(jt * tk, tk)
        ka = jnp.concatenate([k_ref[pl.ds(ks, tk), :], kfh_ref[pl.ds(ks, tk), :]], axis=1)
        dst_ref[...] = lax.dot_general(qa_ref[...], ka, (((1,), (1,)), ((), ())),
                                       preferred_element_type=F32)

    def absorb(src_ref, jt, last):
        ks = pl.multiple_of(jt * tk, tk)
        s = src_ref[...]
        if last:
            row = lax.broadcasted_iota(jnp.int32, (tq, tk), 0) + (i - per_tile * jt) * tq
            col = lax.broadcasted_iota(jnp.int32, (tq, tk), 1)
            s = jnp.where(col <= row, s, NEG)
        shift = c_q - jnp.full((1, LANES), c_tab[cbase + per_tile * jt], F32)
        m_prev = m_ref[...]
        m_new = jnp.maximum(m_prev, jnp.max(s, axis=-1, keepdims=True) + shift)
        p = jnp.exp2(s - jnp.concatenate([m_new - shift] * (tk // LANES), axis=1))
        scale = jnp.exp2(m_prev - m_new)
        va = jnp.concatenate([v_ref[pl.ds(ks, tk), :], ones_col], axis=1)
        acc_ref[...] = jnp.concatenate([scale, scale], axis=1) * acc_ref[...] + jnp.dot(
            p.astype(BF16), va, preferred_element_type=F32)
        m_ref[...] = m_new

    n_full = i // per_tile
    scores(0, sa_ref)

    def pair(first):
        scores(first + 1, sb_ref)
        absorb(sa_ref, first, False)
        scores(first + 2, sa_ref)
        absorb(sb_ref, first + 1, False)

    def quad(jj, carry):
        pair(4 * jj)
        pair(4 * jj + 2)
        return carry

    n_quads = lax.shift_right_logical(n_full, 2)
    lax.fori_loop(0, n_quads, quad, 0)

    @pl.when((n_full & 2) != 0)
    def _():
        pair(n_quads * 4)

    @pl.when((n_full & 1) == 1)
    def _():
        scores(n_full, sb_ref)
        absorb(sa_ref, n_full - 1, False)
        absorb(sb_ref, n_full, True)

    @pl.when((n_full & 1) == 0)
    def _():
        absorb(sa_ref, n_full, True)

    acc = acc_ref[...]
    o_ref[...] = (acc[:, 0:B_HEAD_DIM] / acc[:, B_HEAD_DIM:B_HEAD_DIM + 1]).astype(o_ref.dtype)


def _fox(q, kv, qf, kf, c_tab):
    b, s, d = q.shape
    t = FOX_T
    hd = B_HEAD_DIM
    grid_spec = pltpu.PrefetchScalarGridSpec(
        num_scalar_prefetch=1,
        grid=(b, B_HEADS, s // t),
        in_specs=[
            pl.BlockSpec((None, t, hd), lambda bb, h, i, c: (bb, i, h)),
            pl.BlockSpec((None, t, LANES), lambda bb, h, i, c: (bb, i, 0)),
            pl.BlockSpec((None, s, hd), lambda bb, h, i, c: (bb, 0, h)),
            pl.BlockSpec((None, s, LANES), lambda bb, h, i, c: (bb, 0, 0)),
            pl.BlockSpec((None, s, hd), lambda bb, h, i, c: (bb, 0, B_HEADS + h)),
        ],
        out_specs=pl.BlockSpec((None, t, hd), lambda bb, h, i, c: (bb, i, h)),
        scratch_shapes=[
            pltpu.VMEM((t, 2 * hd), BF16),
            pltpu.VMEM((s, LANES), BF16),
            pltpu.VMEM((t, FOX_TK), F32),
            pltpu.VMEM((t, FOX_TK), F32),
            pltpu.VMEM((t, LANES), F32),
            pltpu.VMEM((t, 2 * hd), F32),
        ],
    )
    return pl.pallas_call(
        _fox_kernel,
        out_shape=jax.ShapeDtypeStruct((b, s, d), BF16),
        grid_spec=grid_spec,
        compiler_params=_params(("arbitrary", "arbitrary", "arbitrary")),
        name="mixer_b",
    )(c_tab, q, qf, kv, kf, kv)


def _moe_layer(x1, x1b, experts, gates, w_gate, w_up, w_down, layer, gain, bias):
    meta, tok_sorted, pos = _moe_plan(experts)
    xs = x1b[tok_sorted]
    yb = _moe(xs, w_gate, w_up, w_down, layer, meta)
    return _combine_ln(x1, yb[pos[0]], yb[pos[1]], gates.T, gain, bias)


def kernel(x, a_w_qkv, a_rel_bias, a_w_o, kv_w, fg_w, fg_b, b_w_q, b_w_o, router_w, router_b,
           moe_w_gate, moe_w_up, moe_w_down, ln_gain, ln_bias):
    b, s, d = x.shape
    n = b * s
    assert MM_TM == PAD_ROWS and s % MM_TM == 0 and s % FOX_TK == 0 and FOX_TK % FOX_T == 0
    assert s % (A_TQ * A_SUB) == 0 and PAD_ROWS % A_TQ == 0 and A_SUB % 4 == 0 and A_SUB >= 8
    assert (2 * n) % MOE_BLK == 0

    rw = _pad_small_weight(router_w)
    rb = router_b.astype(F32).reshape(N_EXPERTS, 1)
    gain = ln_gain.astype(F32).reshape(DEPTH, 2, 1, d)
    bias = ln_bias.astype(F32).reshape(DEPTH, 2, 1, d)
    wg, wu, wd = moe_w_gate, moe_w_up, moe_w_down

    qscale = jnp.concatenate([jnp.full((d,), LOG2E / math.sqrt(A_HEAD_DIM), F32),
                              jnp.ones((2 * d,), F32)])
    w_qkv = (a_w_qkv[0] * qscale[None, :]).astype(BF16)
    qkv = _mm(x, w_qkv, pad_blocks=1)
    attn = _attn_a(qkv, _rel_bias_table(a_rel_bias[0]), s)
    x1, x1b, experts, gates = _post_attn(attn.reshape(n, d), a_w_o[0].astype(BF16),
                                         x.reshape(n, d), gain[0, 0], bias[0, 0], rw, rb)
    x2, x2b = _moe_layer(x1, x1b, experts, gates, wg, wu, wd, 0, gain[0, 1], bias[0, 1])

    kv = _mm(x2b.reshape(b, s, d), kv_w.astype(BF16))
    fb = jnp.zeros((1, LANES), F32).at[0, :B_HEADS].set(fg_b.astype(F32))
    qf, kf, carry_in = _forget_columns(x2.reshape(b, s, d), _pad_small_weight(fg_w), fb)
    c_tab = carry_in[:, :, 0, :B_HEADS].transpose(0, 2, 1).reshape(-1)

    w_q = (b_w_q[0] * (LOG2E / math.sqrt(B_HEAD_DIM))).astype(BF16)
    q = _mm(x2b.reshape(b, s, d), w_q)
    attn = _fox(q, kv, qf, kf, c_tab)
    x3, x3b, experts, gates = _post_attn(attn.reshape(n, d), b_w_o[0].astype(BF16), x2,
                                         gain[1, 0], bias[1, 0], rw, rb)
    x4, _ = _moe_layer(x3, x3b, experts, gates, wg, wu, wd, 1, gain[1, 1], bias[1, 1])
    return x4.reshape(b, s, d)
```

```python
import functools
import math

import jax
import jax.numpy as jnp
from jax import lax
from jax.experimental import pallas as pl
from jax.experimental.pallas import tpu as pltpu

F32 = jnp.float32
BF16 = jnp.bfloat16

CHUNK = 64
LEFT_CHUNKS = 8
A_HEADS = 32
A_HEAD_DIM = 64
REL_CLIP = 2 * CHUNK
B_HEADS = 16
B_HEAD_DIM = 128
N_EXPERTS = 16
N_GROUPS = 4
EXPERTS_PER_GROUP = 4
DEPTH = 2
ALPHA = (2.0 * DEPTH) ** 0.25
LN_EPS = 1e-5
NEG = -1e30
LOG2E = 1.4426950408889634

LANES = 128
VMEM_LIMIT = 56 * 1024 * 1024

MM_TM = 512
MM_TN = 2048
PAD_ROWS = LEFT_CHUNKS * CHUNK
A_TQ = 256
A_WIN = A_TQ + PAD_ROWS
A_SUB = 32
A_TABLES = PAD_ROWS // A_TQ + 1
POST_TM = 256
MOE_BLK = 1024
MOE_CHUNK = 256
MOE_TF = 256
LN_TM = 512
FOX_T = 512
FOX_TK = 1024
FOX_KLANE = 64
FOX_RUNS = (8, 4, 2)


def _params(sem):
    return pltpu.CompilerParams(dimension_semantics=sem, vmem_limit_bytes=VMEM_LIMIT)


def _mm_kernel(x_ref, w_ref, o_ref, *, pad_blocks):
    def compute():
        o_ref[...] = jnp.dot(x_ref[...].astype(BF16), w_ref[...],
                             preferred_element_type=F32).astype(o_ref.dtype)

    if pad_blocks:
        i = pl.program_id(1)

        @pl.when(i < pad_blocks)
        def _():
            o_ref[...] = jnp.zeros_like(o_ref)

        pl.when(i >= pad_blocks)(compute)
    else:
        compute()


def _mm(x, w, *, pad_blocks=0):
    b, s, k = x.shape
    m = w.shape[1]
    tn = min(MM_TN, m)
    grid = (b, s // MM_TM + pad_blocks, m // tn)
    return pl.pallas_call(
        functools.partial(_mm_kernel, pad_blocks=pad_blocks),
        out_shape=jax.ShapeDtypeStruct((b, s + pad_blocks * MM_TM, m), BF16),
        grid=grid,
        in_specs=[
            pl.BlockSpec((None, MM_TM, k),
                         lambda bb, i, j: (bb, jnp.maximum(i - pad_blocks, 0), 0)),
            pl.BlockSpec((k, tn), lambda bb, i, j: (0, j)),
        ],
        out_specs=pl.BlockSpec((None, MM_TM, tn), lambda bb, i, j: (bb, i, j)),
        compiler_params=_params(("arbitrary", "arbitrary", "arbitrary")),
        name="dense_proj",
    )(x, w)


def _attn_a_kernel(q_ref, k_ref, v_ref, bias_ref, o_ref, sa_ref, sb_ref):
    i = pl.program_id(2)
    head0 = lax.broadcasted_iota(jnp.int32, (A_TQ, LANES), 1) < A_HEAD_DIM
    ones_col = jnp.where(lax.broadcasted_iota(jnp.int32, (A_WIN, LANES), 1) == 0, 1.0, 0.0).astype(BF16)

    def window(sb):
        g = i * A_SUB + sb
        return g, pl.multiple_of(g * A_TQ, A_TQ)

    def scores(sb, dst_ref):
        _, start = window(sb)
        q = q_ref[pl.ds(start + PAD_ROWS, A_TQ), :]
        zero = jnp.zeros_like(q)
        q2 = jnp.concatenate([jnp.where(head0, q, zero), jnp.where(head0, zero, q)], axis=0)
        dst_ref[...] = lax.dot_general(q2, k_ref[pl.ds(start, A_WIN), :], (((1,), (1,)), ((), ())),
                                       preferred_element_type=F32)

    def absorb(src_ref, sb):
        g, start = window(sb)
        s = src_ref[...] + bias_ref[jnp.minimum(g, A_TABLES - 1)]
        p = jnp.exp2(s - jnp.max(s, axis=-1, keepdims=True))
        va = jnp.concatenate([v_ref[pl.ds(start, A_WIN), :], ones_col], axis=1)
        o = jnp.dot(p.astype(BF16), va, preferred_element_type=F32)
        o = o[:, 0:LANES] / o[:, LANES:LANES + 1]
        rows = pl.ds(pl.multiple_of(sb * A_TQ, A_TQ), A_TQ)
        o_ref[rows, :] = jnp.where(head0, o[0:A_TQ], o[A_TQ:2 * A_TQ]).astype(o_ref.dtype)

    scores(0, sa_ref)

    def pair(first):
        scores(first + 1, sb_ref)
        absorb(sa_ref, first)
        scores(first + 2, sa_ref)
        absorb(sb_ref, first + 1)

    def quad(jj, carry):
        pair(4 * jj)
        pair(4 * jj + 2)
        return carry

    lax.fori_loop(0, A_SUB // 4 - 1, quad, 0)
    pair(A_SUB - 4)
    scores(A_SUB - 1, sb_ref)
    absorb(sa_ref, A_SUB - 2)
    absorb(sb_ref, A_SUB - 1)


def _attn_a(qkv_pad, bias_tab, s):
    b = qkv_pad.shape[0]
    d = A_HEADS * A_HEAD_DIM
    pairs = d // LANES
    sp = s + PAD_ROWS
    rows = A_TQ * A_SUB
    return pl.pallas_call(
        _attn_a_kernel,
        out_shape=jax.ShapeDtypeStruct((b, s, d), BF16),
        grid=(pairs, b, s // rows),
        in_specs=[
            pl.BlockSpec((None, sp, LANES), lambda j, bb, i: (bb, 0, j)),
            pl.BlockSpec((None, sp, LANES), lambda j, bb, i: (bb, 0, pairs + j)),
            pl.BlockSpec((None, sp, LANES), lambda j, bb, i: (bb, 0, 2 * pairs + j)),
            pl.BlockSpec((None, A_TABLES, 2 * A_TQ, A_WIN), lambda j, bb, i: (j, 0, 0, 0)),
        ],
        out_specs=pl.BlockSpec((None, rows, LANES), lambda j, bb, i: (bb, i, j)),
        scratch_shapes=[pltpu.VMEM((2 * A_TQ, A_WIN), F32), pltpu.VMEM((2 * A_TQ, A_WIN), F32)],
        compiler_params=_params(("arbitrary", "arbitrary", "arbitrary")),
        name="mixer_a",
    )(qkv_pad, qkv_pad, qkv_pad, bias_tab)


def _rel_bias_table(rel_bias):
    qi = jnp.arange(A_TQ)[:, None]
    kj = jnp.arange(A_WIN)[None, :]
    dchunk = kj // CHUNK - qi // CHUNK
    inband = (dchunk >= 0) & (dchunk <= LEFT_CHUNKS)
    period = A_TQ + A_WIN
    u = jnp.arange(period)
    u = jnp.where(u < A_WIN, u, u - period)
    idx = jnp.clip(PAD_ROWS - u, -REL_CLIP, REL_CLIP) + REL_CLIP
    g = rel_bias.astype(F32)[:, idx] * LOG2E
    flat = jnp.tile(g, (1, A_TQ))[:, :A_TQ * (period - 1)]
    tab = flat.reshape(-1, A_TQ, period - 1)[:, :, :A_WIN]
    first_real = PAD_ROWS - A_TQ * jnp.arange(A_TABLES)[:, None, None]
    keep = inband[None] & (kj[None] >= first_real)
    tabs = jnp.where(keep[None], tab[:, None], NEG)
    tabs = tabs.reshape(-1, 2, A_TABLES, A_TQ, A_WIN).transpose(0, 2, 1, 3, 4)
    return tabs.reshape(-1, A_TABLES, 2 * A_TQ, A_WIN)


def _layer_norm(z, gain, bias):
    mu = jnp.mean(z, axis=-1, keepdims=True)
    zc = z - mu
    var = jnp.mean(zc * zc, axis=-1, keepdims=True)
    return zc * lax.rsqrt(var + LN_EPS) * gain + bias


def _small_proj_t(y, yb, w_ref, n_out):
    ylo = (y - yb.astype(F32)).astype(BF16)
    w = w_ref[...]
    wh = w.astype(BF16)
    wl = (w - wh.astype(F32)).astype(BF16)
    both = jnp.dot(yb, jnp.concatenate([wh, wl], axis=1), preferred_element_type=F32)
    acc = both[:, 0:LANES] + jnp.dot(ylo, wh, preferred_element_type=F32) + both[:, LANES:2 * LANES]
    return acc.T[:n_out, :]


def _pad_small_weight(w):
    k, n = w.shape
    return jnp.zeros((k, LANES), F32).at[:, :n].set(w.astype(F32))


def _route(lt):
    m = jnp.max(lt, axis=0, keepdims=True)
    e = jnp.exp(lt - m)
    p = e / jnp.sum(e, axis=0, keepdims=True)
    rows = [p[k:k + 1, :] for k in range(N_EXPERTS)]
    best = None
    gi = None
    for g in range(N_GROUPS):
        r = rows[EXPERTS_PER_GROUP * g:EXPERTS_PER_GROUP * (g + 1)]
        sc = None
        for a in range(EXPERTS_PER_GROUP):
            for c in range(a + 1, EXPERTS_PER_GROUP):
                pair = r[a] + r[c]
                sc = pair if sc is None else jnp.maximum(sc, pair)
        if g == 0:
            best = sc
            gi = jnp.zeros(sc.shape, jnp.int32)
        else:
            upd = sc > best
            best = jnp.where(upd, sc, best)
            gi = jnp.where(upd, g, gi)
    pin = []
    for k in range(EXPERTS_PER_GROUP):
        v = rows[(N_GROUPS - 1) * EXPERTS_PER_GROUP + k]
        for g in range(N_GROUPS - 2, -1, -1):
            v = jnp.where(gi == g, rows[g * EXPERTS_PER_GROUP + k], v)
        pin.append(v)
    v1 = pin[0]
    i1 = jnp.zeros(v1.shape, jnp.int32)
    for k in range(1, EXPERTS_PER_GROUP):
        upd = pin[k] > v1
        v1 = jnp.where(upd, pin[k], v1)
        i1 = jnp.where(upd, k, i1)
    v2 = jnp.full(v1.shape, -1.0, F32)
    i2 = jnp.zeros(v1.shape, jnp.int32)
    for k in range(EXPERTS_PER_GROUP):
        cand = jnp.where(i1 == k, -1.0, pin[k])
        upd = cand > v2
        v2 = jnp.where(upd, cand, v2)
        i2 = jnp.where(upd, k, i2)
    tot = v1 + v2
    base = gi * EXPERTS_PER_GROUP
    experts = jnp.concatenate([base + i1, base + i2], axis=0)
    gates = jnp.concatenate([v1 / tot, v2 / tot], axis=0)
    return experts, gates


def _post_attn_kernel(a0_ref, alo_ref, ahi_ref, w_ref, x_ref, g_ref, b_ref, rw_ref, rb_ref,
                      x1_ref, x1b_ref, ex_ref, gt_ref, h0_ref, h1_ref):
    t = POST_TM

    def project(a_ref, h_ref):
        h_ref[...] = jnp.dot(a_ref[...], w_ref[...], preferred_element_type=F32)

    def finish(h_ref, half):
        rows = slice(half * t, (half + 1) * t)
        y = _layer_norm(ALPHA * x_ref[rows, :] + h_ref[...], g_ref[...], b_ref[...])
        yb = y.astype(BF16)
        x1_ref[rows, :] = y
        x1b_ref[rows, :] = yb
        lt = _small_proj_t(y, yb, rw_ref, N_EXPERTS) + rb_ref[...]
        experts, gates = _route(lt)
        ex_ref[:, rows] = experts
        gt_ref[:, rows] = gates

    @pl.when(pl.program_id(0) == 0)
    def _():
        project(a0_ref, h0_ref)

    project(alo_ref, h1_ref)
    finish(h0_ref, 0)
    project(ahi_ref, h0_ref)
    finish(h1_ref, 1)


def _post_attn(attn, w_o, x, gain, bias, rw, rb):
    n, d = x.shape
    t = POST_TM
    last_tile = n // t - 1
    row = lambda i: (i, 0)
    const = lambda i: (0, 0)
    return pl.pallas_call(
        _post_attn_kernel,
        out_shape=(jax.ShapeDtypeStruct((n, d), F32),
                   jax.ShapeDtypeStruct((n, d), BF16),
                   jax.ShapeDtypeStruct((2, n), jnp.int32),
                   jax.ShapeDtypeStruct((2, n), F32)),
        grid=(n // (2 * t),),
        in_specs=[
            pl.BlockSpec((t, d), const),
            pl.BlockSpec((t, d), lambda i: (2 * i + 1, 0)),
            pl.BlockSpec((t, d), lambda i: (jnp.minimum(2 * i + 2, last_tile), 0)),
            pl.BlockSpec((d, d), const),
            pl.BlockSpec((2 * t, d), row),
            pl.BlockSpec((1, d), const),
            pl.BlockSpec((1, d), const),
            pl.BlockSpec((d, LANES), const),
            pl.BlockSpec((N_EXPERTS, 1), const),
        ],
        out_specs=(pl.BlockSpec((2 * t, d), row),
                   pl.BlockSpec((2 * t, d), row),
                   pl.BlockSpec((2, 2 * t), lambda i: (0, i)),
                   pl.BlockSpec((2, 2 * t), lambda i: (0, i))),
        scratch_shapes=[pltpu.VMEM((t, d), F32), pltpu.VMEM((t, d), F32)],
        compiler_params=_params(("arbitrary",)),
        name="out_proj_ln_router",
    )(attn, attn, attn, w_o, x, gain, bias, rw, rb)


def _moe_kernel(iblk, iex, ivalid, ifirst, starts, ends, x_ref, wg_ref, wu_ref, wd_ref, o_ref,
                acc_ref):
    w = pl.program_id(0)
    f = pl.program_id(1)

    @pl.when((ifirst[w] == 1) & (f == 0))
    def _():
        acc_ref[...] = jnp.zeros_like(acc_ref)

    @pl.when(ivalid[w] == 1)
    def _():
        e = iex[w]
        lo = starts[e]
        hi = ends[e]
        base = iblk[w] * MOE_BLK

        def swiglu(x, wg, wu, wd, keep):
            hg = jnp.dot(x, wg, preferred_element_type=F32)
            hu = jnp.dot(x, wu, preferred_element_type=F32)
            hid = hg * jax.nn.sigmoid(hg) * hu
            if keep is not None:
                hid = jnp.where(keep, hid, 0.0)
            return jnp.dot(hid.astype(BF16), wd, preferred_element_type=F32)

        whole = (lo <= base) & (hi >= base + MOE_BLK)

        @pl.when(whole)
        def _():
            acc_ref[...] += swiglu(x_ref[...], wg_ref[...].astype(BF16), wu_ref[...].astype(BF16),
                                   wd_ref[...].astype(BF16), None)

        @pl.when(jnp.logical_not(whole))
        def _():
            wg = wg_ref[...].astype(BF16)
            wu = wu_ref[...].astype(BF16)
            wd = wd_ref[...].astype(BF16)
            for c in range(MOE_BLK // MOE_CHUNK):
                r0 = base + c * MOE_CHUNK

                @pl.when((lo < r0 + MOE_CHUNK) & (hi > r0))
                def _():
                    rows = slice(c * MOE_CHUNK, (c + 1) * MOE_CHUNK)
                    row = r0 + lax.broadcasted_iota(jnp.int32, (MOE_CHUNK, 1), 0)
                    acc_ref[rows, :] += swiglu(x_ref[rows, :], wg, wu, wd, (row >= lo) & (row < hi))

        @pl.when(f == pl.num_programs(1) - 1)
        def _():
            o_ref[...] = acc_ref[...].astype(o_ref.dtype)


def _moe(xs, w_gate, w_up, w_down, layer, meta):
    a, d = xs.shape
    dff = w_gate.shape[-1]
    nf = dff // MOE_TF
    n_items = meta[0].shape[0]

    def fsel(w, f, ivalid):
        return jnp.where(ivalid[w] == 1, f, nf - 1)

    grid_spec = pltpu.PrefetchScalarGridSpec(
        num_scalar_prefetch=6,
        grid=(n_items, nf),
        in_specs=[
            pl.BlockSpec((MOE_BLK, d), lambda w, f, ib, ie, iv, i1, st, en: (ib[w], 0)),
            pl.BlockSpec((None, None, d, MOE_TF),
                         lambda w, f, ib, ie, iv, i1, st, en: (layer, ie[w], 0, fsel(w, f, iv))),
            pl.BlockSpec((None, None, d, MOE_TF),
                         lambda w, f, ib, ie, iv, i1, st, en: (layer, ie[w], 0, fsel(w, f, iv))),
            pl.BlockSpec((None, None, MOE_TF, d),
                         lambda w, f, ib, ie, iv, i1, st, en: (layer, ie[w], fsel(w, f, iv), 0)),
        ],
        out_specs=pl.BlockSpec((MOE_BLK, d), lambda w, f, ib, ie, iv, i1, st, en: (ib[w], 0)),
        scratch_shapes=[pltpu.VMEM((MOE_BLK, d), F32)],
    )
    return pl.pallas_call(
        _moe_kernel,
        out_shape=jax.ShapeDtypeStruct((a, d), BF16),
        grid_spec=grid_spec,
        compiler_params=_params(("arbitrary", "arbitrary")),
        name="moe_experts",
    )(*meta, xs, w_gate, w_up, w_down)


def _moe_plan(experts):
    n = experts.shape[1]
    a = 2 * n
    nblk = a // MOE_BLK
    n_items = nblk + N_EXPERTS - 1
    e_flat = experts.reshape(-1)
    iota = jnp.arange(a, dtype=jnp.int32)
    e_sorted, order = lax.sort((e_flat, iota), num_keys=1, is_stable=True)
    tok_sorted = order % n
    _, pos = lax.sort((order, iota), num_keys=1)
    pos = pos.reshape(2, n)
    eids = jnp.arange(N_EXPERTS, dtype=jnp.int32)
    starts = jnp.searchsorted(e_sorted, eids, side="left").astype(jnp.int32)
    ends = jnp.searchsorted(e_sorted, eids, side="right").astype(jnp.int32)
    e_lo = e_sorted[0::MOE_BLK]
    e_hi = e_sorted[MOE_BLK - 1::MOE_BLK]
    per_blk = e_hi - e_lo + 1
    iend = jnp.cumsum(per_blk).astype(jnp.int32)
    istart = iend - per_blk
    total = iend[-1]
    w = jnp.arange(n_items, dtype=jnp.int32)
    blk = jnp.clip(jnp.searchsorted(iend, w, side="right"), 0, nblk - 1).astype(jnp.int32)
    ex = e_lo[blk] + (w - istart[blk])
    valid = w < total
    blk = jnp.where(valid, blk, nblk - 1)
    ex = jnp.where(valid, ex, e_hi[nblk - 1])
    first = valid & (w == istart[blk])
    meta = (blk, ex.astype(jnp.int32), valid.astype(jnp.int32), first.astype(jnp.int32),
            starts, ends)
    return meta, tok_sorted, pos


def _combine_ln_kernel(x_ref, y0_ref, y1_ref, gt_ref, g_ref, b_ref, o_ref, ob_ref):
    gt = gt_ref[...]
    y = y0_ref[...].astype(F32) * gt[:, 0:1] + y1_ref[...].astype(F32) * gt[:, 1:2]
    out = _layer_norm(ALPHA * x_ref[...] + y, g_ref[...], b_ref[...])
    o_ref[...] = out
    ob_ref[...] = out.astype(BF16)


def _combine_ln(x, y0, y1, gates_t, gain, bias):
    n, d = x.shape
    tm = LN_TM
    row = lambda i: (i, 0)
    const = lambda i: (0, 0)
    return pl.pallas_call(
        _combine_ln_kernel,
        out_shape=(jax.ShapeDtypeStruct((n, d), F32), jax.ShapeDtypeStruct((n, d), BF16)),
        grid=(n // tm,),
        in_specs=[
            pl.BlockSpec((tm, d), row),
            pl.BlockSpec((tm, d), row),
            pl.BlockSpec((tm, d), row),
            pl.BlockSpec((tm, 2), row),
            pl.BlockSpec((1, d), const),
            pl.BlockSpec((1, d), const),
        ],
        out_specs=(pl.BlockSpec((tm, d), row), pl.BlockSpec((tm, d), row)),
        compiler_params=_params(("arbitrary",)),
        name="moe_combine_ln",
    )(x, y0, y1, gates_t, gain, bias)


def _fg_kernel(x_ref, w_ref, b_ref, tri_ref, qf_ref, kf_ref, c_ref, carry_ref, koff_ref):
    i = pl.program_id(1)

    @pl.when(i == 0)
    def _():
        carry_ref[...] = jnp.zeros_like(carry_ref)

    @pl.when(i % (FOX_TK // FOX_T) == 0)
    def _():
        koff_ref[...] = jnp.zeros_like(koff_ref)

    def split3(v):
        hi = v.astype(BF16)
        r1 = v - hi.astype(F32)
        mid = r1.astype(BF16)
        lo = (r1 - mid.astype(F32)).astype(BF16)
        return hi, mid, lo

    def columns(v, first_lane):
        hi, mid, lo = (p.astype(F32) for p in split3(v))
        return (pltpu.roll(hi, first_lane, 1) + pltpu.roll(mid, first_lane + B_HEADS, 1)
                + pltpu.roll(lo, first_lane + 2 * B_HEADS, 1))

    y = x_ref[...]
    yb = y.astype(BF16)
    ylo = (y - yb.astype(F32)).astype(BF16)
    w = w_ref[...]
    wh = w.astype(BF16)
    wl = (w - wh.astype(F32)).astype(BF16)
    z = (jnp.dot(yb, wh, preferred_element_type=F32) + jnp.dot(ylo, wh, preferred_element_type=F32)
         + jnp.dot(yb, wl, preferred_element_type=F32)) + b_ref[...]
    head_lane = lax.broadcasted_iota(jnp.int32, (FOX_T, LANES), 1) < B_HEADS
    lf = jnp.where(head_lane, (jnp.minimum(z, 0.0) - jnp.log1p(jnp.exp(-jnp.abs(z)))) * LOG2E, 0.0)
    parts = jnp.concatenate(split3(lf), axis=1)
    cs3 = jnp.dot(tri_ref[...], parts, preferred_element_type=F32)
    cs = cs3[:, 0:LANES] + cs3[:, LANES:2 * LANES] + cs3[:, 2 * LANES:3 * LANES]
    c_ref[...] = carry_ref[...]
    carry_ref[...] = carry_ref[...] + cs[FOX_T - 1:FOX_T, :]
    csk = cs + koff_ref[...]
    koff_ref[...] = koff_ref[...] + cs[FOX_T - 1:FOX_T, :]
    qf_ref[...] = columns(cs, 0).astype(BF16)
    kf_ref[...] = columns(-csk, FOX_KLANE).astype(BF16)


def _forget_columns(x, fw, fb):
    b, s, d = x.shape
    t = FOX_T
    nblk = s // t
    tri = (jnp.arange(t)[:, None] >= jnp.arange(t)[None, :]).astype(BF16)
    const = lambda bb, i: (0, 0)
    cols = jax.ShapeDtypeStruct((b, s, LANES), BF16)
    col_spec = pl.BlockSpec((None, t, LANES), lambda bb, i: (bb, i, 0))
    return pl.pallas_call(
        _fg_kernel,
        out_shape=(cols, cols, jax.ShapeDtypeStruct((b, nblk, 1, LANES), F32)),
        grid=(b, nblk),
        in_specs=[
            pl.BlockSpec((None, t, d), lambda bb, i: (bb, i, 0)),
            pl.BlockSpec((d, LANES), const),
            pl.BlockSpec((1, LANES), const),
            pl.BlockSpec((t, t), const),
        ],
        out_specs=(col_spec, col_spec,
                   pl.BlockSpec((None, None, 1, LANES), lambda bb, i: (bb, i, 0, 0))),
        scratch_shapes=[pltpu.VMEM((1, LANES), F32), pltpu.VMEM((1, LANES), F32)],
        compiler_params=_params(("arbitrary", "arbitrary")),
        name="forget_cumsum",
    )(x, fw, fb, tri)


def _fox_kernel(c_tab, q_ref, qf_ref, k_ref, kf_ref, v_ref, o_ref,
                qa_ref, kfh_ref, sa_ref, sb_ref, m_ref, acc_ref):
    h = pl.program_id(1)
    i = pl.program_id(2)
    tq, tk = FOX_T, FOX_TK
    per_tile = tk // tq
    lane = lax.broadcasted_iota(jnp.int32, (tq, LANES), 1)
    mine = (lane & (B_HEADS - 1)) == h
    q_side = lane < FOX_KLANE
    part_lane = (lane & (FOX_KLANE - 1)) < 3 * B_HEADS

    @pl.when(i == 0)
    def _():
        k_ones = jnp.where(mine & part_lane, 1.0, 0.0).astype(BF16)

        def fill(r, carry):
            rows = pl.ds(pl.multiple_of(r * tq, tq), tq)
            kfh_ref[rows, :] = jnp.where(q_side, k_ones, kf_ref[rows, :])
            return carry

        lax.fori_loop(0, kf_ref.shape[0] // tq, fill, 0)
    cbase = (pl.program_id(0) * B_HEADS + pl.program_id(1)) * pl.num_programs(2)
    c_q = jnp.full((1, LANES), c_tab[cbase + i], F32)
    qa_ref[:, 0:B_HEAD_DIM] = q_ref[...]
    q_ones = jnp.where(mine & part_lane, 1.0, 0.0).astype(BF16)
    qa_ref[:, B_HEAD_DIM:2 * B_HEAD_DIM] = jnp.where(q_side, qf_ref[...], q_ones)
    m_ref[...] = jnp.full(m_ref.shape, NEG, F32)
    acc_ref[...] = jnp.zeros_like(acc_ref)
    ones_col = jnp.where(lax.broadcasted_iota(jnp.int32, (tk, LANES), 1) == 0, 1.0, 0.0).astype(BF16)

    def scores(jt, dst_ref):
        ks = pl.multiple_of(jt * tk, tk)
        ka = jnp.concatenate([k_ref[pl.ds(ks, tk), :], kfh_ref[pl.ds(ks, tk), :]], axis=1)
        dst_ref[...] = lax.dot_general(qa_ref[...], ka, (((1,), (1,)), ((), ())),
                                       preferred_element_type=F32)

    def absorb(src_ref, jt, last):
        ks = pl.multiple_of(jt * tk, tk)
        s = src_ref[...]
        if last:
            row = lax.broadcasted_iota(jnp.int32, (tq, tk), 0) + (i - per_tile * jt) * tq
            col = lax.broadcasted_iota(jnp.int32, (tq, tk), 1)
            s = jnp.where(col <= row, s, NEG)
        shift = c_q - jnp.full((1, LANES), c_tab[cbase + per_tile * jt], F32)
        m_prev = m_ref[...]
        m_new = jnp.maximum(m_prev, jnp.max(s, axis=-1, keepdims=True) + shift)
        p = jnp.exp2(s - jnp.concatenate([m_new - shift] * (tk // LANES), axis=1))
        scale = jnp.exp2(m_prev - m_new)
        va = jnp.concatenate([v_ref[pl.ds(ks, tk), :], ones_col], axis=1)
        acc_ref[...] = jnp.concatenate([scale, scale], axis=1) * acc_ref[...] + jnp.dot(
            p.astype(BF16), va, preferred_element_type=F32)
        m_ref[...] = m_new

    n_full = i // per_tile
    scores(0, sa_ref)

    def pair(first):
        scores(first + 1, sb_ref)
        absorb(sa_ref, first, False)
        scores(first + 2, sa_ref)
        absorb(sb_ref, first + 1, False)

    def run(first, count):
        for t in range(0, count, 2):
            pair(first + t)

    done = 0
    for length in FOX_RUNS:
        @pl.when((n_full & length) != 0)
        def _(done=done, length=length):
            run(done, length)
        done = done + (n_full & length)

    @pl.when((n_full & 1) == 1)
    def _():
        scores(n_full, sb_ref)
        absorb(sa_ref, n_full - 1, False)
        absorb(sb_ref, n_full, True)

    @pl.when((n_full & 1) == 0)
    def _():
        absorb(sa_ref, n_full, True)

    acc = acc_ref[...]
    o_ref[...] = (acc[:, 0:B_HEAD_DIM] / acc[:, B_HEAD_DIM:B_HEAD_DIM + 1]).astype(o_ref.dtype)


def _fox(q, kv, qf, kf, c_tab):
    b, s, d = q.shape
    t = FOX_T
    hd = B_HEAD_DIM
    grid_spec = pltpu.PrefetchScalarGridSpec(
        num_scalar_prefetch=1,
        grid=(b, B_HEADS, s // t),
        in_specs=[
            pl.BlockSpec((None, t, hd), lambda bb, h, i, c: (bb, i, h)),
            pl.BlockSpec((None, t, LANES), lambda bb, h, i, c: (bb, i, 0)),
            pl.BlockSpec((None, s, hd), lambda bb, h, i, c: (bb, 0, h)),
            pl.BlockSpec((None, s, LANES), lambda bb, h, i, c: (bb, 0, 0)),
            pl.BlockSpec((None, s, hd), lambda bb, h, i, c: (bb, 0, B_HEADS + h)),
        ],
        out_specs=pl.BlockSpec((None, t, hd), lambda bb, h, i, c: (bb, i, h)),
        scratch_shapes=[
            pltpu.VMEM((t, 2 * hd), BF16),
            pltpu.VMEM((s, LANES), BF16),
            pltpu.VMEM((t, FOX_TK), F32),
            pltpu.VMEM((t, FOX_TK), F32),
            pltpu.VMEM((t, LANES), F32),
            pltpu.VMEM((t, 2 * hd), F32),
        ],
    )
    return pl.pallas_call(
        _fox_kernel,
        out_shape=jax.ShapeDtypeStruct((b, s, d), BF16),
        grid_spec=grid_spec,
        compiler_params=_params(("arbitrary", "arbitrary", "arbitrary")),
        name="mixer_b",
    )(c_tab, q, qf, kv, kf, kv)


def _moe_layer(x1, x1b, experts, gates, w_gate, w_up, w_down, layer, gain, bias):
    meta, tok_sorted, pos = _moe_plan(experts)
    xs = x1b[tok_sorted]
    yb = _moe(xs, w_gate, w_up, w_down, layer, meta)
    return _combine_ln(x1, yb[pos[0]], yb[pos[1]], gates.T, gain, bias)


def kernel(x, a_w_qkv, a_rel_bias, a_w_o, kv_w, fg_w, fg_b, b_w_q, b_w_o, router_w, router_b,
           moe_w_gate, moe_w_up, moe_w_down, ln_gain, ln_bias):
    b, s, d = x.shape
    n = b * s
    assert MM_TM == PAD_ROWS and s % MM_TM == 0 and s % FOX_TK == 0 and FOX_TK % FOX_T == 0
    assert s % (A_TQ * A_SUB) == 0 and PAD_ROWS % A_TQ == 0 and A_SUB % 4 == 0 and A_SUB >= 8
    assert (2 * n) % MOE_BLK == 0 and s // FOX_TK <= 2 * FOX_RUNS[0]

    rw = _pad_small_weight(router_w)
    rb = router_b.astype(F32).reshape(N_EXPERTS, 1)
    gain = ln_gain.astype(F32).reshape(DEPTH, 2, 1, d)
    bias = ln_bias.astype(F32).reshape(DEPTH, 2, 1, d)
    wg, wu, wd = moe_w_gate, moe_w_up, moe_w_down

    qscale = jnp.concatenate([jnp.full((d,), LOG2E / math.sqrt(A_HEAD_DIM), F32),
                              jnp.ones((2 * d,), F32)])
    w_qkv = (a_w_qkv[0] * qscale[None, :]).astype(BF16)
    qkv = _mm(x, w_qkv, pad_blocks=1)
    attn = _attn_a(qkv, _rel_bias_table(a_rel_bias[0]), s)
    x1, x1b, experts, gates = _post_attn(attn.reshape(n, d), a_w_o[0].astype(BF16),
                                         x.reshape(n, d), gain[0, 0], bias[0, 0], rw, rb)
    x2, x2b = _moe_layer(x1, x1b, experts, gates, wg, wu, wd, 0, gain[0, 1], bias[0, 1])

    kv = _mm(x2b.reshape(b, s, d), kv_w.astype(BF16))
    fb = jnp.zeros((1, LANES), F32).at[0, :B_HEADS].set(fg_b.astype(F32))
    qf, kf, carry_in = _forget_columns(x2.reshape(b, s, d), _pad_small_weight(fg_w), fb)
    c_tab = carry_in[:, :, 0, :B_HEADS].transpose(0, 2, 1).reshape(-1)

    w_q = (b_w_q[0] * (LOG2E / math.sqrt(B_HEAD_DIM))).astype(BF16)
    q = _mm(x2b.reshape(b, s, d), w_q)
    attn = _fox(q, kv, qf, kf, c_tab)
    x3, x3b, experts, gates = _post_attn(attn.reshape(n, d), b_w_o[0].astype(BF16), x2,
                                         gain[1, 0], bias[1, 0], rw, rb)
    x4, _ = _moe_layer(x3, x3b, experts, gates, wg, wu, wd, 1, gain[1, 1], bias[1, 1])
    return x4.reshape(b, s, d)
```

```python
import functools
import math

import jax
import jax.numpy as jnp
from jax import lax
from jax.experimental import pallas as pl
from jax.experimental.pallas import tpu as pltpu

F32 = jnp.float32
BF16 = jnp.bfloat16

CHUNK = 64
LEFT_CHUNKS = 8
A_HEADS = 32
A_HEAD_DIM = 64
REL_CLIP = 2 * CHUNK
B_HEADS = 16
B_HEAD_DIM = 128
N_EXPERTS = 16
N_GROUPS = 4
EXPERTS_PER_GROUP = 4
DEPTH = 2
ALPHA = (2.0 * DEPTH) ** 0.25
LN_EPS = 1e-5
NEG = -1e30
LOG2E = 1.4426950408889634

LANES = 128
VMEM_LIMIT = 56 * 1024 * 1024

MM_TM = 512
MM_TN = 2048
PAD_ROWS = LEFT_CHUNKS * CHUNK
A_TQ = 256
A_WIN = A_TQ + PAD_ROWS
A_SUB = 32
A_TABLES = PAD_ROWS // A_TQ + 1
POST_TM = 256
MOE_BLK = 1024
MOE_CHUNK = 256
MOE_TF = 256
LN_TM = 512
FOX_T = 512
FOX_TK = 1024
FOX_KLANE = 64
FOX_RUNS = (8, 4, 2)


def _params(sem):
    return pltpu.CompilerParams(dimension_semantics=sem, vmem_limit_bytes=VMEM_LIMIT)


def _mm_kernel(x_ref, w_ref, o_ref, *, pad_blocks):
    def compute():
        o_ref[...] = jnp.dot(x_ref[...].astype(BF16), w_ref[...],
                             preferred_element_type=F32).astype(o_ref.dtype)

    if pad_blocks:
        i = pl.program_id(1)

        @pl.when(i < pad_blocks)
        def _():
            o_ref[...] = jnp.zeros_like(o_ref)

        pl.when(i >= pad_blocks)(compute)
    else:
        compute()


def _mm(x, w, *, pad_blocks=0):
    b, s, k = x.shape
    m = w.shape[1]
    tn = min(MM_TN, m)
    grid = (b, s // MM_TM + pad_blocks, m // tn)
    return pl.pallas_call(
        functools.partial(_mm_kernel, pad_blocks=pad_blocks),
        out_shape=jax.ShapeDtypeStruct((b, s + pad_blocks * MM_TM, m), BF16),
        grid=grid,
        in_specs=[
            pl.BlockSpec((None, MM_TM, k),
                         lambda bb, i, j: (bb, jnp.maximum(i - pad_blocks, 0), 0)),
            pl.BlockSpec((k, tn), lambda bb, i, j: (0, j)),
        ],
        out_specs=pl.BlockSpec((None, MM_TM, tn), lambda bb, i, j: (bb, i, j)),
        compiler_params=_params(("arbitrary", "arbitrary", "arbitrary")),
        name="dense_proj",
    )(x, w)


def _attn_a_kernel(q_ref, k_ref, v_ref, bias_ref, o_ref, sa_ref, sb_ref):
    i = pl.program_id(2)
    head0 = lax.broadcasted_iota(jnp.int32, (A_TQ, LANES), 1) < A_HEAD_DIM
    ones_col = jnp.where(lax.broadcasted_iota(jnp.int32, (A_WIN, LANES), 1) == 0, 1.0, 0.0).astype(BF16)

    def window(sb):
        g = i * A_SUB + sb
        return g, pl.multiple_of(g * A_TQ, A_TQ)

    def scores(sb, dst_ref):
        _, start = window(sb)
        q = q_ref[pl.ds(start + PAD_ROWS, A_TQ), :]
        zero = jnp.zeros_like(q)
        q2 = jnp.concatenate([jnp.where(head0, q, zero), jnp.where(head0, zero, q)], axis=0)
        dst_ref[...] = lax.dot_general(q2, k_ref[pl.ds(start, A_WIN), :], (((1,), (1,)), ((), ())),
                                       preferred_element_type=F32)

    def absorb(src_ref, sb):
        g, start = window(sb)
        s = src_ref[...] + bias_ref[jnp.minimum(g, A_TABLES - 1)]
        p = jnp.exp2(s - jnp.max(s, axis=-1, keepdims=True))
        va = jnp.concatenate([v_ref[pl.ds(start, A_WIN), :], ones_col], axis=1)
        o = jnp.dot(p.astype(BF16), va, preferred_element_type=F32)
        o = o[:, 0:LANES] / o[:, LANES:LANES + 1]
        rows = pl.ds(pl.multiple_of(sb * A_TQ, A_TQ), A_TQ)
        o_ref[rows, :] = jnp.where(head0, o[0:A_TQ], o[A_TQ:2 * A_TQ]).astype(o_ref.dtype)

    scores(0, sa_ref)

    def pair(first):
        scores(first + 1, sb_ref)
        absorb(sa_ref, first)
        scores(first + 2, sa_ref)
        absorb(sb_ref, first + 1)

    def quad(jj, carry):
        pair(4 * jj)
        pair(4 * jj + 2)
        return carry

    lax.fori_loop(0, A_SUB // 4 - 1, quad, 0)
    pair(A_SUB - 4)
    scores(A_SUB - 1, sb_ref)
    absorb(sa_ref, A_SUB - 2)
    absorb(sb_ref, A_SUB - 1)


def _attn_a(qkv_pad, bias_tab, s):
    b = qkv_pad.shape[0]
    d = A_HEADS * A_HEAD_DIM
    pairs = d // LANES
    sp = s + PAD_ROWS
    rows = A_TQ * A_SUB
    return pl.pallas_call(
        _attn_a_kernel,
        out_shape=jax.ShapeDtypeStruct((b, s, d), BF16),
        grid=(pairs, b, s // rows),
        in_specs=[
            pl.BlockSpec((None, sp, LANES), lambda j, bb, i: (bb, 0, j)),
            pl.BlockSpec((None, sp, LANES), lambda j, bb, i: (bb, 0, pairs + j)),
            pl.BlockSpec((None, sp, LANES), lambda j, bb, i: (bb, 0, 2 * pairs + j)),
            pl.BlockSpec((None, A_TABLES, 2 * A_TQ, A_WIN), lambda j, bb, i: (j, 0, 0, 0)),
        ],
        out_specs=pl.BlockSpec((None, rows, LANES), lambda j, bb, i: (bb, i, j)),
        scratch_shapes=[pltpu.VMEM((2 * A_TQ, A_WIN), F32), pltpu.VMEM((2 * A_TQ, A_WIN), F32)],
        compiler_params=_params(("arbitrary", "arbitrary", "arbitrary")),
        name="mixer_a",
    )(qkv_pad, qkv_pad, qkv_pad, bias_tab)


def _rel_bias_table(rel_bias):
    qi = jnp.arange(A_TQ)[:, None]
    kj = jnp.arange(A_WIN)[None, :]
    dchunk = kj // CHUNK - qi // CHUNK
    inband = (dchunk >= 0) & (dchunk <= LEFT_CHUNKS)
    period = A_TQ + A_WIN
    u = jnp.arange(period)
    u = jnp.where(u < A_WIN, u, u - period)
    idx = jnp.clip(PAD_ROWS - u, -REL_CLIP, REL_CLIP) + REL_CLIP
    g = rel_bias.astype(F32)[:, idx] * LOG2E
    flat = jnp.tile(g, (1, A_TQ))[:, :A_TQ * (period - 1)]
    tab = flat.reshape(-1, A_TQ, period - 1)[:, :, :A_WIN]
    first_real = PAD_ROWS - A_TQ * jnp.arange(A_TABLES)[:, None, None]
    keep = inband[None] & (kj[None] >= first_real)
    tabs = jnp.where(keep[None], tab[:, None], NEG)
    tabs = tabs.reshape(-1, 2, A_TABLES, A_TQ, A_WIN).transpose(0, 2, 1, 3, 4)
    return tabs.reshape(-1, A_TABLES, 2 * A_TQ, A_WIN)


def _layer_norm(z, gain, bias):
    mu = jnp.mean(z, axis=-1, keepdims=True)
    zc = z - mu
    var = jnp.mean(zc * zc, axis=-1, keepdims=True)
    return zc * lax.rsqrt(var + LN_EPS) * gain + bias


def _small_proj_t(y, yb, w_ref, n_out):
    ylo = (y - yb.astype(F32)).astype(BF16)
    w = w_ref[...]
    wh = w.astype(BF16)
    wl = (w - wh.astype(F32)).astype(BF16)
    both = jnp.dot(yb, jnp.concatenate([wh, wl], axis=1), preferred_element_type=F32)
    acc = both[:, 0:LANES] + jnp.dot(ylo, wh, preferred_element_type=F32) + both[:, LANES:2 * LANES]
    return acc.T[:n_out, :]


def _pad_small_weight(w):
    k, n = w.shape
    return jnp.zeros((k, LANES), F32).at[:, :n].set(w.astype(F32))


def _route(lt):
    m = jnp.max(lt, axis=0, keepdims=True)
    e = jnp.exp(lt - m)
    p = e / jnp.sum(e, axis=0, keepdims=True)
    rows = [p[k:k + 1, :] for k in range(N_EXPERTS)]
    best = None
    gi = None
    for g in range(N_GROUPS):
        r = rows[EXPERTS_PER_GROUP * g:EXPERTS_PER_GROUP * (g + 1)]
        sc = None
        for a in range(EXPERTS_PER_GROUP):
            for c in range(a + 1, EXPERTS_PER_GROUP):
                pair = r[a] + r[c]
                sc = pair if sc is None else jnp.maximum(sc, pair)
        if g == 0:
            best = sc
            gi = jnp.zeros(sc.shape, jnp.int32)
        else:
            upd = sc > best
            best = jnp.where(upd, sc, best)
            gi = jnp.where(upd, g, gi)
    pin = []
    for k in range(EXPERTS_PER_GROUP):
        v = rows[(N_GROUPS - 1) * EXPERTS_PER_GROUP + k]
        for g in range(N_GROUPS - 2, -1, -1):
            v = jnp.where(gi == g, rows[g * EXPERTS_PER_GROUP + k], v)
        pin.append(v)
    v1 = pin[0]
    i1 = jnp.zeros(v1.shape, jnp.int32)
    for k in range(1, EXPERTS_PER_GROUP):
        upd = pin[k] > v1
        v1 = jnp.where(upd, pin[k], v1)
        i1 = jnp.where(upd, k, i1)
    v2 = jnp.full(v1.shape, -1.0, F32)
    i2 = jnp.zeros(v1.shape, jnp.int32)
    for k in range(EXPERTS_PER_GROUP):
        cand = jnp.where(i1 == k, -1.0, pin[k])
        upd = cand > v2
        v2 = jnp.where(upd, cand, v2)
        i2 = jnp.where(upd, k, i2)
    tot = v1 + v2
    base = gi * EXPERTS_PER_GROUP
    experts = jnp.concatenate([base + i1, base + i2], axis=0)
    gates = jnp.concatenate([v1 / tot, v2 / tot], axis=0)
    return experts, gates


def _post_attn_kernel(a0_ref, alo_ref, ahi_ref, w_ref, x_ref, g_ref, b_ref, rw_ref, rb_ref,
                      x1_ref, x1b_ref, ex_ref, gt_ref, h0_ref, h1_ref):
    t = POST_TM

    def project(a_ref, h_ref):
        h_ref[...] = jnp.dot(a_ref[...], w_ref[...], preferred_element_type=F32)

    def finish(h_ref, half):
        rows = slice(half * t, (half + 1) * t)
        y = _layer_norm(ALPHA * x_ref[rows, :] + h_ref[...], g_ref[...], b_ref[...])
        yb = y.astype(BF16)
        x1_ref[rows, :] = y
        x1b_ref[rows, :] = yb
        lt = _small_proj_t(y, yb, rw_ref, N_EXPERTS) + rb_ref[...]
        experts, gates = _route(lt)
        ex_ref[:, rows] = experts
        gt_ref[:, rows] = gates

    @pl.when(pl.program_id(0) == 0)
    def _():
        project(a0_ref, h0_ref)

    project(alo_ref, h1_ref)
    finish(h0_ref, 0)
    project(ahi_ref, h0_ref)
    finish(h1_ref, 1)


def _post_attn(attn, w_o, x, gain, bias, rw, rb):
    n, d = x.shape
    t = POST_TM
    last_tile = n // t - 1
    row = lambda i: (i, 0)
    const = lambda i: (0, 0)
    return pl.pallas_call(
        _post_attn_kernel,
        out_shape=(jax.ShapeDtypeStruct((n, d), F32),
                   jax.ShapeDtypeStruct((n, d), BF16),
                   jax.ShapeDtypeStruct((2, n), jnp.int32),
                   jax.ShapeDtypeStruct((2, n), F32)),
        grid=(n // (2 * t),),
        in_specs=[
            pl.BlockSpec((t, d), const),
            pl.BlockSpec((t, d), lambda i: (2 * i + 1, 0)),
            pl.BlockSpec((t, d), lambda i: (jnp.minimum(2 * i + 2, last_tile), 0)),
            pl.BlockSpec((d, d), const),
            pl.BlockSpec((2 * t, d), row),
            pl.BlockSpec((1, d), const),
            pl.BlockSpec((1, d), const),
            pl.BlockSpec((d, LANES), const),
            pl.BlockSpec((N_EXPERTS, 1), const),
        ],
        out_specs=(pl.BlockSpec((2 * t, d), row),
                   pl.BlockSpec((2 * t, d), row),
                   pl.BlockSpec((2, 2 * t), lambda i: (0, i)),
                   pl.BlockSpec((2, 2 * t), lambda i: (0, i))),
        scratch_shapes=[pltpu.VMEM((t, d), F32), pltpu.VMEM((t, d), F32)],
        compiler_params=_params(("arbitrary",)),
        name="out_proj_ln_router",
    )(attn, attn, attn, w_o, x, gain, bias, rw, rb)


def _moe_kernel(iblk, iex, ivalid, ifirst, starts, ends, x_ref, wg_ref, wu_ref, wd_ref, o_ref,
                acc_ref):
    w = pl.program_id(0)
    f = pl.program_id(1)

    @pl.when((ifirst[w] == 1) & (f == 0))
    def _():
        acc_ref[...] = jnp.zeros_like(acc_ref)

    @pl.when(ivalid[w] == 1)
    def _():
        e = iex[w]
        lo = starts[e]
        hi = ends[e]
        base = iblk[w] * MOE_BLK

        def swiglu(x, wg, wu, wd, keep):
            hg = jnp.dot(x, wg, preferred_element_type=F32)
            hu = jnp.dot(x, wu, preferred_element_type=F32)
            hid = hg * jax.nn.sigmoid(hg) * hu
            if keep is not None:
                hid = jnp.where(keep, hid, 0.0)
            return jnp.dot(hid.astype(BF16), wd, preferred_element_type=F32)

        whole = (lo <= base) & (hi >= base + MOE_BLK)

        @pl.when(whole)
        def _():
            acc_ref[...] += swiglu(x_ref[...], wg_ref[...].astype(BF16), wu_ref[...].astype(BF16),
                                   wd_ref[...].astype(BF16), None)

        @pl.when(jnp.logical_not(whole))
        def _():
            wg = wg_ref[...].astype(BF16)
            wu = wu_ref[...].astype(BF16)
            wd = wd_ref[...].astype(BF16)
            for c in range(MOE_BLK // MOE_CHUNK):
                r0 = base + c * MOE_CHUNK

                @pl.when((lo < r0 + MOE_CHUNK) & (hi > r0))
                def _():
                    rows = slice(c * MOE_CHUNK, (c + 1) * MOE_CHUNK)
                    row = r0 + lax.broadcasted_iota(jnp.int32, (MOE_CHUNK, 1), 0)
                    acc_ref[rows, :] += swiglu(x_ref[rows, :], wg, wu, wd, (row >= lo) & (row < hi))

        @pl.when(f == pl.num_programs(1) - 1)
        def _():
            o_ref[...] = acc_ref[...].astype(o_ref.dtype)


def _moe(xs, w_gate, w_up, w_down, layer, meta):
    a, d = xs.shape
    dff = w_gate.shape[-1]
    nf = dff // MOE_TF
    n_items = meta[0].shape[0]

    def fsel(w, f, ivalid):
        return jnp.where(ivalid[w] == 1, f, nf - 1)

    grid_spec = pltpu.PrefetchScalarGridSpec(
        num_scalar_prefetch=6,
        grid=(n_items, nf),
        in_specs=[
            pl.BlockSpec((MOE_BLK, d), lambda w, f, ib, ie, iv, i1, st, en: (ib[w], 0)),
            pl.BlockSpec((None, None, d, MOE_TF),
                         lambda w, f, ib, ie, iv, i1, st, en: (layer, ie[w], 0, fsel(w, f, iv))),
            pl.BlockSpec((None, None, d, MOE_TF),
                         lambda w, f, ib, ie, iv, i1, st, en: (layer, ie[w], 0, fsel(w, f, iv))),
            pl.BlockSpec((None, None, MOE_TF, d),
                         lambda w, f, ib, ie, iv, i1, st, en: (layer, ie[w], fsel(w, f, iv), 0)),
        ],
        out_specs=pl.BlockSpec((MOE_BLK, d), lambda w, f, ib, ie, iv, i1, st, en: (ib[w], 0)),
        scratch_shapes=[pltpu.VMEM((MOE_BLK, d), F32)],
    )
    return pl.pallas_call(
        _moe_kernel,
        out_shape=jax.ShapeDtypeStruct((a, d), BF16),
        grid_spec=grid_spec,
        compiler_params=_params(("arbitrary", "arbitrary")),
        name="moe_experts",
    )(*meta, xs, w_gate, w_up, w_down)


def _moe_plan(experts):
    n = experts.shape[1]
    a = 2 * n
    nblk = a // MOE_BLK
    n_items = nblk + N_EXPERTS - 1
    e_flat = experts.reshape(-1)
    iota = jnp.arange(a, dtype=jnp.int32)
    e_sorted, order = lax.sort((e_flat, iota), num_keys=1, is_stable=True)
    tok_sorted = order % n
    _, pos = lax.sort((order, iota), num_keys=1)
    pos = pos.reshape(2, n)
    eids = jnp.arange(N_EXPERTS, dtype=jnp.int32)
    starts = jnp.searchsorted(e_sorted, eids, side="left").astype(jnp.int32)
    ends = jnp.searchsorted(e_sorted, eids, side="right").astype(jnp.int32)
    e_lo = e_sorted[0::MOE_BLK]
    e_hi = e_sorted[MOE_BLK - 1::MOE_BLK]
    per_blk = e_hi - e_lo + 1
    iend = jnp.cumsum(per_blk).astype(jnp.int32)
    istart = iend - per_blk
    total = iend[-1]
    w = jnp.arange(n_items, dtype=jnp.int32)
    blk = jnp.clip(jnp.searchsorted(iend, w, side="right"), 0, nblk - 1).astype(jnp.int32)
    ex = e_lo[blk] + (w - istart[blk])
    valid = w < total
    blk = jnp.where(valid, blk, nblk - 1)
    ex = jnp.where(valid, ex, e_hi[nblk - 1])
    first = valid & (w == istart[blk])
    meta = (blk, ex.astype(jnp.int32), valid.astype(jnp.int32), first.astype(jnp.int32),
            starts, ends)
    return meta, tok_sorted, pos


def _combine_ln_kernel(x_ref, y0_ref, y1_ref, gt_ref, g_ref, b_ref, o_ref, ob_ref):
    gt = gt_ref[...]
    y = y0_ref[...].astype(F32) * gt[:, 0:1] + y1_ref[...].astype(F32) * gt[:, 1:2]
    out = _layer_norm(ALPHA * x_ref[...] + y, g_ref[...], b_ref[...])
    o_ref[...] = out
    ob_ref[...] = out.astype(BF16)


def _combine_ln(x, y0, y1, gates_t, gain, bias):
    n, d = x.shape
    tm = LN_TM
    row = lambda i: (i, 0)
    const = lambda i: (0, 0)
    return pl.pallas_call(
        _combine_ln_kernel,
        out_shape=(jax.ShapeDtypeStruct((n, d), F32), jax.ShapeDtypeStruct((n, d), BF16)),
        grid=(n // tm,),
        in_specs=[
            pl.BlockSpec((tm, d), row),
            pl.BlockSpec((tm, d), row),
            pl.BlockSpec((tm, d), row),
            pl.BlockSpec((tm, 2), row),
            pl.BlockSpec((1, d), const),
            pl.BlockSpec((1, d), const),
        ],
        out_specs=(pl.BlockSpec((tm, d), row), pl.BlockSpec((tm, d), row)),
        compiler_params=_params(("arbitrary",)),
        name="moe_combine_ln",
    )(x, y0, y1, gates_t, gain, bias)


def _fg_kernel(x_ref, w_ref, b_ref, tri_ref, qf_ref, kf_ref, c_ref, carry_ref, koff_ref):
    i = pl.program_id(1)

    @pl.when(i == 0)
    def _():
        carry_ref[...] = jnp.zeros_like(carry_ref)

    @pl.when(i % (FOX_TK // FOX_T) == 0)
    def _():
        koff_ref[...] = jnp.zeros_like(koff_ref)

    def split3(v):
        hi = v.astype(BF16)
        r1 = v - hi.astype(F32)
        mid = r1.astype(BF16)
        lo = (r1 - mid.astype(F32)).astype(BF16)
        return hi, mid, lo

    def columns(v, first_lane):
        hi, mid, lo = (p.astype(F32) for p in split3(v))
        return (pltpu.roll(hi, first_lane, 1) + pltpu.roll(mid, first_lane + B_HEADS, 1)
                + pltpu.roll(lo, first_lane + 2 * B_HEADS, 1))

    y = x_ref[...]
    yb = y.astype(BF16)
    ylo = (y - yb.astype(F32)).astype(BF16)
    w = w_ref[...]
    wh = w.astype(BF16)
    wl = (w - wh.astype(F32)).astype(BF16)
    z = (jnp.dot(yb, wh, preferred_element_type=F32) + jnp.dot(ylo, wh, preferred_element_type=F32)
         + jnp.dot(yb, wl, preferred_element_type=F32)) + b_ref[...]
    head_lane = lax.broadcasted_iota(jnp.int32, (FOX_T, LANES), 1) < B_HEADS
    lf = jnp.where(head_lane, (jnp.minimum(z, 0.0) - jnp.log1p(jnp.exp(-jnp.abs(z)))) * LOG2E, 0.0)
    parts = jnp.concatenate(split3(lf), axis=1)
    cs3 = jnp.dot(tri_ref[...], parts, preferred_element_type=F32)
    cs = cs3[:, 0:LANES] + cs3[:, LANES:2 * LANES] + cs3[:, 2 * LANES:3 * LANES]
    c_ref[...] = carry_ref[...]
    carry_ref[...] = carry_ref[...] + cs[FOX_T - 1:FOX_T, :]
    csk = cs + koff_ref[...]
    koff_ref[...] = koff_ref[...] + cs[FOX_T - 1:FOX_T, :]
    qf_ref[...] = columns(cs, 0).astype(BF16)
    kf_ref[...] = columns(-csk, FOX_KLANE).astype(BF16)


def _forget_columns(x, fw, fb):
    b, s, d = x.shape
    t = FOX_T
    nblk = s // t
    tri = (jnp.arange(t)[:, None] >= jnp.arange(t)[None, :]).astype(BF16)
    const = lambda bb, i: (0, 0)
    cols = jax.ShapeDtypeStruct((b, s, LANES), BF16)
    col_spec = pl.BlockSpec((None, t, LANES), lambda bb, i: (bb, i, 0))
    return pl.pallas_call(
        _fg_kernel,
        out_shape=(cols, cols, jax.ShapeDtypeStruct((b, nblk, 1, LANES), F32)),
        grid=(b, nblk),
        in_specs=[
            pl.BlockSpec((None, t, d), lambda bb, i: (bb, i, 0)),
            pl.BlockSpec((d, LANES), const),
            pl.BlockSpec((1, LANES), const),
            pl.BlockSpec((t, t), const),
        ],
        out_specs=(col_spec, col_spec,
                   pl.BlockSpec((None, None, 1, LANES), lambda bb, i: (bb, i, 0, 0))),
        scratch_shapes=[pltpu.VMEM((1, LANES), F32), pltpu.VMEM((1, LANES), F32)],
        compiler_params=_params(("arbitrary", "arbitrary")),
        name="forget_cumsum",
    )(x, fw, fb, tri)


def _fox_kernel(c_tab, q_ref, qf_ref, k_ref, kf_ref, v_ref, tri_ref, o_ref,
                qa_ref, kfh_ref, sa_ref, sb_ref, m_ref, acc_ref):
    h = pl.program_id(1)
    i = pl.program_id(2)
    tq, tk = FOX_T, FOX_TK
    per_tile = tk // tq
    lane = lax.broadcasted_iota(jnp.int32, (tq, LANES), 1)
    mine = (lane & (B_HEADS - 1)) == h
    q_side = lane < FOX_KLANE
    part_lane = (lane & (FOX_KLANE - 1)) < 3 * B_HEADS

    @pl.when(i == 0)
    def _():
        k_ones = jnp.where(mine & part_lane, 1.0, 0.0).astype(BF16)

        def fill(r, carry):
            rows = pl.ds(pl.multiple_of(r * tq, tq), tq)
            kfh_ref[rows, :] = jnp.where(q_side, k_ones, kf_ref[rows, :])
            return carry

        lax.fori_loop(0, kf_ref.shape[0] // tq, fill, 0)
    cbase = (pl.program_id(0) * B_HEADS + pl.program_id(1)) * pl.num_programs(2)
    c_q = jnp.full((1, LANES), c_tab[cbase + i], F32)
    qa_ref[:, 0:B_HEAD_DIM] = q_ref[...]
    q_ones = jnp.where(mine & part_lane, 1.0, 0.0).astype(BF16)
    qa_ref[:, B_HEAD_DIM:2 * B_HEAD_DIM] = jnp.where(q_side, qf_ref[...], q_ones)
    m_ref[...] = jnp.full(m_ref.shape, NEG, F32)
    acc_ref[...] = jnp.zeros_like(acc_ref)
    def ones_col(rows):
        return jnp.where(lax.broadcasted_iota(jnp.int32, (rows, LANES), 1) == 0, 1.0, 0.0).astype(BF16)

    def scores(jt, dst_ref):
        ks = pl.multiple_of(jt * tk, tk)
        ka = jnp.concatenate([k_ref[pl.ds(ks, tk), :], kfh_ref[pl.ds(ks, tk), :]], axis=1)
        dst_ref[...] = lax.dot_general(qa_ref[...], ka, (((1,), (1,)), ((), ())),
                                       preferred_element_type=F32)

    def absorb(src_ref, jt, own_blocks=None):
        ks = pl.multiple_of(jt * tk, tk)
        width = tk if own_blocks is None else own_blocks * tq
        if own_blocks is None:
            s = src_ref[...]
        else:
            diag = src_ref[:, width - tq:width] + tri_ref[...]
            s = diag if own_blocks == 1 else jnp.concatenate([src_ref[:, 0:width - tq], diag], axis=1)
        shift = c_q - jnp.full((1, LANES), c_tab[cbase + per_tile * jt], F32)
        m_prev = m_ref[...]
        m_new = jnp.maximum(m_prev, jnp.max(s, axis=-1, keepdims=True) + shift)
        p = jnp.exp2(s - jnp.concatenate([m_new - shift] * (width // LANES), axis=1))
        scale = jnp.exp2(m_prev - m_new)
        va = jnp.concatenate([v_ref[pl.ds(ks, width), :], ones_col(width)], axis=1)
        acc_ref[...] = jnp.concatenate([scale, scale], axis=1) * acc_ref[...] + jnp.dot(
            p.astype(BF16), va, preferred_element_type=F32)
        m_ref[...] = m_new

    def absorb_last(src_ref):
        own = i - per_tile * n_full
        for blocks in range(1, per_tile + 1):
            @pl.when(own == blocks - 1)
            def _(blocks=blocks):
                absorb(src_ref, n_full, blocks)

    n_full = i // per_tile
    scores(0, sa_ref)

    def pair(first):
        scores(first + 1, sb_ref)
        absorb(sa_ref, first)
        scores(first + 2, sa_ref)
        absorb(sb_ref, first + 1)

    def run(first, count):
        for t in range(0, count, 2):
            pair(first + t)

    done = 0
    for length in FOX_RUNS:
        @pl.when((n_full & length) != 0)
        def _(done=done, length=length):
            run(done, length)
        done = done + (n_full & length)

    @pl.when((n_full & 1) == 1)
    def _():
        scores(n_full, sb_ref)
        absorb(sa_ref, n_full - 1)
        absorb_last(sb_ref)

    @pl.when((n_full & 1) == 0)
    def _():
        absorb_last(sa_ref)

    acc = acc_ref[...]
    o_ref[...] = (acc[:, 0:B_HEAD_DIM] / acc[:, B_HEAD_DIM:B_HEAD_DIM + 1]).astype(o_ref.dtype)


def _fox(q, kv, qf, kf, c_tab):
    b, s, d = q.shape
    t = FOX_T
    hd = B_HEAD_DIM
    causal = jnp.where(jnp.arange(t)[None, :] <= jnp.arange(t)[:, None], 0.0, NEG).astype(F32)
    grid_spec = pltpu.PrefetchScalarGridSpec(
        num_scalar_prefetch=1,
        grid=(b, B_HEADS, s // t),
        in_specs=[
            pl.BlockSpec((None, t, hd), lambda bb, h, i, c: (bb, i, h)),
            pl.BlockSpec((None, t, LANES), lambda bb, h, i, c: (bb, i, 0)),
            pl.BlockSpec((None, s, hd), lambda bb, h, i, c: (bb, 0, h)),
            pl.BlockSpec((None, s, LANES), lambda bb, h, i, c: (bb, 0, 0)),
            pl.BlockSpec((None, s, hd), lambda bb, h, i, c: (bb, 0, B_HEADS + h)),
            pl.BlockSpec((t, t), lambda bb, h, i, c: (0, 0)),
        ],
        out_specs=pl.BlockSpec((None, t, hd), lambda bb, h, i, c: (bb, i, h)),
        scratch_shapes=[
            pltpu.VMEM((t, 2 * hd), BF16),
            pltpu.VMEM((s, LANES), BF16),
            pltpu.VMEM((t, FOX_TK), F32),
            pltpu.VMEM((t, FOX_TK), F32),
            pltpu.VMEM((t, LANES), F32),
            pltpu.VMEM((t, 2 * hd), F32),
        ],
    )
    return pl.pallas_call(
        _fox_kernel,
        out_shape=jax.ShapeDtypeStruct((b, s, d), BF16),
        grid_spec=grid_spec,
        compiler_params=_params(("arbitrary", "arbitrary", "arbitrary")),
        name="mixer_b",
    )(c_tab, q, qf, kv, kf, kv, causal)


def _moe_layer(x1, x1b, experts, gates, w_gate, w_up, w_down, layer, gain, bias):
    meta, tok_sorted, pos = _moe_plan(experts)
    xs = x1b[tok_sorted]
    yb = _moe(xs, w_gate, w_up, w_down, layer, meta)
    return _combine_ln(x1, yb[pos[0]], yb[pos[1]], gates.T, gain, bias)


def kernel(x, a_w_qkv, a_rel_bias, a_w_o, kv_w, fg_w, fg_b, b_w_q, b_w_o, router_w, router_b,
           moe_w_gate, moe_w_up, moe_w_down, ln_gain, ln_bias):
    b, s, d = x.shape
    n = b * s
    assert MM_TM == PAD_ROWS and s % MM_TM == 0 and s % FOX_TK == 0 and FOX_TK % FOX_T == 0
    assert s % (A_TQ * A_SUB) == 0 and PAD_ROWS % A_TQ == 0 and A_SUB % 4 == 0 and A_SUB >= 8
    assert (2 * n) % MOE_BLK == 0 and s // FOX_TK <= 2 * FOX_RUNS[0]

    rw = _pad_small_weight(router_w)
    rb = router_b.astype(F32).reshape(N_EXPERTS, 1)
    gain = ln_gain.astype(F32).reshape(DEPTH, 2, 1, d)
    bias = ln_bias.astype(F32).reshape(DEPTH, 2, 1, d)
    wg, wu, wd = moe_w_gate, moe_w_up, moe_w_down

    qscale = jnp.concatenate([jnp.full((d,), LOG2E / math.sqrt(A_HEAD_DIM), F32),
                              jnp.ones((2 * d,), F32)])
    w_qkv = (a_w_qkv[0] * qscale[None, :]).astype(BF16)
    qkv = _mm(x, w_qkv, pad_blocks=1)
    attn = _attn_a(qkv, _rel_bias_table(a_rel_bias[0]), s)
    x1, x1b, experts, gates = _post_attn(attn.reshape(n, d), a_w_o[0].astype(BF16),
                                         x.reshape(n, d), gain[0, 0], bias[0, 0], rw, rb)
    x2, x2b = _moe_layer(x1, x1b, experts, gates, wg, wu, wd, 0, gain[0, 1], bias[0, 1])

    kv = _mm(x2b.reshape(b, s, d), kv_w.astype(BF16))
    fb = jnp.zeros((1, LANES), F32).at[0, :B_HEADS].set(fg_b.astype(F32))
    qf, kf, carry_in = _forget_columns(x2.reshape(b, s, d), _pad_small_weight(fg_w), fb)
    c_tab = carry_in[:, :, 0, :B_HEADS].transpose(0, 2, 1).reshape(-1)

    w_q = (b_w_q[0] * (LOG2E / math.sqrt(B_HEAD_DIM))).astype(BF16)
    q = _mm(x2b.reshape(b, s, d), w_q)
    attn = _fox(q, kv, qf, kf, c_tab)
    x3, x3b, experts, gates = _post_attn(attn.reshape(n, d), b_w_o[0].astype(BF16), x2,
                                         gain[1, 0], bias[1, 0], rw, rb)
    x4, _ = _moe_layer(x3, x3b, experts, gates, wg, wu, wd, 1, gain[1, 1], bias[1, 1])
    return x4.reshape(b, s, d)
```

```python
import functools
import math

import jax
import jax.numpy as jnp
from jax import lax
from jax.experimental import pallas as pl
from jax.experimental.pallas import tpu as pltpu

F32 = jnp.float32
BF16 = jnp.bfloat16

CHUNK = 64
LEFT_CHUNKS = 8
A_HEADS = 32
A_HEAD_DIM = 64
REL_CLIP = 2 * CHUNK
B_HEADS = 16
B_HEAD_DIM = 128
N_EXPERTS = 16
N_GROUPS = 4
EXPERTS_PER_GROUP = 4
DEPTH = 2
ALPHA = (2.0 * DEPTH) ** 0.25
LN_EPS = 1e-5
NEG = -1e30
LOG2E = 1.4426950408889634

LANES = 128
VMEM_LIMIT = 56 * 1024 * 1024

MM_TM = 512
MM_TN = 2048
PAD_ROWS = LEFT_CHUNKS * CHUNK
A_TQ = 256
A_WIN = A_TQ + PAD_ROWS
A_SUB = 32
A_TABLES = PAD_ROWS // A_TQ + 1
POST_TM = 256
MOE_BLK = 1024
MOE_CHUNK = 256
MOE_TF = 256
LN_TM = 512
FOX_T = 512
FOX_TK = 1024
FOX_KLANE = 64
FOX_RUNS = (8, 4, 2)


def _params(sem):
    return pltpu.CompilerParams(dimension_semantics=sem, vmem_limit_bytes=VMEM_LIMIT)


def _mm_kernel(x_ref, w_ref, o_ref, *, pad_blocks):
    def compute():
        o_ref[...] = jnp.dot(x_ref[...].astype(BF16), w_ref[...],
                             preferred_element_type=F32).astype(o_ref.dtype)

    if pad_blocks:
        i = pl.program_id(2)

        @pl.when(i < pad_blocks)
        def _():
            o_ref[...] = jnp.zeros_like(o_ref)

        pl.when(i >= pad_blocks)(compute)
    else:
        compute()


def _mm(x, w, *, pad_blocks=0):
    b, s, k = x.shape
    m = w.shape[1]
    tn = min(MM_TN, m)
    grid = (m // tn, b, s // MM_TM + pad_blocks)
    return pl.pallas_call(
        functools.partial(_mm_kernel, pad_blocks=pad_blocks),
        out_shape=jax.ShapeDtypeStruct((b, s + pad_blocks * MM_TM, m), BF16),
        grid=grid,
        in_specs=[
            pl.BlockSpec((None, MM_TM, k),
                         lambda j, bb, i: (bb, jnp.maximum(i - pad_blocks, 0), 0)),
            pl.BlockSpec((k, tn), lambda j, bb, i: (0, j)),
        ],
        out_specs=pl.BlockSpec((None, MM_TM, tn), lambda j, bb, i: (bb, i, j)),
        compiler_params=_params(("arbitrary", "arbitrary", "arbitrary")),
        name="dense_proj",
    )(x, w)


def _attn_a_kernel(q_ref, k_ref, v_ref, bias_ref, o_ref, sa_ref, sb_ref):
    i = pl.program_id(2)
    head0 = lax.broadcasted_iota(jnp.int32, (A_TQ, LANES), 1) < A_HEAD_DIM
    ones_col = jnp.where(lax.broadcasted_iota(jnp.int32, (A_WIN, LANES), 1) == 0, 1.0, 0.0).astype(BF16)

    def window(sb):
        g = i * A_SUB + sb
        return g, pl.multiple_of(g * A_TQ, A_TQ)

    def scores(sb, dst_ref):
        _, start = window(sb)
        q = q_ref[pl.ds(start + PAD_ROWS, A_TQ), :]
        zero = jnp.zeros_like(q)
        q2 = jnp.concatenate([jnp.where(head0, q, zero), jnp.where(head0, zero, q)], axis=0)
        dst_ref[...] = lax.dot_general(q2, k_ref[pl.ds(start, A_WIN), :], (((1,), (1,)), ((), ())),
                                       preferred_element_type=F32)

    def absorb(src_ref, sb):
        g, start = window(sb)
        s = src_ref[...] + bias_ref[jnp.minimum(g, A_TABLES - 1)]
        p = jnp.exp2(s - jnp.max(s, axis=-1, keepdims=True))
        va = jnp.concatenate([v_ref[pl.ds(start, A_WIN), :], ones_col], axis=1)
        o = jnp.dot(p.astype(BF16), va, preferred_element_type=F32)
        o = o[:, 0:LANES] / o[:, LANES:LANES + 1]
        rows = pl.ds(pl.multiple_of(sb * A_TQ, A_TQ), A_TQ)
        o_ref[rows, :] = jnp.where(head0, o[0:A_TQ], o[A_TQ:2 * A_TQ]).astype(o_ref.dtype)

    scores(0, sa_ref)

    def pair(first):
        scores(first + 1, sb_ref)
        absorb(sa_ref, first)
        scores(first + 2, sa_ref)
        absorb(sb_ref, first + 1)

    def quad(jj, carry):
        pair(4 * jj)
        pair(4 * jj + 2)
        return carry

    lax.fori_loop(0, A_SUB // 4 - 1, quad, 0)
    pair(A_SUB - 4)
    scores(A_SUB - 1, sb_ref)
    absorb(sa_ref, A_SUB - 2)
    absorb(sb_ref, A_SUB - 1)


def _attn_a(qkv_pad, bias_tab, s):
    b = qkv_pad.shape[0]
    d = A_HEADS * A_HEAD_DIM
    pairs = d // LANES
    sp = s + PAD_ROWS
    rows = A_TQ * A_SUB
    return pl.pallas_call(
        _attn_a_kernel,
        out_shape=jax.ShapeDtypeStruct((b, s, d), BF16),
        grid=(pairs, b, s // rows),
        in_specs=[
            pl.BlockSpec((None, sp, LANES), lambda j, bb, i: (bb, 0, j)),
            pl.BlockSpec((None, sp, LANES), lambda j, bb, i: (bb, 0, pairs + j)),
            pl.BlockSpec((None, sp, LANES), lambda j, bb, i: (bb, 0, 2 * pairs + j)),
            pl.BlockSpec((None, A_TABLES, 2 * A_TQ, A_WIN), lambda j, bb, i: (j, 0, 0, 0)),
        ],
        out_specs=pl.BlockSpec((None, rows, LANES), lambda j, bb, i: (bb, i, j)),
        scratch_shapes=[pltpu.VMEM((2 * A_TQ, A_WIN), F32), pltpu.VMEM((2 * A_TQ, A_WIN), F32)],
        compiler_params=_params(("arbitrary", "arbitrary", "arbitrary")),
        name="mixer_a",
    )(qkv_pad, qkv_pad, qkv_pad, bias_tab)


def _rel_bias_table(rel_bias):
    qi = jnp.arange(A_TQ)[:, None]
    kj = jnp.arange(A_WIN)[None, :]
    dchunk = kj // CHUNK - qi // CHUNK
    inband = (dchunk >= 0) & (dchunk <= LEFT_CHUNKS)
    period = A_TQ + A_WIN
    u = jnp.arange(period)
    u = jnp.where(u < A_WIN, u, u - period)
    idx = jnp.clip(PAD_ROWS - u, -REL_CLIP, REL_CLIP) + REL_CLIP
    g = rel_bias.astype(F32)[:, idx] * LOG2E
    flat = jnp.tile(g, (1, A_TQ))[:, :A_TQ * (period - 1)]
    tab = flat.reshape(-1, A_TQ, period - 1)[:, :, :A_WIN]
    first_real = PAD_ROWS - A_TQ * jnp.arange(A_TABLES)[:, None, None]
    keep = inband[None] & (kj[None] >= first_real)
    pair_tab = tab.reshape(-1, 1, 2, A_TQ, A_WIN)
    tabs = jnp.where(keep[None, :, None], pair_tab, NEG)
    return tabs.reshape(-1, A_TABLES, 2 * A_TQ, A_WIN)


def _layer_norm(z, gain, bias):
    mu = jnp.mean(z, axis=-1, keepdims=True)
    zc = z - mu
    var = jnp.mean(zc * zc, axis=-1, keepdims=True)
    return zc * lax.rsqrt(var + LN_EPS) * gain + bias


def _small_proj_t(y, yb, w_ref, n_out):
    ylo = (y - yb.astype(F32)).astype(BF16)
    w = w_ref[...]
    wh = w.astype(BF16)
    wl = (w - wh.astype(F32)).astype(BF16)
    both = jnp.dot(yb, jnp.concatenate([wh, wl], axis=1), preferred_element_type=F32)
    acc = both[:, 0:LANES] + jnp.dot(ylo, wh, preferred_element_type=F32) + both[:, LANES:2 * LANES]
    return acc.T[:n_out, :]


def _pad_small_weight(w):
    k, n = w.shape
    return jnp.zeros((k, LANES), F32).at[:, :n].set(w.astype(F32))


def _route(lt):
    m = jnp.max(lt, axis=0, keepdims=True)
    e = jnp.exp(lt - m)
    p = e / jnp.sum(e, axis=0, keepdims=True)
    rows = [p[k:k + 1, :] for k in range(N_EXPERTS)]
    best = None
    gi = None
    for g in range(N_GROUPS):
        r = rows[EXPERTS_PER_GROUP * g:EXPERTS_PER_GROUP * (g + 1)]
        sc = None
        for a in range(EXPERTS_PER_GROUP):
            for c in range(a + 1, EXPERTS_PER_GROUP):
                pair = r[a] + r[c]
                sc = pair if sc is None else jnp.maximum(sc, pair)
        if g == 0:
            best = sc
            gi = jnp.zeros(sc.shape, jnp.int32)
        else:
            upd = sc > best
            best = jnp.where(upd, sc, best)
            gi = jnp.where(upd, g, gi)
    pin = []
    for k in range(EXPERTS_PER_GROUP):
        v = rows[(N_GROUPS - 1) * EXPERTS_PER_GROUP + k]
        for g in range(N_GROUPS - 2, -1, -1):
            v = jnp.where(gi == g, rows[g * EXPERTS_PER_GROUP + k], v)
        pin.append(v)
    v1 = pin[0]
    i1 = jnp.zeros(v1.shape, jnp.int32)
    for k in range(1, EXPERTS_PER_GROUP):
        upd = pin[k] > v1
        v1 = jnp.where(upd, pin[k], v1)
        i1 = jnp.where(upd, k, i1)
    v2 = jnp.full(v1.shape, -1.0, F32)
    i2 = jnp.zeros(v1.shape, jnp.int32)
    for k in range(EXPERTS_PER_GROUP):
        cand = jnp.where(i1 == k, -1.0, pin[k])
        upd = cand > v2
        v2 = jnp.where(upd, cand, v2)
        i2 = jnp.where(upd, k, i2)
    tot = v1 + v2
    base = gi * EXPERTS_PER_GROUP
    experts = jnp.concatenate([base + i1, base + i2], axis=0)
    gates = jnp.concatenate([v1 / tot, v2 / tot], axis=0)
    return experts, gates


def _post_attn_kernel(a0_ref, alo_ref, ahi_ref, w_ref, x_ref, g_ref, b_ref, rw_ref, rb_ref,
                      x1_ref, x1b_ref, ex_ref, gt_ref, h0_ref, h1_ref):
    t = POST_TM

    def project(a_ref, h_ref):
        h_ref[...] = jnp.dot(a_ref[...], w_ref[...], preferred_element_type=F32)

    def finish(h_ref, half):
        rows = slice(half * t, (half + 1) * t)
        y = _layer_norm(ALPHA * x_ref[rows, :] + h_ref[...], g_ref[...], b_ref[...])
        yb = y.astype(BF16)
        x1_ref[rows, :] = y
        x1b_ref[rows, :] = yb
        lt = _small_proj_t(y, yb, rw_ref, N_EXPERTS) + rb_ref[...]
        experts, gates = _route(lt)
        ex_ref[:, rows] = experts
        gt_ref[:, rows] = gates

    @pl.when(pl.program_id(0) == 0)
    def _():
        project(a0_ref, h0_ref)

    project(alo_ref, h1_ref)
    finish(h0_ref, 0)
    project(ahi_ref, h0_ref)
    finish(h1_ref, 1)


def _post_attn(attn, w_o, x, gain, bias, rw, rb):
    n, d = x.shape
    t = POST_TM
    last_tile = n // t - 1
    row = lambda i: (i, 0)
    const = lambda i: (0, 0)
    return pl.pallas_call(
        _post_attn_kernel,
        out_shape=(jax.ShapeDtypeStruct((n, d), F32),
                   jax.ShapeDtypeStruct((n, d), BF16),
                   jax.ShapeDtypeStruct((2, n), jnp.int32),
                   jax.ShapeDtypeStruct((2, n), F32)),
        grid=(n // (2 * t),),
        in_specs=[
            pl.BlockSpec((t, d), const),
            pl.BlockSpec((t, d), lambda i: (2 * i + 1, 0)),
            pl.BlockSpec((t, d), lambda i: (jnp.minimum(2 * i + 2, last_tile), 0)),
            pl.BlockSpec((d, d), const),
            pl.BlockSpec((2 * t, d), row),
            pl.BlockSpec((1, d), const),
            pl.BlockSpec((1, d), const),
            pl.BlockSpec((d, LANES), const),
            pl.BlockSpec((N_EXPERTS, 1), const),
        ],
        out_specs=(pl.BlockSpec((2 * t, d), row),
                   pl.BlockSpec((2 * t, d), row),
                   pl.BlockSpec((2, 2 * t), lambda i: (0, i)),
                   pl.BlockSpec((2, 2 * t), lambda i: (0, i))),
        scratch_shapes=[pltpu.VMEM((t, d), F32), pltpu.VMEM((t, d), F32)],
        compiler_params=_params(("arbitrary",)),
        name="out_proj_ln_router",
    )(attn, attn, attn, w_o, x, gain, bias, rw, rb)


def _moe_kernel(iblk, iex, ivalid, ifirst, starts, ends, x_ref, wg_ref, wu_ref, wd_ref, o_ref,
                acc_ref):
    w = pl.program_id(0)
    f = pl.program_id(1)

    @pl.when((ifirst[w] == 1) & (f == 0))
    def _():
        acc_ref[...] = jnp.zeros_like(acc_ref)

    @pl.when(ivalid[w] == 1)
    def _():
        e = iex[w]
        lo = starts[e]
        hi = ends[e]
        base = iblk[w] * MOE_BLK

        def swiglu(x, wg, wu, wd, keep):
            hg = jnp.dot(x, wg, preferred_element_type=F32)
            hu = jnp.dot(x, wu, preferred_element_type=F32)
            hid = hg * jax.nn.sigmoid(hg) * hu
            if keep is not None:
                hid = jnp.where(keep, hid, 0.0)
            return jnp.dot(hid.astype(BF16), wd, preferred_element_type=F32)

        whole = (lo <= base) & (hi >= base + MOE_BLK)

        @pl.when(whole)
        def _():
            acc_ref[...] += swiglu(x_ref[...], wg_ref[...].astype(BF16), wu_ref[...].astype(BF16),
                                   wd_ref[...].astype(BF16), None)

        @pl.when(jnp.logical_not(whole))
        def _():
            wg = wg_ref[...].astype(BF16)
            wu = wu_ref[...].astype(BF16)
            wd = wd_ref[...].astype(BF16)
            for c in range(MOE_BLK // MOE_CHUNK):
                r0 = base + c * MOE_CHUNK

                @pl.when((lo < r0 + MOE_CHUNK) & (hi > r0))
                def _():
                    rows = slice(c * MOE_CHUNK, (c + 1) * MOE_CHUNK)
                    row = r0 + lax.broadcasted_iota(jnp.int32, (MOE_CHUNK, 1), 0)
                    acc_ref[rows, :] += swiglu(x_ref[rows, :], wg, wu, wd, (row >= lo) & (row < hi))

        @pl.when(f == pl.num_programs(1) - 1)
        def _():
            o_ref[...] = acc_ref[...].astype(o_ref.dtype)


def _moe(xs, w_gate, w_up, w_down, layer, meta):
    a, d = xs.shape
    dff = w_gate.shape[-1]
    nf = dff // MOE_TF
    n_items = meta[0].shape[0]

    def fsel(w, f, ivalid):
        return jnp.where(ivalid[w] == 1, f, nf - 1)

    grid_spec = pltpu.PrefetchScalarGridSpec(
        num_scalar_prefetch=6,
        grid=(n_items, nf),
        in_specs=[
            pl.BlockSpec((MOE_BLK, d), lambda w, f, ib, ie, iv, i1, st, en: (ib[w], 0)),
            pl.BlockSpec((None, None, d, MOE_TF),
                         lambda w, f, ib, ie, iv, i1, st, en: (layer, ie[w], 0, fsel(w, f, iv))),
            pl.BlockSpec((None, None, d, MOE_TF),
                         lambda w, f, ib, ie, iv, i1, st, en: (layer, ie[w], 0, fsel(w, f, iv))),
            pl.BlockSpec((None, None, MOE_TF, d),
                         lambda w, f, ib, ie, iv, i1, st, en: (layer, ie[w], fsel(w, f, iv), 0)),
        ],
        out_specs=pl.BlockSpec((MOE_BLK, d), lambda w, f, ib, ie, iv, i1, st, en: (ib[w], 0)),
        scratch_shapes=[pltpu.VMEM((MOE_BLK, d), F32)],
    )
    return pl.pallas_call(
        _moe_kernel,
        out_shape=jax.ShapeDtypeStruct((a, d), BF16),
        grid_spec=grid_spec,
        compiler_params=_params(("arbitrary", "arbitrary")),
        name="moe_experts",
    )(*meta, xs, w_gate, w_up, w_down)


def _moe_plan(experts):
    n = experts.shape[1]
    a = 2 * n
    nblk = a // MOE_BLK
    n_items = nblk + N_EXPERTS - 1
    e_flat = experts.reshape(-1)
    iota = jnp.arange(a, dtype=jnp.int32)
    e_sorted, order = lax.sort((e_flat, iota), num_keys=1, is_stable=True)
    tok_sorted = order % n
    _, pos = lax.sort((order, iota), num_keys=1)
    pos = pos.reshape(2, n)
    eids = jnp.arange(N_EXPERTS, dtype=jnp.int32)
    starts = jnp.searchsorted(e_sorted, eids, side="left").astype(jnp.int32)
    ends = jnp.searchsorted(e_sorted, eids, side="right").astype(jnp.int32)
    e_lo = e_sorted[0::MOE_BLK]
    e_hi = e_sorted[MOE_BLK - 1::MOE_BLK]
    per_blk = e_hi - e_lo + 1
    iend = jnp.cumsum(per_blk).astype(jnp.int32)
    istart = iend - per_blk
    total = iend[-1]
    w = jnp.arange(n_items, dtype=jnp.int32)
    blk = jnp.clip(jnp.searchsorted(iend, w, side="right"), 0, nblk - 1).astype(jnp.int32)
    ex = e_lo[blk] + (w - istart[blk])
    valid = w < total
    blk = jnp.where(valid, blk, nblk - 1)
    ex = jnp.where(valid, ex, e_hi[nblk - 1])
    first = valid & (w == istart[blk])
    meta = (blk, ex.astype(jnp.int32), valid.astype(jnp.int32), first.astype(jnp.int32),
            starts, ends)
    return meta, tok_sorted, pos


def _combine_ln_kernel(x_ref, y0_ref, y1_ref, gt_ref, g_ref, b_ref, o_ref, *maybe_ob_ref):
    gt = gt_ref[...]
    y = y0_ref[...].astype(F32) * gt[:, 0:1] + y1_ref[...].astype(F32) * gt[:, 1:2]
    out = _layer_norm(ALPHA * x_ref[...] + y, g_ref[...], b_ref[...])
    o_ref[...] = out
    for ob_ref in maybe_ob_ref:
        ob_ref[...] = out.astype(BF16)


def _combine_ln(x, y0, y1, gates_t, gain, bias, *, with_bf16):
    n, d = x.shape
    tm = LN_TM
    row = lambda i: (i, 0)
    const = lambda i: (0, 0)
    dtypes = (F32, BF16) if with_bf16 else (F32,)
    return pl.pallas_call(
        _combine_ln_kernel,
        out_shape=tuple(jax.ShapeDtypeStruct((n, d), t) for t in dtypes),
        grid=(n // tm,),
        in_specs=[
            pl.BlockSpec((tm, d), row),
            pl.BlockSpec((tm, d), row),
            pl.BlockSpec((tm, d), row),
            pl.BlockSpec((tm, 2), row),
            pl.BlockSpec((1, d), const),
            pl.BlockSpec((1, d), const),
        ],
        out_specs=tuple(pl.BlockSpec((tm, d), row) for _ in dtypes),
        compiler_params=_params(("arbitrary",)),
        name="moe_combine_ln",
    )(x, y0, y1, gates_t, gain, bias)


def _fg_kernel(x_ref, w_ref, b_ref, tri_ref, qf_ref, kf_ref, c_ref, carry_ref, koff_ref):
    i = pl.program_id(1)

    @pl.when(i == 0)
    def _():
        carry_ref[...] = jnp.zeros_like(carry_ref)

    @pl.when(i % (FOX_TK // FOX_T) == 0)
    def _():
        koff_ref[...] = jnp.zeros_like(koff_ref)

    def split3(v):
        hi = v.astype(BF16)
        r1 = v - hi.astype(F32)
        mid = r1.astype(BF16)
        lo = (r1 - mid.astype(F32)).astype(BF16)
        return hi, mid, lo

    def columns(v, first_lane):
        hi, mid, lo = (p.astype(F32) for p in split3(v))
        return (pltpu.roll(hi, first_lane, 1) + pltpu.roll(mid, first_lane + B_HEADS, 1)
                + pltpu.roll(lo, first_lane + 2 * B_HEADS, 1))

    y = x_ref[...]
    yb = y.astype(BF16)
    ylo = (y - yb.astype(F32)).astype(BF16)
    w = w_ref[...]
    wh = w.astype(BF16)
    wl = (w - wh.astype(F32)).astype(BF16)
    z = (jnp.dot(yb, wh, preferred_element_type=F32) + jnp.dot(ylo, wh, preferred_element_type=F32)
         + jnp.dot(yb, wl, preferred_element_type=F32)) + b_ref[...]
    head_lane = lax.broadcasted_iota(jnp.int32, (FOX_T, LANES), 1) < B_HEADS
    lf = jnp.where(head_lane, (jnp.minimum(z, 0.0) - jnp.log1p(jnp.exp(-jnp.abs(z)))) * LOG2E, 0.0)
    parts = jnp.concatenate(split3(lf), axis=1)
    cs3 = jnp.dot(tri_ref[...], parts, preferred_element_type=F32)
    cs = cs3[:, 0:LANES] + cs3[:, LANES:2 * LANES] + cs3[:, 2 * LANES:3 * LANES]
    c_ref[...] = carry_ref[...]
    carry_ref[...] = carry_ref[...] + cs[FOX_T - 1:FOX_T, :]
    csk = cs + koff_ref[...]
    koff_ref[...] = koff_ref[...] + cs[FOX_T - 1:FOX_T, :]
    qf_ref[...] = columns(cs, 0).astype(BF16)
    kf_ref[...] = columns(-csk, FOX_KLANE).astype(BF16)


def _forget_columns(x, fw, fb):
    b, s, d = x.shape
    t = FOX_T
    nblk = s // t
    tri = (jnp.arange(t)[:, None] >= jnp.arange(t)[None, :]).astype(BF16)
    const = lambda bb, i: (0, 0)
    cols = jax.ShapeDtypeStruct((b, s, LANES), BF16)
    col_spec = pl.BlockSpec((None, t, LANES), lambda bb, i: (bb, i, 0))
    return pl.pallas_call(
        _fg_kernel,
        out_shape=(cols, cols, jax.ShapeDtypeStruct((b, nblk, 1, LANES), F32)),
        grid=(b, nblk),
        in_specs=[
            pl.BlockSpec((None, t, d), lambda bb, i: (bb, i, 0)),
            pl.BlockSpec((d, LANES), const),
            pl.BlockSpec((1, LANES), const),
            pl.BlockSpec((t, t), const),
        ],
        out_specs=(col_spec, col_spec,
                   pl.BlockSpec((None, None, 1, LANES), lambda bb, i: (bb, i, 0, 0))),
        scratch_shapes=[pltpu.VMEM((1, LANES), F32), pltpu.VMEM((1, LANES), F32)],
        compiler_params=_params(("arbitrary", "arbitrary")),
        name="forget_cumsum",
    )(x, fw, fb, tri)


def _fox_kernel(c_tab, q_ref, qf_ref, k_ref, kf_ref, v_ref, tri_ref, o_ref,
                qa_ref, kfh_ref, sa_ref, sb_ref, m_ref, acc_ref):
    h = pl.program_id(1)
    i = pl.program_id(2)
    tq, tk = FOX_T, FOX_TK
    per_tile = tk // tq
    lane = lax.broadcasted_iota(jnp.int32, (tq, LANES), 1)
    mine = (lane & (B_HEADS - 1)) == h
    q_side = lane < FOX_KLANE
    part_lane = (lane & (FOX_KLANE - 1)) < 3 * B_HEADS

    @pl.when(i == 0)
    def _():
        k_ones = jnp.where(mine & part_lane, 1.0, 0.0).astype(BF16)

        def fill(r, carry):
            rows = pl.ds(pl.multiple_of(r * tq, tq), tq)
            kfh_ref[rows, :] = jnp.where(q_side, k_ones, kf_ref[rows, :])
            return carry

        lax.fori_loop(0, kf_ref.shape[0] // tq, fill, 0)
    cbase = (pl.program_id(0) * B_HEADS + pl.program_id(1)) * pl.num_programs(2)
    c_q = jnp.full((1, LANES), c_tab[cbase + i], F32)
    qa_ref[:, 0:B_HEAD_DIM] = q_ref[...]
    q_ones = jnp.where(mine & part_lane, 1.0, 0.0).astype(BF16)
    qa_ref[:, B_HEAD_DIM:2 * B_HEAD_DIM] = jnp.where(q_side, qf_ref[...], q_ones)
    m_ref[...] = jnp.full(m_ref.shape, NEG, F32)
    acc_ref[...] = jnp.zeros_like(acc_ref)
    def ones_col(rows):
        return jnp.where(lax.broadcasted_iota(jnp.int32, (rows, LANES), 1) == 0, 1.0, 0.0).astype(BF16)

    def scores(jt, dst_ref):
        ks = pl.multiple_of(jt * tk, tk)
        ka = jnp.concatenate([k_ref[pl.ds(ks, tk), :], kfh_ref[pl.ds(ks, tk), :]], axis=1)
        dst_ref[...] = lax.dot_general(qa_ref[...], ka, (((1,), (1,)), ((), ())),
                                       preferred_element_type=F32)

    def absorb(src_ref, jt, own_blocks=None):
        ks = pl.multiple_of(jt * tk, tk)
        width = tk if own_blocks is None else own_blocks * tq
        if own_blocks is None:
            s = src_ref[...]
        else:
            diag = src_ref[:, width - tq:width] + tri_ref[...]
            s = diag if own_blocks == 1 else jnp.concatenate([src_ref[:, 0:width - tq], diag], axis=1)
        shift = c_q - jnp.full((1, LANES), c_tab[cbase + per_tile * jt], F32)
        m_prev = m_ref[...]
        m_new = jnp.maximum(m_prev, jnp.max(s, axis=-1, keepdims=True) + shift)
        p = jnp.exp2(s - jnp.concatenate([m_new - shift] * (width // LANES), axis=1))
        scale = jnp.exp2(m_prev - m_new)
        va = jnp.concatenate([v_ref[pl.ds(ks, width), :], ones_col(width)], axis=1)
        acc_ref[...] = jnp.concatenate([scale, scale], axis=1) * acc_ref[...] + jnp.dot(
            p.astype(BF16), va, preferred_element_type=F32)
        m_ref[...] = m_new

    def absorb_last(src_ref):
        own = i - per_tile * n_full
        for blocks in range(1, per_tile + 1):
            @pl.when(own == blocks - 1)
            def _(blocks=blocks):
                absorb(src_ref, n_full, blocks)

    n_full = i // per_tile
    scores(0, sa_ref)

    def pair(first):
        scores(first + 1, sb_ref)
        absorb(sa_ref, first)
        scores(first + 2, sa_ref)
        absorb(sb_ref, first + 1)

    def run(first, count):
        for t in range(0, count, 2):
            pair(first + t)

    done = 0
    for length in FOX_RUNS:
        @pl.when((n_full & length) != 0)
        def _(done=done, length=length):
            run(done, length)
        done = done + (n_full & length)

    @pl.when((n_full & 1) == 1)
    def _():
        scores(n_full, sb_ref)
        absorb(sa_ref, n_full - 1)
        absorb_last(sb_ref)

    @pl.when((n_full & 1) == 0)
    def _():
        absorb_last(sa_ref)

    acc = acc_ref[...]
    o_ref[...] = (acc[:, 0:B_HEAD_DIM] / acc[:, B_HEAD_DIM:B_HEAD_DIM + 1]).astype(o_ref.dtype)


def _fox(q, kv, qf, kf, c_tab):
    b, s, d = q.shape
    t = FOX_T
    hd = B_HEAD_DIM
    causal = jnp.where(jnp.arange(t)[None, :] <= jnp.arange(t)[:, None], 0.0, NEG).astype(F32)
    grid_spec = pltpu.PrefetchScalarGridSpec(
        num_scalar_prefetch=1,
        grid=(b, B_HEADS, s // t),
        in_specs=[
            pl.BlockSpec((None, t, hd), lambda bb, h, i, c: (bb, i, h)),
            pl.BlockSpec((None, t, LANES), lambda bb, h, i, c: (bb, i, 0)),
            pl.BlockSpec((None, s, hd), lambda bb, h, i, c: (bb, 0, h)),
            pl.BlockSpec((None, s, LANES), lambda bb, h, i, c: (bb, 0, 0)),
            pl.BlockSpec((None, s, hd), lambda bb, h, i, c: (bb, 0, B_HEADS + h)),
            pl.BlockSpec((t, t), lambda bb, h, i, c: (0, 0)),
        ],
        out_specs=pl.BlockSpec((None, t, hd), lambda bb, h, i, c: (bb, i, h)),
        scratch_shapes=[
            pltpu.VMEM((t, 2 * hd), BF16),
            pltpu.VMEM((s, LANES), BF16),
            pltpu.VMEM((t, FOX_TK), F32),
            pltpu.VMEM((t, FOX_TK), F32),
            pltpu.VMEM((t, LANES), F32),
            pltpu.VMEM((t, 2 * hd), F32),
        ],
    )
    return pl.pallas_call(
        _fox_kernel,
        out_shape=jax.ShapeDtypeStruct((b, s, d), BF16),
        grid_spec=grid_spec,
        compiler_params=_params(("arbitrary", "arbitrary", "arbitrary")),
        name="mixer_b",
    )(c_tab, q, qf, kv, kf, kv, causal)


def _moe_layer(x1, x1b, experts, gates, w_gate, w_up, w_down, layer, gain, bias):
    meta, tok_sorted, pos = _moe_plan(experts)
    xs = x1b[tok_sorted]
    yb = _moe(xs, w_gate, w_up, w_down, layer, meta)
    return _combine_ln(x1, yb[pos[0]], yb[pos[1]], gates.T, gain, bias, with_bf16=layer < DEPTH - 1)


def kernel(x, a_w_qkv, a_rel_bias, a_w_o, kv_w, fg_w, fg_b, b_w_q, b_w_o, router_w, router_b,
           moe_w_gate, moe_w_up, moe_w_down, ln_gain, ln_bias):
    b, s, d = x.shape
    n = b * s
    assert MM_TM == PAD_ROWS and s % MM_TM == 0 and s % FOX_TK == 0 and FOX_TK % FOX_T == 0
    assert s % (A_TQ * A_SUB) == 0 and PAD_ROWS % A_TQ == 0 and A_SUB % 4 == 0 and A_SUB >= 8
    assert (2 * n) % MOE_BLK == 0 and s // FOX_TK <= 2 * FOX_RUNS[0]

    rw = _pad_small_weight(router_w)
    rb = router_b.astype(F32).reshape(N_EXPERTS, 1)
    gain = ln_gain.astype(F32).reshape(DEPTH, 2, 1, d)
    bias = ln_bias.astype(F32).reshape(DEPTH, 2, 1, d)
    wg, wu, wd = moe_w_gate, moe_w_up, moe_w_down

    qscale = jnp.concatenate([jnp.full((d,), LOG2E / math.sqrt(A_HEAD_DIM), F32),
                              jnp.ones((2 * d,), F32)])
    w_qkv = (a_w_qkv[0] * qscale[None, :]).astype(BF16)
    qkv = _mm(x, w_qkv, pad_blocks=1)
    attn = _attn_a(qkv, _rel_bias_table(a_rel_bias[0]), s)
    x1, x1b, experts, gates = _post_attn(attn.reshape(n, d), a_w_o[0].astype(BF16),
                                         x.reshape(n, d), gain[0, 0], bias[0, 0], rw, rb)
    x2, x2b = _moe_layer(x1, x1b, experts, gates, wg, wu, wd, 0, gain[0, 1], bias[0, 1])

    kv = _mm(x2b.reshape(b, s, d), kv_w.astype(BF16))
    fb = jnp.zeros((1, LANES), F32).at[0, :B_HEADS].set(fg_b.astype(F32))
    qf, kf, carry_in = _forget_columns(x2.reshape(b, s, d), _pad_small_weight(fg_w), fb)
    c_tab = carry_in[:, :, 0, :B_HEADS].transpose(0, 2, 1).reshape(-1)

    w_q = (b_w_q[0] * (LOG2E / math.sqrt(B_HEAD_DIM))).astype(BF16)
    q = _mm(x2b.reshape(b, s, d), w_q)
    attn = _fox(q, kv, qf, kf, c_tab)
    x3, x3b, experts, gates = _post_attn(attn.reshape(n, d), b_w_o[0].astype(BF16), x2,
                                         gain[1, 0], bias[1, 0], rw, rb)
    (x4,) = _moe_layer(x3, x3b, experts, gates, wg, wu, wd, 1, gain[1, 1], bias[1, 1])
    return x4.reshape(b, s, d)
```

```python
import functools
import math

import jax
import jax.numpy as jnp
from jax import lax
from jax.experimental import pallas as pl
from jax.experimental.pallas import tpu as pltpu

F32 = jnp.float32
BF16 = jnp.bfloat16

CHUNK = 64
LEFT_CHUNKS = 8
A_HEADS = 32
A_HEAD_DIM = 64
REL_CLIP = 2 * CHUNK
B_HEADS = 16
B_HEAD_DIM = 128
N_EXPERTS = 16
N_GROUPS = 4
EXPERTS_PER_GROUP = 4
DEPTH = 2
ALPHA = (2.0 * DEPTH) ** 0.25
LN_EPS = 1e-5
NEG = -1e30
LOG2E = 1.4426950408889634

LANES = 128
VMEM_LIMIT = 56 * 1024 * 1024

MM_TM = 512
MM_TN = 2048
PAD_ROWS = LEFT_CHUNKS * CHUNK
A_TQ = 256
A_WIN = A_TQ + PAD_ROWS
A_SUB = 32
A_TABLES = PAD_ROWS // A_TQ + 1
POST_TM = 256
MOE_BLK = 1024
MOE_CHUNK = 256
MOE_TF = 256
LN_TM = 512
FOX_T = 512
FOX_TK = 2048
FOX_KLANE = 64
FOX_RUNS = (4, 2)


def _params(sem):
    return pltpu.CompilerParams(dimension_semantics=sem, vmem_limit_bytes=VMEM_LIMIT)


def _mm_kernel(x_ref, w_ref, o_ref, *, pad_blocks):
    def compute():
        o_ref[...] = jnp.dot(x_ref[...].astype(BF16), w_ref[...],
                             preferred_element_type=F32).astype(o_ref.dtype)

    if pad_blocks:
        i = pl.program_id(2)

        @pl.when(i < pad_blocks)
        def _():
            o_ref[...] = jnp.zeros_like(o_ref)

        pl.when(i >= pad_blocks)(compute)
    else:
        compute()


def _mm(x, w, *, pad_blocks=0):
    b, s, k = x.shape
    m = w.shape[1]
    tn = min(MM_TN, m)
    grid = (m // tn, b, s // MM_TM + pad_blocks)
    return pl.pallas_call(
        functools.partial(_mm_kernel, pad_blocks=pad_blocks),
        out_shape=jax.ShapeDtypeStruct((b, s + pad_blocks * MM_TM, m), BF16),
        grid=grid,
        in_specs=[
            pl.BlockSpec((None, MM_TM, k),
                         lambda j, bb, i: (bb, jnp.maximum(i - pad_blocks, 0), 0)),
            pl.BlockSpec((k, tn), lambda j, bb, i: (0, j)),
        ],
        out_specs=pl.BlockSpec((None, MM_TM, tn), lambda j, bb, i: (bb, i, j)),
        compiler_params=_params(("arbitrary", "arbitrary", "arbitrary")),
        name="dense_proj",
    )(x, w)


def _attn_a_kernel(q_ref, k_ref, v_ref, bias_ref, o_ref, sa_ref, sb_ref):
    i = pl.program_id(2)
    head0 = lax.broadcasted_iota(jnp.int32, (A_TQ, LANES), 1) < A_HEAD_DIM
    ones_col = jnp.where(lax.broadcasted_iota(jnp.int32, (A_WIN, LANES), 1) == 0, 1.0, 0.0).astype(BF16)

    def window(sb):
        g = i * A_SUB + sb
        return g, pl.multiple_of(g * A_TQ, A_TQ)

    def scores(sb, dst_ref):
        _, start = window(sb)
        q = q_ref[pl.ds(start + PAD_ROWS, A_TQ), :]
        zero = jnp.zeros_like(q)
        q2 = jnp.concatenate([jnp.where(head0, q, zero), jnp.where(head0, zero, q)], axis=0)
        dst_ref[...] = lax.dot_general(q2, k_ref[pl.ds(start, A_WIN), :], (((1,), (1,)), ((), ())),
                                       preferred_element_type=F32)

    def absorb(src_ref, sb):
        g, start = window(sb)
        s = src_ref[...] + bias_ref[jnp.minimum(g, A_TABLES - 1)]
        p = jnp.exp2(s - jnp.max(s, axis=-1, keepdims=True))
        va = jnp.concatenate([v_ref[pl.ds(start, A_WIN), :], ones_col], axis=1)
        o = jnp.dot(p.astype(BF16), va, preferred_element_type=F32)
        o = o[:, 0:LANES] / o[:, LANES:LANES + 1]
        rows = pl.ds(pl.multiple_of(sb * A_TQ, A_TQ), A_TQ)
        o_ref[rows, :] = jnp.where(head0, o[0:A_TQ], o[A_TQ:2 * A_TQ]).astype(o_ref.dtype)

    scores(0, sa_ref)

    def pair(first):
        scores(first + 1, sb_ref)
        absorb(sa_ref, first)
        scores(first + 2, sa_ref)
        absorb(sb_ref, first + 1)

    def quad(jj, carry):
        pair(4 * jj)
        pair(4 * jj + 2)
        return carry

    lax.fori_loop(0, A_SUB // 4 - 1, quad, 0)
    pair(A_SUB - 4)
    scores(A_SUB - 1, sb_ref)
    absorb(sa_ref, A_SUB - 2)
    absorb(sb_ref, A_SUB - 1)


def _attn_a(qkv_pad, bias_tab, s):
    b = qkv_pad.shape[0]
    d = A_HEADS * A_HEAD_DIM
    pairs = d // LANES
    sp = s + PAD_ROWS
    rows = A_TQ * A_SUB
    return pl.pallas_call(
        _attn_a_kernel,
        out_shape=jax.ShapeDtypeStruct((b, s, d), BF16),
        grid=(pairs, b, s // rows),
        in_specs=[
            pl.BlockSpec((None, sp, LANES), lambda j, bb, i: (bb, 0, j)),
            pl.BlockSpec((None, sp, LANES), lambda j, bb, i: (bb, 0, pairs + j)),
            pl.BlockSpec((None, sp, LANES), lambda j, bb, i: (bb, 0, 2 * pairs + j)),
            pl.BlockSpec((None, A_TABLES, 2 * A_TQ, A_WIN), lambda j, bb, i: (j, 0, 0, 0)),
        ],
        out_specs=pl.BlockSpec((None, rows, LANES), lambda j, bb, i: (bb, i, j)),
        scratch_shapes=[pltpu.VMEM((2 * A_TQ, A_WIN), F32), pltpu.VMEM((2 * A_TQ, A_WIN), F32)],
        compiler_params=_params(("arbitrary", "arbitrary", "arbitrary")),
        name="mixer_a",
    )(qkv_pad, qkv_pad, qkv_pad, bias_tab)


def _rel_bias_table(rel_bias):
    qi = jnp.arange(A_TQ)[:, None]
    kj = jnp.arange(A_WIN)[None, :]
    dchunk = kj // CHUNK - qi // CHUNK
    inband = (dchunk >= 0) & (dchunk <= LEFT_CHUNKS)
    period = A_TQ + A_WIN
    u = jnp.arange(period)
    u = jnp.where(u < A_WIN, u, u - period)
    idx = jnp.clip(PAD_ROWS - u, -REL_CLIP, REL_CLIP) + REL_CLIP
    g = rel_bias.astype(F32)[:, idx] * LOG2E
    flat = jnp.tile(g, (1, A_TQ))[:, :A_TQ * (period - 1)]
    tab = flat.reshape(-1, A_TQ, period - 1)[:, :, :A_WIN]
    first_real = PAD_ROWS - A_TQ * jnp.arange(A_TABLES)[:, None, None]
    keep = inband[None] & (kj[None] >= first_real)
    pair_tab = tab.reshape(-1, 1, 2, A_TQ, A_WIN)
    tabs = jnp.where(keep[None, :, None], pair_tab, NEG)
    return tabs.reshape(-1, A_TABLES, 2 * A_TQ, A_WIN)


def _layer_norm(z, gain, bias):
    mu = jnp.mean(z, axis=-1, keepdims=True)
    zc = z - mu
    var = jnp.mean(zc * zc, axis=-1, keepdims=True)
    return zc * lax.rsqrt(var + LN_EPS) * gain + bias


def _small_proj_t(y, yb, w_ref, n_out):
    ylo = (y - yb.astype(F32)).astype(BF16)
    w = w_ref[...]
    wh = w.astype(BF16)
    wl = (w - wh.astype(F32)).astype(BF16)
    both = jnp.dot(yb, jnp.concatenate([wh, wl], axis=1), preferred_element_type=F32)
    acc = both[:, 0:LANES] + jnp.dot(ylo, wh, preferred_element_type=F32) + both[:, LANES:2 * LANES]
    return acc.T[:n_out, :]


def _pad_small_weight(w):
    k, n = w.shape
    return jnp.zeros((k, LANES), F32).at[:, :n].set(w.astype(F32))


def _route(lt):
    m = jnp.max(lt, axis=0, keepdims=True)
    e = jnp.exp(lt - m)
    p = e / jnp.sum(e, axis=0, keepdims=True)
    rows = [p[k:k + 1, :] for k in range(N_EXPERTS)]
    best = None
    gi = None
    for g in range(N_GROUPS):
        r = rows[EXPERTS_PER_GROUP * g:EXPERTS_PER_GROUP * (g + 1)]
        sc = None
        for a in range(EXPERTS_PER_GROUP):
            for c in range(a + 1, EXPERTS_PER_GROUP):
                pair = r[a] + r[c]
                sc = pair if sc is None else jnp.maximum(sc, pair)
        if g == 0:
            best = sc
            gi = jnp.zeros(sc.shape, jnp.int32)
        else:
            upd = sc > best
            best = jnp.where(upd, sc, best)
            gi = jnp.where(upd, g, gi)
    pin = []
    for k in range(EXPERTS_PER_GROUP):
        v = rows[(N_GROUPS - 1) * EXPERTS_PER_GROUP + k]
        for g in range(N_GROUPS - 2, -1, -1):
            v = jnp.where(gi == g, rows[g * EXPERTS_PER_GROUP + k], v)
        pin.append(v)
    v1 = pin[0]
    i1 = jnp.zeros(v1.shape, jnp.int32)
    for k in range(1, EXPERTS_PER_GROUP):
        upd = pin[k] > v1
        v1 = jnp.where(upd, pin[k], v1)
        i1 = jnp.where(upd, k, i1)
    v2 = jnp.full(v1.shape, -1.0, F32)
    i2 = jnp.zeros(v1.shape, jnp.int32)
    for k in range(EXPERTS_PER_GROUP):
        cand = jnp.where(i1 == k, -1.0, pin[k])
        upd = cand > v2
        v2 = jnp.where(upd, cand, v2)
        i2 = jnp.where(upd, k, i2)
    tot = v1 + v2
    base = gi * EXPERTS_PER_GROUP
    experts = jnp.concatenate([base + i1, base + i2], axis=0)
    gates = jnp.concatenate([v1 / tot, v2 / tot], axis=0)
    return experts, gates


def _post_attn_kernel(a0_ref, alo_ref, ahi_ref, w_ref, x_ref, g_ref, b_ref, rw_ref, rb_ref,
                      x1_ref, x1b_ref, ex_ref, gt_ref, h0_ref, h1_ref):
    t = POST_TM

    def project(a_ref, h_ref):
        h_ref[...] = jnp.dot(a_ref[...], w_ref[...], preferred_element_type=F32)

    def finish(h_ref, half):
        rows = slice(half * t, (half + 1) * t)
        y = _layer_norm(ALPHA * x_ref[rows, :] + h_ref[...], g_ref[...], b_ref[...])
        yb = y.astype(BF16)
        x1_ref[rows, :] = y
        x1b_ref[rows, :] = yb
        lt = _small_proj_t(y, yb, rw_ref, N_EXPERTS) + rb_ref[...]
        experts, gates = _route(lt)
        ex_ref[:, rows] = experts
        gt_ref[:, rows] = gates

    @pl.when(pl.program_id(0) == 0)
    def _():
        project(a0_ref, h0_ref)

    project(alo_ref, h1_ref)
    finish(h0_ref, 0)
    project(ahi_ref, h0_ref)
    finish(h1_ref, 1)


def _post_attn(attn, w_o, x, gain, bias, rw, rb):
    n, d = x.shape
    t = POST_TM
    last_tile = n // t - 1
    row = lambda i: (i, 0)
    const = lambda i: (0, 0)
    return pl.pallas_call(
        _post_attn_kernel,
        out_shape=(jax.ShapeDtypeStruct((n, d), F32),
                   jax.ShapeDtypeStruct((n, d), BF16),
                   jax.ShapeDtypeStruct((2, n), jnp.int32),
                   jax.ShapeDtypeStruct((2, n), F32)),
        grid=(n // (2 * t),),
        in_specs=[
            pl.BlockSpec((t, d), const),
            pl.BlockSpec((t, d), lambda i: (2 * i + 1, 0)),
            pl.BlockSpec((t, d), lambda i: (jnp.minimum(2 * i + 2, last_tile), 0)),
            pl.BlockSpec((d, d), const),
            pl.BlockSpec((2 * t, d), row),
            pl.BlockSpec((1, d), const),
            pl.BlockSpec((1, d), const),
            pl.BlockSpec((d, LANES), const),
            pl.BlockSpec((N_EXPERTS, 1), const),
        ],
        out_specs=(pl.BlockSpec((2 * t, d), row),
                   pl.BlockSpec((2 * t, d), row),
                   pl.BlockSpec((2, 2 * t), lambda i: (0, i)),
                   pl.BlockSpec((2, 2 * t), lambda i: (0, i))),
        scratch_shapes=[pltpu.VMEM((t, d), F32), pltpu.VMEM((t, d), F32)],
        compiler_params=_params(("arbitrary",)),
        name="out_proj_ln_router",
    )(attn, attn, attn, w_o, x, gain, bias, rw, rb)


def _moe_kernel(iblk, iex, ivalid, ifirst, starts, ends, x_ref, wg_ref, wu_ref, wd_ref, o_ref,
                acc_ref):
    w = pl.program_id(0)
    f = pl.program_id(1)

    @pl.when((ifirst[w] == 1) & (f == 0))
    def _():
        acc_ref[...] = jnp.zeros_like(acc_ref)

    @pl.when(ivalid[w] == 1)
    def _():
        e = iex[w]
        lo = starts[e]
        hi = ends[e]
        base = iblk[w] * MOE_BLK

        def swiglu(x, wg, wu, wd, keep):
            hg = jnp.dot(x, wg, preferred_element_type=F32)
            hu = jnp.dot(x, wu, preferred_element_type=F32)
            hid = hg * jax.nn.sigmoid(hg) * hu
            if keep is not None:
                hid = jnp.where(keep, hid, 0.0)
            return jnp.dot(hid.astype(BF16), wd, preferred_element_type=F32)

        whole = (lo <= base) & (hi >= base + MOE_BLK)

        @pl.when(whole)
        def _():
            acc_ref[...] += swiglu(x_ref[...], wg_ref[...].astype(BF16), wu_ref[...].astype(BF16),
                                   wd_ref[...].astype(BF16), None)

        @pl.when(jnp.logical_not(whole))
        def _():
            wg = wg_ref[...].astype(BF16)
            wu = wu_ref[...].astype(BF16)
            wd = wd_ref[...].astype(BF16)
            for c in range(MOE_BLK // MOE_CHUNK):
                r0 = base + c * MOE_CHUNK

                @pl.when((lo < r0 + MOE_CHUNK) & (hi > r0))
                def _():
                    rows = slice(c * MOE_CHUNK, (c + 1) * MOE_CHUNK)
                    row = r0 + lax.broadcasted_iota(jnp.int32, (MOE_CHUNK, 1), 0)
                    acc_ref[rows, :] += swiglu(x_ref[rows, :], wg, wu, wd, (row >= lo) & (row < hi))

        @pl.when(f == pl.num_programs(1) - 1)
        def _():
            o_ref[...] = acc_ref[...].astype(o_ref.dtype)


def _moe(xs, w_gate, w_up, w_down, layer, meta):
    a, d = xs.shape
    dff = w_gate.shape[-1]
    nf = dff // MOE_TF
    n_items = meta[0].shape[0]

    def fsel(w, f, ivalid):
        return jnp.where(ivalid[w] == 1, f, nf - 1)

    grid_spec = pltpu.PrefetchScalarGridSpec(
        num_scalar_prefetch=6,
        grid=(n_items, nf),
        in_specs=[
            pl.BlockSpec((MOE_BLK, d), lambda w, f, ib, ie, iv, i1, st, en: (ib[w], 0)),
            pl.BlockSpec((None, None, d, MOE_TF),
                         lambda w, f, ib, ie, iv, i1, st, en: (layer, ie[w], 0, fsel(w, f, iv))),
            pl.BlockSpec((None, None, d, MOE_TF),
                         lambda w, f, ib, ie, iv, i1, st, en: (layer, ie[w], 0, fsel(w, f, iv))),
            pl.BlockSpec((None, None, MOE_TF, d),
                         lambda w, f, ib, ie, iv, i1, st, en: (layer, ie[w], fsel(w, f, iv), 0)),
        ],
        out_specs=pl.BlockSpec((MOE_BLK, d), lambda w, f, ib, ie, iv, i1, st, en: (ib[w], 0)),
        scratch_shapes=[pltpu.VMEM((MOE_BLK, d), F32)],
    )
    return pl.pallas_call(
        _moe_kernel,
        out_shape=jax.ShapeDtypeStruct((a, d), BF16),
        grid_spec=grid_spec,
        compiler_params=_params(("arbitrary", "arbitrary")),
        name="moe_experts",
    )(*meta, xs, w_gate, w_up, w_down)


def _moe_plan(experts):
    n = experts.shape[1]
    a = 2 * n
    nblk = a // MOE_BLK
    n_items = nblk + N_EXPERTS - 1
    e_flat = experts.reshape(-1)
    iota = jnp.arange(a, dtype=jnp.int32)
    e_sorted, order = lax.sort((e_flat, iota), num_keys=1, is_stable=True)
    tok_sorted = order % n
    _, pos = lax.sort((order, iota), num_keys=1)
    pos = pos.reshape(2, n)
    eids = jnp.arange(N_EXPERTS, dtype=jnp.int32)
    starts = jnp.searchsorted(e_sorted, eids, side="left").astype(jnp.int32)
    ends = jnp.searchsorted(e_sorted, eids, side="right").astype(jnp.int32)
    e_lo = e_sorted[0::MOE_BLK]
    e_hi = e_sorted[MOE_BLK - 1::MOE_BLK]
    per_blk = e_hi - e_lo + 1
    iend = jnp.cumsum(per_blk).astype(jnp.int32)
    istart = iend - per_blk
    total = iend[-1]
    w = jnp.arange(n_items, dtype=jnp.int32)
    blk = jnp.clip(jnp.searchsorted(iend, w, side="right"), 0, nblk - 1).astype(jnp.int32)
    ex = e_lo[blk] + (w - istart[blk])
    valid = w < total
    blk = jnp.where(valid, blk, nblk - 1)
    ex = jnp.where(valid, ex, e_hi[nblk - 1])
    first = valid & (w == istart[blk])
    meta = (blk, ex.astype(jnp.int32), valid.astype(jnp.int32), first.astype(jnp.int32),
            starts, ends)
    return meta, tok_sorted, pos


def _combine_ln_kernel(x_ref, y0_ref, y1_ref, gt_ref, g_ref, b_ref, o_ref, *maybe_ob_ref):
    gt = gt_ref[...]
    y = y0_ref[...].astype(F32) * gt[:, 0:1] + y1_ref[...].astype(F32) * gt[:, 1:2]
    out = _layer_norm(ALPHA * x_ref[...] + y, g_ref[...], b_ref[...])
    o_ref[...] = out
    for ob_ref in maybe_ob_ref:
        ob_ref[...] = out.astype(BF16)


def _combine_ln(x, y0, y1, gates_t, gain, bias, *, with_bf16):
    n, d = x.shape
    tm = LN_TM
    row = lambda i: (i, 0)
    const = lambda i: (0, 0)
    dtypes = (F32, BF16) if with_bf16 else (F32,)
    return pl.pallas_call(
        _combine_ln_kernel,
        out_shape=tuple(jax.ShapeDtypeStruct((n, d), t) for t in dtypes),
        grid=(n // tm,),
        in_specs=[
            pl.BlockSpec((tm, d), row),
            pl.BlockSpec((tm, d), row),
            pl.BlockSpec((tm, d), row),
            pl.BlockSpec((tm, 2), row),
            pl.BlockSpec((1, d), const),
            pl.BlockSpec((1, d), const),
        ],
        out_specs=tuple(pl.BlockSpec((tm, d), row) for _ in dtypes),
        compiler_params=_params(("arbitrary",)),
        name="moe_combine_ln",
    )(x, y0, y1, gates_t, gain, bias)


def _fg_kernel(x_ref, w_ref, b_ref, tri_ref, qf_ref, kf_ref, c_ref, carry_ref, koff_ref):
    i = pl.program_id(1)

    @pl.when(i == 0)
    def _():
        carry_ref[...] = jnp.zeros_like(carry_ref)

    @pl.when(i % (FOX_TK // FOX_T) == 0)
    def _():
        koff_ref[...] = jnp.zeros_like(koff_ref)

    def split3(v):
        hi = v.astype(BF16)
        r1 = v - hi.astype(F32)
        mid = r1.astype(BF16)
        lo = (r1 - mid.astype(F32)).astype(BF16)
        return hi, mid, lo

    def columns(v, first_lane):
        hi, mid, lo = (p.astype(F32) for p in split3(v))
        return (pltpu.roll(hi, first_lane, 1) + pltpu.roll(mid, first_lane + B_HEADS, 1)
                + pltpu.roll(lo, first_lane + 2 * B_HEADS, 1))

    y = x_ref[...]
    yb = y.astype(BF16)
    ylo = (y - yb.astype(F32)).astype(BF16)
    w = w_ref[...]
    wh = w.astype(BF16)
    wl = (w - wh.astype(F32)).astype(BF16)
    z = (jnp.dot(yb, wh, preferred_element_type=F32) + jnp.dot(ylo, wh, preferred_element_type=F32)
         + jnp.dot(yb, wl, preferred_element_type=F32)) + b_ref[...]
    head_lane = lax.broadcasted_iota(jnp.int32, (FOX_T, LANES), 1) < B_HEADS
    lf = jnp.where(head_lane, (jnp.minimum(z, 0.0) - jnp.log1p(jnp.exp(-jnp.abs(z)))) * LOG2E, 0.0)
    parts = jnp.concatenate(split3(lf), axis=1)
    cs3 = jnp.dot(tri_ref[...], parts, preferred_element_type=F32)
    cs = cs3[:, 0:LANES] + cs3[:, LANES:2 * LANES] + cs3[:, 2 * LANES:3 * LANES]
    c_ref[...] = carry_ref[...]
    carry_ref[...] = carry_ref[...] + cs[FOX_T - 1:FOX_T, :]
    csk = cs + koff_ref[...]
    koff_ref[...] = koff_ref[...] + cs[FOX_T - 1:FOX_T, :]
    qf_ref[...] = columns(cs, 0).astype(BF16)
    kf_ref[...] = columns(-csk, FOX_KLANE).astype(BF16)


def _forget_columns(x, fw, fb):
    b, s, d = x.shape
    t = FOX_T
    nblk = s // t
    tri = (jnp.arange(t)[:, None] >= jnp.arange(t)[None, :]).astype(BF16)
    const = lambda bb, i: (0, 0)
    cols = jax.ShapeDtypeStruct((b, s, LANES), BF16)
    col_spec = pl.BlockSpec((None, t, LANES), lambda bb, i: (bb, i, 0))
    return pl.pallas_call(
        _fg_kernel,
        out_shape=(cols, cols, jax.ShapeDtypeStruct((b, nblk, 1, LANES), F32)),
        grid=(b, nblk),
        in_specs=[
            pl.BlockSpec((None, t, d), lambda bb, i: (bb, i, 0)),
            pl.BlockSpec((d, LANES), const),
            pl.BlockSpec((1, LANES), const),
            pl.BlockSpec((t, t), const),
        ],
        out_specs=(col_spec, col_spec,
                   pl.BlockSpec((None, None, 1, LANES), lambda bb, i: (bb, i, 0, 0))),
        scratch_shapes=[pltpu.VMEM((1, LANES), F32), pltpu.VMEM((1, LANES), F32)],
        compiler_params=_params(("arbitrary", "arbitrary")),
        name="forget_cumsum",
    )(x, fw, fb, tri)


def _fox_kernel(c_tab, q_ref, qf_ref, k_ref, kf_ref, v_ref, tri_ref, o_ref,
                qa_ref, kfh_ref, sa_ref, sb_ref, m_ref, acc_ref):
    h = pl.program_id(1)
    i = pl.program_id(2)
    tq, tk = FOX_T, FOX_TK
    per_tile = tk // tq
    lane = lax.broadcasted_iota(jnp.int32, (tq, LANES), 1)
    mine = (lane & (B_HEADS - 1)) == h
    q_side = lane < FOX_KLANE
    part_lane = (lane & (FOX_KLANE - 1)) < 3 * B_HEADS

    @pl.when(i == 0)
    def _():
        k_ones = jnp.where(mine & part_lane, 1.0, 0.0).astype(BF16)

        def fill(r, carry):
            rows = pl.ds(pl.multiple_of(r * tq, tq), tq)
            kfh_ref[rows, :] = jnp.where(q_side, k_ones, kf_ref[rows, :])
            return carry

        lax.fori_loop(0, kf_ref.shape[0] // tq, fill, 0)
    cbase = (pl.program_id(0) * B_HEADS + pl.program_id(1)) * pl.num_programs(2)
    c_q = jnp.full((1, LANES), c_tab[cbase + i], F32)
    qa_ref[:, 0:B_HEAD_DIM] = q_ref[...]
    q_ones = jnp.where(mine & part_lane, 1.0, 0.0).astype(BF16)
    qa_ref[:, B_HEAD_DIM:2 * B_HEAD_DIM] = jnp.where(q_side, qf_ref[...], q_ones)
    m_ref[...] = jnp.full(m_ref.shape, NEG, F32)
    acc_ref[...] = jnp.zeros_like(acc_ref)
    def ones_col(rows):
        return jnp.where(lax.broadcasted_iota(jnp.int32, (rows, LANES), 1) == 0, 1.0, 0.0).astype(BF16)

    def scores(jt, dst_ref):
        ks = pl.multiple_of(jt * tk, tk)
        ka = jnp.concatenate([k_ref[pl.ds(ks, tk), :], kfh_ref[pl.ds(ks, tk), :]], axis=1)
        dst_ref[...] = lax.dot_general(qa_ref[...], ka, (((1,), (1,)), ((), ())),
                                       preferred_element_type=F32)

    def absorb(src_ref, jt, own_blocks=None):
        ks = pl.multiple_of(jt * tk, tk)
        width = tk if own_blocks is None else own_blocks * tq
        if own_blocks is None:
            s = src_ref[...]
        else:
            diag = src_ref[:, width - tq:width] + tri_ref[...]
            s = diag if own_blocks == 1 else jnp.concatenate([src_ref[:, 0:width - tq], diag], axis=1)
        shift = c_q - jnp.full((1, LANES), c_tab[cbase + per_tile * jt], F32)
        m_prev = m_ref[...]
        m_new = jnp.maximum(m_prev, jnp.max(s, axis=-1, keepdims=True) + shift)
        p = jnp.exp2(s - jnp.concatenate([m_new - shift] * (width // LANES), axis=1))
        scale = jnp.exp2(m_prev - m_new)
        va = jnp.concatenate([v_ref[pl.ds(ks, width), :], ones_col(width)], axis=1)
        acc_ref[...] = jnp.concatenate([scale, scale], axis=1) * acc_ref[...] + jnp.dot(
            p.astype(BF16), va, preferred_element_type=F32)
        m_ref[...] = m_new

    def absorb_last(src_ref):
        own = i - per_tile * n_full
        for blocks in range(1, per_tile + 1):
            @pl.when(own == blocks - 1)
            def _(blocks=blocks):
                absorb(src_ref, n_full, blocks)

    n_full = i // per_tile
    scores(0, sa_ref)

    def pair(first):
        scores(first + 1, sb_ref)
        absorb(sa_ref, first)
        scores(first + 2, sa_ref)
        absorb(sb_ref, first + 1)

    def run(first, count):
        for t in range(0, count, 2):
            pair(first + t)

    done = 0
    for length in FOX_RUNS:
        @pl.when((n_full & length) != 0)
        def _(done=done, length=length):
            run(done, length)
        done = done + (n_full & length)

    @pl.when((n_full & 1) == 1)
    def _():
        scores(n_full, sb_ref)
        absorb(sa_ref, n_full - 1)
        absorb_last(sb_ref)

    @pl.when((n_full & 1) == 0)
    def _():
        absorb_last(sa_ref)

    acc = acc_ref[...]
    o_ref[...] = (acc[:, 0:B_HEAD_DIM] / acc[:, B_HEAD_DIM:B_HEAD_DIM + 1]).astype(o_ref.dtype)


def _fox(q, kv, qf, kf, c_tab):
    b, s, d = q.shape
    t = FOX_T
    hd = B_HEAD_DIM
    causal = jnp.where(jnp.arange(t)[None, :] <= jnp.arange(t)[:, None], 0.0, NEG).astype(F32)
    grid_spec = pltpu.PrefetchScalarGridSpec(
        num_scalar_prefetch=1,
        grid=(b, B_HEADS, s // t),
        in_specs=[
            pl.BlockSpec((None, t, hd), lambda bb, h, i, c: (bb, i, h)),
            pl.BlockSpec((None, t, LANES), lambda bb, h, i, c: (bb, i, 0)),
            pl.BlockSpec((None, s, hd), lambda bb, h, i, c: (bb, 0, h)),
            pl.BlockSpec((None, s, LANES), lambda bb, h, i, c: (bb, 0, 0)),
            pl.BlockSpec((None, s, hd), lambda bb, h, i, c: (bb, 0, B_HEADS + h)),
            pl.BlockSpec((t, t), lambda bb, h, i, c: (0, 0)),
        ],
        out_specs=pl.BlockSpec((None, t, hd), lambda bb, h, i, c: (bb, i, h)),
        scratch_shapes=[
            pltpu.VMEM((t, 2 * hd), BF16),
            pltpu.VMEM((s, LANES), BF16),
            pltpu.VMEM((t, FOX_TK), F32),
            pltpu.VMEM((t, FOX_TK), F32),
            pltpu.VMEM((t, LANES), F32),
            pltpu.VMEM((t, 2 * hd), F32),
        ],
    )
    return pl.pallas_call(
        _fox_kernel,
        out_shape=jax.ShapeDtypeStruct((b, s, d), BF16),
        grid_spec=grid_spec,
        compiler_params=_params(("arbitrary", "arbitrary", "arbitrary")),
        name="mixer_b",
    )(c_tab, q, qf, kv, kf, kv, causal)


def _moe_layer(x1, x1b, experts, gates, w_gate, w_up, w_down, layer, gain, bias):
    meta, tok_sorted, pos = _moe_plan(experts)
    xs = x1b[tok_sorted]
    yb = _moe(xs, w_gate, w_up, w_down, layer, meta)
    return _combine_ln(x1, yb[pos[0]], yb[pos[1]], gates.T, gain, bias, with_bf16=layer < DEPTH - 1)


def kernel(x, a_w_qkv, a_rel_bias, a_w_o, kv_w, fg_w, fg_b, b_w_q, b_w_o, router_w, router_b,
           moe_w_gate, moe_w_up, moe_w_down, ln_gain, ln_bias):
    b, s, d = x.shape
    n = b * s
    assert MM_TM == PAD_ROWS and s % MM_TM == 0 and s % FOX_TK == 0 and FOX_TK % FOX_T == 0
    assert s % (A_TQ * A_SUB) == 0 and PAD_ROWS % A_TQ == 0 and A_SUB % 4 == 0 and A_SUB >= 8
    assert (2 * n) % MOE_BLK == 0 and s // FOX_TK <= 2 * FOX_RUNS[0]

    rw = _pad_small_weight(router_w)
    rb = router_b.astype(F32).reshape(N_EXPERTS, 1)
    gain = ln_gain.astype(F32).reshape(DEPTH, 2, 1, d)
    bias = ln_bias.astype(F32).reshape(DEPTH, 2, 1, d)
    wg, wu, wd = moe_w_gate, moe_w_up, moe_w_down

    qscale = jnp.concatenate([jnp.full((d,), LOG2E / math.sqrt(A_HEAD_DIM), F32),
                              jnp.ones((2 * d,), F32)])
    w_qkv = (a_w_qkv[0] * qscale[None, :]).astype(BF16)
    qkv = _mm(x, w_qkv, pad_blocks=1)
    attn = _attn_a(qkv, _rel_bias_table(a_rel_bias[0]), s)
    x1, x1b, experts, gates = _post_attn(attn.reshape(n, d), a_w_o[0].astype(BF16),
                                         x.reshape(n, d), gain[0, 0], bias[0, 0], rw, rb)
    x2, x2b = _moe_layer(x1, x1b, experts, gates, wg, wu, wd, 0, gain[0, 1], bias[0, 1])

    kv = _mm(x2b.reshape(b, s, d), kv_w.astype(BF16))
    fb = jnp.zeros((1, LANES), F32).at[0, :B_HEADS].set(fg_b.astype(F32))
    qf, kf, carry_in = _forget_columns(x2.reshape(b, s, d), _pad_small_weight(fg_w), fb)
    c_tab = carry_in[:, :, 0, :B_HEADS].transpose(0, 2, 1).reshape(-1)

    w_q = (b_w_q[0] * (LOG2E / math.sqrt(B_HEAD_DIM))).astype(BF16)
    q = _mm(x2b.reshape(b, s, d), w_q)
    attn = _fox(q, kv, qf, kf, c_tab)
    x3, x3b, experts, gates = _post_attn(attn.reshape(n, d), b_w_o[0].astype(BF16), x2,
                                         gain[1, 0], bias[1, 0], rw, rb)
    (x4,) = _moe_layer(x3, x3b, experts, gates, wg, wu, wd, 1, gain[1, 1], bias[1, 1])
    return x4.reshape(b, s, d)
```

```python
import functools
import math

import jax
import jax.numpy as jnp
from jax import lax
from jax.experimental import pallas as pl
from jax.experimental.pallas import tpu as pltpu

F32 = jnp.float32
BF16 = jnp.bfloat16

CHUNK = 64
LEFT_CHUNKS = 8
A_HEADS = 32
A_HEAD_DIM = 64
REL_CLIP = 2 * CHUNK
B_HEADS = 16
B_HEAD_DIM = 128
N_EXPERTS = 16
N_GROUPS = 4
EXPERTS_PER_GROUP = 4
DEPTH = 2
ALPHA = (2.0 * DEPTH) ** 0.25
LN_EPS = 1e-5
NEG = -1e30
LOG2E = 1.4426950408889634

LANES = 128
VMEM_LIMIT = 56 * 1024 * 1024

MM_TM = 512
MM_TN = 2048
PAD_ROWS = LEFT_CHUNKS * CHUNK
A_TQ = 256
A_WIN = A_TQ + PAD_ROWS
A_SUB = 32
A_UNROLL = 8
A_TABLES = PAD_ROWS // A_TQ + 1
POST_TM = 256
MOE_BLK = 1024
MOE_CHUNK = 256
MOE_TF = 256
LN_TM = 512
FOX_T = 512
FOX_TK = 1024
FOX_KLANE = 64
FOX_RUNS = (8, 4, 2)


def _params(sem):
    return pltpu.CompilerParams(dimension_semantics=sem, vmem_limit_bytes=VMEM_LIMIT)


def _mm_kernel(x_ref, w_ref, o_ref, *, pad_blocks):
    def compute():
        o_ref[...] = jnp.dot(x_ref[...].astype(BF16), w_ref[...],
                             preferred_element_type=F32).astype(o_ref.dtype)

    if pad_blocks:
        i = pl.program_id(2)

        @pl.when(i < pad_blocks)
        def _():
            o_ref[...] = jnp.zeros_like(o_ref)

        pl.when(i >= pad_blocks)(compute)
    else:
        compute()


def _mm(x, w, *, pad_blocks=0):
    b, s, k = x.shape
    m = w.shape[1]
    tn = min(MM_TN, m)
    grid = (m // tn, b, s // MM_TM + pad_blocks)
    return pl.pallas_call(
        functools.partial(_mm_kernel, pad_blocks=pad_blocks),
        out_shape=jax.ShapeDtypeStruct((b, s + pad_blocks * MM_TM, m), BF16),
        grid=grid,
        in_specs=[
            pl.BlockSpec((None, MM_TM, k),
                         lambda j, bb, i: (bb, jnp.maximum(i - pad_blocks, 0), 0)),
            pl.BlockSpec((k, tn), lambda j, bb, i: (0, j)),
        ],
        out_specs=pl.BlockSpec((None, MM_TM, tn), lambda j, bb, i: (bb, i, j)),
        compiler_params=_params(("arbitrary", "arbitrary", "arbitrary")),
        name="dense_proj",
    )(x, w)


def _attn_a_kernel(q_ref, k_ref, v_ref, bias_ref, o_ref, sa_ref, sb_ref):
    i = pl.program_id(2)
    head0 = lax.broadcasted_iota(jnp.int32, (A_TQ, LANES), 1) < A_HEAD_DIM
    ones_col = jnp.where(lax.broadcasted_iota(jnp.int32, (A_WIN, LANES), 1) == 0, 1.0, 0.0).astype(BF16)

    def window(sb):
        g = i * A_SUB + sb
        return g, pl.multiple_of(g * A_TQ, A_TQ)

    def scores(sb, dst_ref):
        _, start = window(sb)
        q = q_ref[pl.ds(start + PAD_ROWS, A_TQ), :]
        zero = jnp.zeros_like(q)
        q2 = jnp.concatenate([jnp.where(head0, q, zero), jnp.where(head0, zero, q)], axis=0)
        dst_ref[...] = lax.dot_general(q2, k_ref[pl.ds(start, A_WIN), :], (((1,), (1,)), ((), ())),
                                       preferred_element_type=F32)

    def absorb(src_ref, sb):
        g, start = window(sb)
        s = src_ref[...] + bias_ref[jnp.minimum(g, A_TABLES - 1)]
        p = jnp.exp2(s - jnp.max(s, axis=-1, keepdims=True))
        va = jnp.concatenate([v_ref[pl.ds(start, A_WIN), :], ones_col], axis=1)
        o = jnp.dot(p.astype(BF16), va, preferred_element_type=F32)
        o = o[:, 0:LANES] / o[:, LANES:LANES + 1]
        rows = pl.ds(pl.multiple_of(sb * A_TQ, A_TQ), A_TQ)
        o_ref[rows, :] = jnp.where(head0, o[0:A_TQ], o[A_TQ:2 * A_TQ]).astype(o_ref.dtype)

    scores(0, sa_ref)

    def pair(first):
        scores(first + 1, sb_ref)
        absorb(sa_ref, first)
        scores(first + 2, sa_ref)
        absorb(sb_ref, first + 1)

    def octet(jj, carry):
        for t in range(0, A_UNROLL, 2):
            pair(A_UNROLL * jj + t)
        return carry

    lax.fori_loop(0, A_SUB // A_UNROLL - 1, octet, 0)
    for t in range(A_SUB - A_UNROLL, A_SUB - 2, 2):
        pair(t)
    scores(A_SUB - 1, sb_ref)
    absorb(sa_ref, A_SUB - 2)
    absorb(sb_ref, A_SUB - 1)


def _attn_a(qkv_pad, bias_tab, s):
    b = qkv_pad.shape[0]
    d = A_HEADS * A_HEAD_DIM
    pairs = d // LANES
    sp = s + PAD_ROWS
    rows = A_TQ * A_SUB
    return pl.pallas_call(
        _attn_a_kernel,
        out_shape=jax.ShapeDtypeStruct((b, s, d), BF16),
        grid=(pairs, b, s // rows),
        in_specs=[
            pl.BlockSpec((None, sp, LANES), lambda j, bb, i: (bb, 0, j)),
            pl.BlockSpec((None, sp, LANES), lambda j, bb, i: (bb, 0, pairs + j)),
            pl.BlockSpec((None, sp, LANES), lambda j, bb, i: (bb, 0, 2 * pairs + j)),
            pl.BlockSpec((None, A_TABLES, 2 * A_TQ, A_WIN), lambda j, bb, i: (j, 0, 0, 0)),
        ],
        out_specs=pl.BlockSpec((None, rows, LANES), lambda j, bb, i: (bb, i, j)),
        scratch_shapes=[pltpu.VMEM((2 * A_TQ, A_WIN), F32), pltpu.VMEM((2 * A_TQ, A_WIN), F32)],
        compiler_params=_params(("arbitrary", "arbitrary", "arbitrary")),
        name="mixer_a",
    )(qkv_pad, qkv_pad, qkv_pad, bias_tab)


def _rel_bias_table(rel_bias):
    qi = jnp.arange(A_TQ)[:, None]
    kj = jnp.arange(A_WIN)[None, :]
    dchunk = kj // CHUNK - qi // CHUNK
    inband = (dchunk >= 0) & (dchunk <= LEFT_CHUNKS)
    period = A_TQ + A_WIN
    u = jnp.arange(period)
    u = jnp.where(u < A_WIN, u, u - period)
    idx = jnp.clip(PAD_ROWS - u, -REL_CLIP, REL_CLIP) + REL_CLIP
    g = rel_bias.astype(F32)[:, idx] * LOG2E
    flat = jnp.tile(g, (1, A_TQ))[:, :A_TQ * (period - 1)]
    tab = flat.reshape(-1, A_TQ, period - 1)[:, :, :A_WIN]
    first_real = PAD_ROWS - A_TQ * jnp.arange(A_TABLES)[:, None, None]
    keep = inband[None] & (kj[None] >= first_real)
    pair_tab = tab.reshape(-1, 1, 2, A_TQ, A_WIN)
    tabs = jnp.where(keep[None, :, None], pair_tab, NEG)
    return tabs.reshape(-1, A_TABLES, 2 * A_TQ, A_WIN)


def _layer_norm(z, gain, bias):
    mu = jnp.mean(z, axis=-1, keepdims=True)
    zc = z - mu
    var = jnp.mean(zc * zc, axis=-1, keepdims=True)
    return zc * lax.rsqrt(var + LN_EPS) * gain + bias


def _small_proj_t(y, yb, w_ref, n_out):
    ylo = (y - yb.astype(F32)).astype(BF16)
    w = w_ref[...]
    wh = w.astype(BF16)
    wl = (w - wh.astype(F32)).astype(BF16)
    both = jnp.dot(yb, jnp.concatenate([wh, wl], axis=1), preferred_element_type=F32)
    acc = both[:, 0:LANES] + jnp.dot(ylo, wh, preferred_element_type=F32) + both[:, LANES:2 * LANES]
    return acc.T[:n_out, :]


def _pad_small_weight(w):
    k, n = w.shape
    return jnp.zeros((k, LANES), F32).at[:, :n].set(w.astype(F32))


def _route(lt):
    m = jnp.max(lt, axis=0, keepdims=True)
    e = jnp.exp(lt - m)
    p = e / jnp.sum(e, axis=0, keepdims=True)
    rows = [p[k:k + 1, :] for k in range(N_EXPERTS)]
    best = None
    gi = None
    for g in range(N_GROUPS):
        r = rows[EXPERTS_PER_GROUP * g:EXPERTS_PER_GROUP * (g + 1)]
        sc = None
        for a in range(EXPERTS_PER_GROUP):
            for c in range(a + 1, EXPERTS_PER_GROUP):
                pair = r[a] + r[c]
                sc = pair if sc is None else jnp.maximum(sc, pair)
        if g == 0:
            best = sc
            gi = jnp.zeros(sc.shape, jnp.int32)
        else:
            upd = sc > best
            best = jnp.where(upd, sc, best)
            gi = jnp.where(upd, g, gi)
    pin = []
    for k in range(EXPERTS_PER_GROUP):
        v = rows[(N_GROUPS - 1) * EXPERTS_PER_GROUP + k]
        for g in range(N_GROUPS - 2, -1, -1):
            v = jnp.where(gi == g, rows[g * EXPERTS_PER_GROUP + k], v)
        pin.append(v)
    v1 = pin[0]
    i1 = jnp.zeros(v1.shape, jnp.int32)
    for k in range(1, EXPERTS_PER_GROUP):
        upd = pin[k] > v1
        v1 = jnp.where(upd, pin[k], v1)
        i1 = jnp.where(upd, k, i1)
    v2 = jnp.full(v1.shape, -1.0, F32)
    i2 = jnp.zeros(v1.shape, jnp.int32)
    for k in range(EXPERTS_PER_GROUP):
        cand = jnp.where(i1 == k, -1.0, pin[k])
        upd = cand > v2
        v2 = jnp.where(upd, cand, v2)
        i2 = jnp.where(upd, k, i2)
    tot = v1 + v2
    base = gi * EXPERTS_PER_GROUP
    experts = jnp.concatenate([base + i1, base + i2], axis=0)
    gates = jnp.concatenate([v1 / tot, v2 / tot], axis=0)
    return experts, gates


def _post_attn_kernel(a0_ref, alo_ref, ahi_ref, w_ref, x_ref, g_ref, b_ref, rw_ref, rb_ref,
                      x1_ref, x1b_ref, ex_ref, gt_ref, h0_ref, h1_ref):
    t = POST_TM

    def project(a_ref, h_ref):
        h_ref[...] = jnp.dot(a_ref[...], w_ref[...], preferred_element_type=F32)

    def finish(h_ref, half):
        rows = slice(half * t, (half + 1) * t)
        y = _layer_norm(ALPHA * x_ref[rows, :] + h_ref[...], g_ref[...], b_ref[...])
        yb = y.astype(BF16)
        x1_ref[rows, :] = y
        x1b_ref[rows, :] = yb
        lt = _small_proj_t(y, yb, rw_ref, N_EXPERTS) + rb_ref[...]
        experts, gates = _route(lt)
        ex_ref[:, rows] = experts
        gt_ref[:, rows] = gates

    @pl.when(pl.program_id(0) == 0)
    def _():
        project(a0_ref, h0_ref)

    project(alo_ref, h1_ref)
    finish(h0_ref, 0)
    project(ahi_ref, h0_ref)
    finish(h1_ref, 1)


def _post_attn(attn, w_o, x, gain, bias, rw, rb):
    n, d = x.shape
    t = POST_TM
    last_tile = n // t - 1
    row = lambda i: (i, 0)
    const = lambda i: (0, 0)
    return pl.pallas_call(
        _post_attn_kernel,
        out_shape=(jax.ShapeDtypeStruct((n, d), F32),
                   jax.ShapeDtypeStruct((n, d), BF16),
                   jax.ShapeDtypeStruct((2, n), jnp.int32),
                   jax.ShapeDtypeStruct((2, n), F32)),
        grid=(n // (2 * t),),
        in_specs=[
            pl.BlockSpec((t, d), const),
            pl.BlockSpec((t, d), lambda i: (2 * i + 1, 0)),
            pl.BlockSpec((t, d), lambda i: (jnp.minimum(2 * i + 2, last_tile), 0)),
            pl.BlockSpec((d, d), const),
            pl.BlockSpec((2 * t, d), row),
            pl.BlockSpec((1, d), const),
            pl.BlockSpec((1, d), const),
            pl.BlockSpec((d, LANES), const),
            pl.BlockSpec((N_EXPERTS, 1), const),
        ],
        out_specs=(pl.BlockSpec((2 * t, d), row),
                   pl.BlockSpec((2 * t, d), row),
                   pl.BlockSpec((2, 2 * t), lambda i: (0, i)),
                   pl.BlockSpec((2, 2 * t), lambda i: (0, i))),
        scratch_shapes=[pltpu.VMEM((t, d), F32), pltpu.VMEM((t, d), F32)],
        compiler_params=_params(("arbitrary",)),
        name="out_proj_ln_router",
    )(attn, attn, attn, w_o, x, gain, bias, rw, rb)


def _moe_kernel(iblk, iex, ivalid, ifirst, starts, ends, x_ref, wg_ref, wu_ref, wd_ref, o_ref,
                acc_ref):
    w = pl.program_id(0)
    f = pl.program_id(1)

    @pl.when((ifirst[w] == 1) & (f == 0))
    def _():
        acc_ref[...] = jnp.zeros_like(acc_ref)

    @pl.when(ivalid[w] == 1)
    def _():
        e = iex[w]
        lo = starts[e]
        hi = ends[e]
        base = iblk[w] * MOE_BLK

        def swiglu(x, wg, wu, wd, keep):
            hg = jnp.dot(x, wg, preferred_element_type=F32)
            hu = jnp.dot(x, wu, preferred_element_type=F32)
            hid = hg * jax.nn.sigmoid(hg) * hu
            if keep is not None:
                hid = jnp.where(keep, hid, 0.0)
            return jnp.dot(hid.astype(BF16), wd, preferred_element_type=F32)

        whole = (lo <= base) & (hi >= base + MOE_BLK)

        @pl.when(whole)
        def _():
            acc_ref[...] += swiglu(x_ref[...], wg_ref[...].astype(BF16), wu_ref[...].astype(BF16),
                                   wd_ref[...].astype(BF16), None)

        @pl.when(jnp.logical_not(whole))
        def _():
            wg = wg_ref[...].astype(BF16)
            wu = wu_ref[...].astype(BF16)
            wd = wd_ref[...].astype(BF16)
            for c in range(MOE_BLK // MOE_CHUNK):
                r0 = base + c * MOE_CHUNK

                @pl.when((lo < r0 + MOE_CHUNK) & (hi > r0))
                def _():
                    rows = slice(c * MOE_CHUNK, (c + 1) * MOE_CHUNK)
                    row = r0 + lax.broadcasted_iota(jnp.int32, (MOE_CHUNK, 1), 0)
                    acc_ref[rows, :] += swiglu(x_ref[rows, :], wg, wu, wd, (row >= lo) & (row < hi))

        @pl.when(f == pl.num_programs(1) - 1)
        def _():
            o_ref[...] = acc_ref[...].astype(o_ref.dtype)


def _moe(xs, w_gate, w_up, w_down, layer, meta):
    a, d = xs.shape
    dff = w_gate.shape[-1]
    nf = dff // MOE_TF
    n_items = meta[0].shape[0]

    def fsel(w, f, ivalid):
        return jnp.where(ivalid[w] == 1, f, nf - 1)

    grid_spec = pltpu.PrefetchScalarGridSpec(
        num_scalar_prefetch=6,
        grid=(n_items, nf),
        in_specs=[
            pl.BlockSpec((MOE_BLK, d), lambda w, f, ib, ie, iv, i1, st, en: (ib[w], 0)),
            pl.BlockSpec((None, None, d, MOE_TF),
                         lambda w, f, ib, ie, iv, i1, st, en: (layer, ie[w], 0, fsel(w, f, iv))),
            pl.BlockSpec((None, None, d, MOE_TF),
                         lambda w, f, ib, ie, iv, i1, st, en: (layer, ie[w], 0, fsel(w, f, iv))),
            pl.BlockSpec((None, None, MOE_TF, d),
                         lambda w, f, ib, ie, iv, i1, st, en: (layer, ie[w], fsel(w, f, iv), 0)),
        ],
        out_specs=pl.BlockSpec((MOE_BLK, d), lambda w, f, ib, ie, iv, i1, st, en: (ib[w], 0)),
        scratch_shapes=[pltpu.VMEM((MOE_BLK, d), F32)],
    )
    return pl.pallas_call(
        _moe_kernel,
        out_shape=jax.ShapeDtypeStruct((a, d), BF16),
        grid_spec=grid_spec,
        compiler_params=_params(("arbitrary", "arbitrary")),
        name="moe_experts",
    )(*meta, xs, w_gate, w_up, w_down)


def _moe_plan(experts):
    n = experts.shape[1]
    a = 2 * n
    nblk = a // MOE_BLK
    n_items = nblk + N_EXPERTS - 1
    e_flat = experts.reshape(-1)
    iota = jnp.arange(a, dtype=jnp.int32)
    e_sorted, order = lax.sort((e_flat, iota), num_keys=1, is_stable=True)
    tok_sorted = order % n
    _, pos = lax.sort((order, iota), num_keys=1)
    pos = pos.reshape(2, n)
    eids = jnp.arange(N_EXPERTS, dtype=jnp.int32)
    starts = jnp.searchsorted(e_sorted, eids, side="left").astype(jnp.int32)
    ends = jnp.searchsorted(e_sorted, eids, side="right").astype(jnp.int32)
    e_lo = e_sorted[0::MOE_BLK]
    e_hi = e_sorted[MOE_BLK - 1::MOE_BLK]
    per_blk = e_hi - e_lo + 1
    iend = jnp.cumsum(per_blk).astype(jnp.int32)
    istart = iend - per_blk
    total = iend[-1]
    w = jnp.arange(n_items, dtype=jnp.int32)
    blk = jnp.clip(jnp.searchsorted(iend, w, side="right"), 0, nblk - 1).astype(jnp.int32)
    ex = e_lo[blk] + (w - istart[blk])
    valid = w < total
    blk = jnp.where(valid, blk, nblk - 1)
    ex = jnp.where(valid, ex, e_hi[nblk - 1])
    first = valid & (w == istart[blk])
    meta = (blk, ex.astype(jnp.int32), valid.astype(jnp.int32), first.astype(jnp.int32),
            starts, ends)
    return meta, tok_sorted, pos


def _combine_ln_kernel(x_ref, y0_ref, y1_ref, gt_ref, g_ref, b_ref, o_ref, *maybe_ob_ref):
    gt = gt_ref[...]
    y = y0_ref[...].astype(F32) * gt[:, 0:1] + y1_ref[...].astype(F32) * gt[:, 1:2]
    out = _layer_norm(ALPHA * x_ref[...] + y, g_ref[...], b_ref[...])
    o_ref[...] = out
    for ob_ref in maybe_ob_ref:
        ob_ref[...] = out.astype(BF16)


def _combine_ln(x, y0, y1, gates_t, gain, bias, *, with_bf16):
    n, d = x.shape
    tm = LN_TM
    row = lambda i: (i, 0)
    const = lambda i: (0, 0)
    dtypes = (F32, BF16) if with_bf16 else (F32,)
    return pl.pallas_call(
        _combine_ln_kernel,
        out_shape=tuple(jax.ShapeDtypeStruct((n, d), t) for t in dtypes),
        grid=(n // tm,),
        in_specs=[
            pl.BlockSpec((tm, d), row),
            pl.BlockSpec((tm, d), row),
            pl.BlockSpec((tm, d), row),
            pl.BlockSpec((tm, 2), row),
            pl.BlockSpec((1, d), const),
            pl.BlockSpec((1, d), const),
        ],
        out_specs=tuple(pl.BlockSpec((tm, d), row) for _ in dtypes),
        compiler_params=_params(("arbitrary",)),
        name="moe_combine_ln",
    )(x, y0, y1, gates_t, gain, bias)


def _fg_kernel(x_ref, w_ref, b_ref, tri_ref, qf_ref, kf_ref, c_ref, carry_ref, koff_ref):
    i = pl.program_id(1)

    @pl.when(i == 0)
    def _():
        carry_ref[...] = jnp.zeros_like(carry_ref)

    @pl.when(i % (FOX_TK // FOX_T) == 0)
    def _():
        koff_ref[...] = jnp.zeros_like(koff_ref)

    def split3(v):
        hi = v.astype(BF16)
        r1 = v - hi.astype(F32)
        mid = r1.astype(BF16)
        lo = (r1 - mid.astype(F32)).astype(BF16)
        return hi, mid, lo

    def columns(v, first_lane):
        hi, mid, lo = (p.astype(F32) for p in split3(v))
        return (pltpu.roll(hi, first_lane, 1) + pltpu.roll(mid, first_lane + B_HEADS, 1)
                + pltpu.roll(lo, first_lane + 2 * B_HEADS, 1))

    y = x_ref[...]
    yb = y.astype(BF16)
    ylo = (y - yb.astype(F32)).astype(BF16)
    w = w_ref[...]
    wh = w.astype(BF16)
    wl = (w - wh.astype(F32)).astype(BF16)
    z = (jnp.dot(yb, wh, preferred_element_type=F32) + jnp.dot(ylo, wh, preferred_element_type=F32)
         + jnp.dot(yb, wl, preferred_element_type=F32)) + b_ref[...]
    head_lane = lax.broadcasted_iota(jnp.int32, (FOX_T, LANES), 1) < B_HEADS
    lf = jnp.where(head_lane, (jnp.minimum(z, 0.0) - jnp.log1p(jnp.exp(-jnp.abs(z)))) * LOG2E, 0.0)
    parts = jnp.concatenate(split3(lf), axis=1)
    cs3 = jnp.dot(tri_ref[...], parts, preferred_element_type=F32)
    cs = cs3[:, 0:LANES] + cs3[:, LANES:2 * LANES] + cs3[:, 2 * LANES:3 * LANES]
    c_ref[...] = carry_ref[...]
    carry_ref[...] = carry_ref[...] + cs[FOX_T - 1:FOX_T, :]
    csk = cs + koff_ref[...]
    koff_ref[...] = koff_ref[...] + cs[FOX_T - 1:FOX_T, :]
    qf_ref[...] = columns(cs, 0).astype(BF16)
    kf_ref[...] = columns(-csk, FOX_KLANE).astype(BF16)


def _forget_columns(x, fw, fb):
    b, s, d = x.shape
    t = FOX_T
    nblk = s // t
    tri = (jnp.arange(t)[:, None] >= jnp.arange(t)[None, :]).astype(BF16)
    const = lambda bb, i: (0, 0)
    cols = jax.ShapeDtypeStruct((b, s, LANES), BF16)
    col_spec = pl.BlockSpec((None, t, LANES), lambda bb, i: (bb, i, 0))
    return pl.pallas_call(
        _fg_kernel,
        out_shape=(cols, cols, jax.ShapeDtypeStruct((b, nblk, 1, LANES), F32)),
        grid=(b, nblk),
        in_specs=[
            pl.BlockSpec((None, t, d), lambda bb, i: (bb, i, 0)),
            pl.BlockSpec((d, LANES), const),
            pl.BlockSpec((1, LANES), const),
            pl.BlockSpec((t, t), const),
        ],
        out_specs=(col_spec, col_spec,
                   pl.BlockSpec((None, None, 1, LANES), lambda bb, i: (bb, i, 0, 0))),
        scratch_shapes=[pltpu.VMEM((1, LANES), F32), pltpu.VMEM((1, LANES), F32)],
        compiler_params=_params(("arbitrary", "arbitrary")),
        name="forget_cumsum",
    )(x, fw, fb, tri)


def _fox_kernel(c_tab, q_ref, qf_ref, k_ref, kf_ref, v_ref, tri_ref, o_ref,
                qa_ref, kfh_ref, sa_ref, sb_ref, m_ref, acc_ref):
    h = pl.program_id(1)
    i = pl.program_id(2)
    tq, tk = FOX_T, FOX_TK
    per_tile = tk // tq
    lane = lax.broadcasted_iota(jnp.int32, (tq, LANES), 1)
    mine = (lane & (B_HEADS - 1)) == h
    q_side = lane < FOX_KLANE
    part_lane = (lane & (FOX_KLANE - 1)) < 3 * B_HEADS

    @pl.when(i == 0)
    def _():
        k_ones = jnp.where(mine & part_lane, 1.0, 0.0).astype(BF16)

        def fill(r, carry):
            rows = pl.ds(pl.multiple_of(r * tq, tq), tq)
            kfh_ref[rows, :] = jnp.where(q_side, k_ones, kf_ref[rows, :])
            return carry

        lax.fori_loop(0, kf_ref.shape[0] // tq, fill, 0)
    cbase = (pl.program_id(0) * B_HEADS + pl.program_id(1)) * pl.num_programs(2)
    c_q = jnp.full((1, LANES), c_tab[cbase + i], F32)
    qa_ref[:, 0:B_HEAD_DIM] = q_ref[...]
    q_ones = jnp.where(mine & part_lane, 1.0, 0.0).astype(BF16)
    qa_ref[:, B_HEAD_DIM:2 * B_HEAD_DIM] = jnp.where(q_side, qf_ref[...], q_ones)
    m_ref[...] = jnp.full(m_ref.shape, NEG, F32)
    acc_ref[...] = jnp.zeros_like(acc_ref)
    def ones_col(rows):
        return jnp.where(lax.broadcasted_iota(jnp.int32, (rows, LANES), 1) == 0, 1.0, 0.0).astype(BF16)

    def scores(jt, dst_ref):
        ks = pl.multiple_of(jt * tk, tk)
        ka = jnp.concatenate([k_ref[pl.ds(ks, tk), :], kfh_ref[pl.ds(ks, tk), :]], axis=1)
        dst_ref[...] = lax.dot_general(qa_ref[...], ka, (((1,), (1,)), ((), ())),
                                       preferred_element_type=F32)

    def absorb(src_ref, jt, own_blocks=None):
        ks = pl.multiple_of(jt * tk, tk)
        width = tk if own_blocks is None else own_blocks * tq
        if own_blocks is None:
            s = src_ref[...]
        else:
            diag = src_ref[:, width - tq:width] + tri_ref[...]
            s = diag if own_blocks == 1 else jnp.concatenate([src_ref[:, 0:width - tq], diag], axis=1)
        shift = c_q - jnp.full((1, LANES), c_tab[cbase + per_tile * jt], F32)
        m_prev = m_ref[...]
        m_new = jnp.maximum(m_prev, jnp.max(s, axis=-1, keepdims=True) + shift)
        p = jnp.exp2(s - jnp.concatenate([m_new - shift] * (width // LANES), axis=1))
        scale = jnp.exp2(m_prev - m_new)
        va = jnp.concatenate([v_ref[pl.ds(ks, width), :], ones_col(width)], axis=1)
        acc_ref[...] = jnp.concatenate([scale, scale], axis=1) * acc_ref[...] + jnp.dot(
            p.astype(BF16), va, preferred_element_type=F32)
        m_ref[...] = m_new

    def absorb_last(src_ref):
        own = i - per_tile * n_full
        for blocks in range(1, per_tile + 1):
            @pl.when(own == blocks - 1)
            def _(blocks=blocks):
                absorb(src_ref, n_full, blocks)

    n_full = i // per_tile
    scores(0, sa_ref)

    def pair(first):
        scores(first + 1, sb_ref)
        absorb(sa_ref, first)
        scores(first + 2, sa_ref)
        absorb(sb_ref, first + 1)

    def run(first, count):
        for t in range(0, count, 2):
            pair(first + t)

    done = 0
    for length in FOX_RUNS:
        @pl.when((n_full & length) != 0)
        def _(done=done, length=length):
            run(done, length)
        done = done + (n_full & length)

    @pl.when((n_full & 1) == 1)
    def _():
        scores(n_full, sb_ref)
        absorb(sa_ref, n_full - 1)
        absorb_last(sb_ref)

    @pl.when((n_full & 1) == 0)
    def _():
        absorb_last(sa_ref)

    acc = acc_ref[...]
    o_ref[...] = (acc[:, 0:B_HEAD_DIM] / acc[:, B_HEAD_DIM:B_HEAD_DIM + 1]).astype(o_ref.dtype)


def _fox(qkv, qf, kf, c_tab):
    b, s, d3 = qkv.shape
    d = d3 // 3
    t = FOX_T
    hd = B_HEAD_DIM
    causal = jnp.where(jnp.arange(t)[None, :] <= jnp.arange(t)[:, None], 0.0, NEG).astype(F32)
    grid_spec = pltpu.PrefetchScalarGridSpec(
        num_scalar_prefetch=1,
        grid=(b, B_HEADS, s // t),
        in_specs=[
            pl.BlockSpec((None, t, hd), lambda bb, h, i, c: (bb, i, h)),
            pl.BlockSpec((None, t, LANES), lambda bb, h, i, c: (bb, i, 0)),
            pl.BlockSpec((None, s, hd), lambda bb, h, i, c: (bb, 0, B_HEADS + h)),
            pl.BlockSpec((None, s, LANES), lambda bb, h, i, c: (bb, 0, 0)),
            pl.BlockSpec((None, s, hd), lambda bb, h, i, c: (bb, 0, 2 * B_HEADS + h)),
            pl.BlockSpec((t, t), lambda bb, h, i, c: (0, 0)),
        ],
        out_specs=pl.BlockSpec((None, t, hd), lambda bb, h, i, c: (bb, i, h)),
        scratch_shapes=[
            pltpu.VMEM((t, 2 * hd), BF16),
            pltpu.VMEM((s, LANES), BF16),
            pltpu.VMEM((t, FOX_TK), F32),
            pltpu.VMEM((t, FOX_TK), F32),
            pltpu.VMEM((t, LANES), F32),
            pltpu.VMEM((t, 2 * hd), F32),
        ],
    )
    return pl.pallas_call(
        _fox_kernel,
        out_shape=jax.ShapeDtypeStruct((b, s, d), BF16),
        grid_spec=grid_spec,
        compiler_params=_params(("arbitrary", "arbitrary", "arbitrary")),
        name="mixer_b",
    )(c_tab, qkv, qf, qkv, kf, qkv, causal)


def _moe_layer(x1, x1b, experts, gates, w_gate, w_up, w_down, layer, gain, bias):
    meta, tok_sorted, pos = _moe_plan(experts)
    xs = x1b[tok_sorted]
    yb = _moe(xs, w_gate, w_up, w_down, layer, meta)
    return _combine_ln(x1, yb[pos[0]], yb[pos[1]], gates.T, gain, bias, with_bf16=layer < DEPTH - 1)


def kernel(x, a_w_qkv, a_rel_bias, a_w_o, kv_w, fg_w, fg_b, b_w_q, b_w_o, router_w, router_b,
           moe_w_gate, moe_w_up, moe_w_down, ln_gain, ln_bias):
    b, s, d = x.shape
    n = b * s
    assert MM_TM == PAD_ROWS and s % MM_TM == 0 and s % FOX_TK == 0 and FOX_TK % FOX_T == 0
    assert s % (A_TQ * A_SUB) == 0 and PAD_ROWS % A_TQ == 0 and A_SUB % A_UNROLL == 0 and A_SUB >= 2 * A_UNROLL and A_UNROLL % 2 == 0
    assert (2 * n) % MOE_BLK == 0 and s // FOX_TK <= 2 * FOX_RUNS[0]

    rw = _pad_small_weight(router_w)
    rb = router_b.astype(F32).reshape(N_EXPERTS, 1)
    gain = ln_gain.astype(F32).reshape(DEPTH, 2, 1, d)
    bias = ln_bias.astype(F32).reshape(DEPTH, 2, 1, d)
    wg, wu, wd = moe_w_gate, moe_w_up, moe_w_down

    qscale = jnp.concatenate([jnp.full((d,), LOG2E / math.sqrt(A_HEAD_DIM), F32),
                              jnp.ones((2 * d,), F32)])
    w_qkv = (a_w_qkv[0] * qscale[None, :]).astype(BF16)
    qkv = _mm(x, w_qkv, pad_blocks=1)
    attn = _attn_a(qkv, _rel_bias_table(a_rel_bias[0]), s)
    x1, x1b, experts, gates = _post_attn(attn.reshape(n, d), a_w_o[0].astype(BF16),
                                         x.reshape(n, d), gain[0, 0], bias[0, 0], rw, rb)
    x2, x2b = _moe_layer(x1, x1b, experts, gates, wg, wu, wd, 0, gain[0, 1], bias[0, 1])

    fb = jnp.zeros((1, LANES), F32).at[0, :B_HEADS].set(fg_b.astype(F32))
    qf, kf, carry_in = _forget_columns(x2.reshape(b, s, d), _pad_small_weight(fg_w), fb)
    c_tab = carry_in[:, :, 0, :B_HEADS].transpose(0, 2, 1).reshape(-1)

    w_q = (b_w_q[0] * (LOG2E / math.sqrt(B_HEAD_DIM))).astype(BF16)
    qkv_b = _mm(x2b.reshape(b, s, d), jnp.concatenate([w_q, kv_w.astype(BF16)], axis=1))
    attn = _fox(qkv_b, qf, kf, c_tab)
    x3, x3b, experts, gates = _post_attn(attn.reshape(n, d), b_w_o[0].astype(BF16), x2,
                                         gain[1, 0], bias[1, 0], rw, rb)
    (x4,) = _moe_layer(x3, x3b, experts, gates, wg, wu, wd, 1, gain[1, 1], bias[1, 1])
    return x4.reshape(b, s, d)
```

```python
import functools
import math

import jax
import jax.numpy as jnp
from jax import lax
from jax.experimental import pallas as pl
from jax.experimental.pallas import tpu as pltpu

F32 = jnp.float32
BF16 = jnp.bfloat16

CHUNK = 64
LEFT_CHUNKS = 8
A_HEADS = 32
A_HEAD_DIM = 64
REL_CLIP = 2 * CHUNK
B_HEADS = 16
B_HEAD_DIM = 128
N_EXPERTS = 16
N_GROUPS = 4
EXPERTS_PER_GROUP = 4
DEPTH = 2
ALPHA = (2.0 * DEPTH) ** 0.25
LN_EPS = 1e-5
NEG = -1e30
LOG2E = 1.4426950408889634

LANES = 128
VMEM_LIMIT = 62 * 1024 * 1024

MM_TM = 512
MM_TN = 2048
PAD_ROWS = LEFT_CHUNKS * CHUNK
A_TQ = 256
A_WIN = A_TQ + PAD_ROWS
A_SUB = 32
A_UNROLL = 8
A_TABLES = PAD_ROWS // A_TQ + 1
POST_TM = 256
MOE_BLK = 1024
MOE_CHUNK = 256
MOE_TF = 512
LN_TM = 512
FOX_T = 512
FOX_TK = 1024
FOX_KLANE = 64
FOX_RUNS = (8, 4, 2)


def _params(sem):
    return pltpu.CompilerParams(dimension_semantics=sem, vmem_limit_bytes=VMEM_LIMIT)


def _mm_kernel(x_ref, w_ref, o_ref, *, pad_blocks):
    def compute():
        o_ref[...] = jnp.dot(x_ref[...].astype(BF16), w_ref[...],
                             preferred_element_type=F32).astype(o_ref.dtype)

    if pad_blocks:
        i = pl.program_id(2)

        @pl.when(i < pad_blocks)
        def _():
            o_ref[...] = jnp.zeros_like(o_ref)

        pl.when(i >= pad_blocks)(compute)
    else:
        compute()


def _mm(x, w, *, pad_blocks=0):
    b, s, k = x.shape
    m = w.shape[1]
    tn = min(MM_TN, m)
    grid = (m // tn, b, s // MM_TM + pad_blocks)
    return pl.pallas_call(
        functools.partial(_mm_kernel, pad_blocks=pad_blocks),
        out_shape=jax.ShapeDtypeStruct((b, s + pad_blocks * MM_TM, m), BF16),
        grid=grid,
        in_specs=[
            pl.BlockSpec((None, MM_TM, k),
                         lambda j, bb, i: (bb, jnp.maximum(i - pad_blocks, 0), 0)),
            pl.BlockSpec((k, tn), lambda j, bb, i: (0, j)),
        ],
        out_specs=pl.BlockSpec((None, MM_TM, tn), lambda j, bb, i: (bb, i, j)),
        compiler_params=_params(("arbitrary", "arbitrary", "arbitrary")),
        name="dense_proj",
    )(x, w)


def _attn_a_kernel(q_ref, k_ref, v_ref, bias_ref, o_ref, sa_ref, sb_ref):
    i = pl.program_id(2)
    head0 = lax.broadcasted_iota(jnp.int32, (A_TQ, LANES), 1) < A_HEAD_DIM
    ones_col = jnp.where(lax.broadcasted_iota(jnp.int32, (A_WIN, LANES), 1) == 0, 1.0, 0.0).astype(BF16)

    def window(sb):
        g = i * A_SUB + sb
        return g, pl.multiple_of(g * A_TQ, A_TQ)

    def scores(sb, dst_ref):
        _, start = window(sb)
        q = q_ref[pl.ds(start + PAD_ROWS, A_TQ), :]
        zero = jnp.zeros_like(q)
        q2 = jnp.concatenate([jnp.where(head0, q, zero), jnp.where(head0, zero, q)], axis=0)
        dst_ref[...] = lax.dot_general(q2, k_ref[pl.ds(start, A_WIN), :], (((1,), (1,)), ((), ())),
                                       preferred_element_type=F32)

    def absorb(src_ref, sb):
        g, start = window(sb)
        s = src_ref[...] + bias_ref[jnp.minimum(g, A_TABLES - 1)]
        p = jnp.exp2(s - jnp.max(s, axis=-1, keepdims=True))
        va = jnp.concatenate([v_ref[pl.ds(start, A_WIN), :], ones_col], axis=1)
        o = jnp.dot(p.astype(BF16), va, preferred_element_type=F32)
        o = o[:, 0:LANES] / o[:, LANES:LANES + 1]
        rows = pl.ds(pl.multiple_of(sb * A_TQ, A_TQ), A_TQ)
        o_ref[rows, :] = jnp.where(head0, o[0:A_TQ], o[A_TQ:2 * A_TQ]).astype(o_ref.dtype)

    scores(0, sa_ref)

    def pair(first):
        scores(first + 1, sb_ref)
        absorb(sa_ref, first)
        scores(first + 2, sa_ref)
        absorb(sb_ref, first + 1)

    def octet(jj, carry):
        for t in range(0, A_UNROLL, 2):
            pair(A_UNROLL * jj + t)
        return carry

    lax.fori_loop(0, A_SUB // A_UNROLL - 1, octet, 0)
    for t in range(A_SUB - A_UNROLL, A_SUB - 2, 2):
        pair(t)
    scores(A_SUB - 1, sb_ref)
    absorb(sa_ref, A_SUB - 2)
    absorb(sb_ref, A_SUB - 1)


def _attn_a(qkv_pad, bias_tab, s):
    b = qkv_pad.shape[0]
    d = A_HEADS * A_HEAD_DIM
    pairs = d // LANES
    sp = s + PAD_ROWS
    rows = A_TQ * A_SUB
    return pl.pallas_call(
        _attn_a_kernel,
        out_shape=jax.ShapeDtypeStruct((b, s, d), BF16),
        grid=(pairs, b, s // rows),
        in_specs=[
            pl.BlockSpec((None, sp, LANES), lambda j, bb, i: (bb, 0, j)),
            pl.BlockSpec((None, sp, LANES), lambda j, bb, i: (bb, 0, pairs + j)),
            pl.BlockSpec((None, sp, LANES), lambda j, bb, i: (bb, 0, 2 * pairs + j)),
            pl.BlockSpec((None, A_TABLES, 2 * A_TQ, A_WIN), lambda j, bb, i: (j, 0, 0, 0)),
        ],
        out_specs=pl.BlockSpec((None, rows, LANES), lambda j, bb, i: (bb, i, j)),
        scratch_shapes=[pltpu.VMEM((2 * A_TQ, A_WIN), F32), pltpu.VMEM((2 * A_TQ, A_WIN), F32)],
        compiler_params=_params(("arbitrary", "arbitrary", "arbitrary")),
        name="mixer_a",
    )(qkv_pad, qkv_pad, qkv_pad, bias_tab)


def _rel_bias_table(rel_bias):
    qi = jnp.arange(A_TQ)[:, None]
    kj = jnp.arange(A_WIN)[None, :]
    dchunk = kj // CHUNK - qi // CHUNK
    inband = (dchunk >= 0) & (dchunk <= LEFT_CHUNKS)
    period = A_TQ + A_WIN
    u = jnp.arange(period)
    u = jnp.where(u < A_WIN, u, u - period)
    idx = jnp.clip(PAD_ROWS - u, -REL_CLIP, REL_CLIP) + REL_CLIP
    g = rel_bias.astype(F32)[:, idx] * LOG2E
    flat = jnp.tile(g, (1, A_TQ))[:, :A_TQ * (period - 1)]
    tab = flat.reshape(-1, A_TQ, period - 1)[:, :, :A_WIN]
    first_real = PAD_ROWS - A_TQ * jnp.arange(A_TABLES)[:, None, None]
    keep = inband[None] & (kj[None] >= first_real)
    pair_tab = tab.reshape(-1, 1, 2, A_TQ, A_WIN)
    tabs = jnp.where(keep[None, :, None], pair_tab, NEG)
    return tabs.reshape(-1, A_TABLES, 2 * A_TQ, A_WIN)


def _layer_norm(z, gain, bias):
    mu = jnp.mean(z, axis=-1, keepdims=True)
    zc = z - mu
    var = jnp.mean(zc * zc, axis=-1, keepdims=True)
    return zc * lax.rsqrt(var + LN_EPS) * gain + bias


def _small_proj_t(y, yb, w_ref, n_out):
    ylo = (y - yb.astype(F32)).astype(BF16)
    w = w_ref[...]
    wh = w.astype(BF16)
    wl = (w - wh.astype(F32)).astype(BF16)
    both = jnp.dot(yb, jnp.concatenate([wh, wl], axis=1), preferred_element_type=F32)
    acc = both[:, 0:LANES] + jnp.dot(ylo, wh, preferred_element_type=F32) + both[:, LANES:2 * LANES]
    return acc.T[:n_out, :]


def _pad_small_weight(w):
    k, n = w.shape
    return jnp.zeros((k, LANES), F32).at[:, :n].set(w.astype(F32))


def _route(lt):
    m = jnp.max(lt, axis=0, keepdims=True)
    e = jnp.exp(lt - m)
    p = e / jnp.sum(e, axis=0, keepdims=True)
    rows = [p[k:k + 1, :] for k in range(N_EXPERTS)]
    best = None
    gi = None
    for g in range(N_GROUPS):
        r = rows[EXPERTS_PER_GROUP * g:EXPERTS_PER_GROUP * (g + 1)]
        sc = None
        for a in range(EXPERTS_PER_GROUP):
            for c in range(a + 1, EXPERTS_PER_GROUP):
                pair = r[a] + r[c]
                sc = pair if sc is None else jnp.maximum(sc, pair)
        if g == 0:
            best = sc
            gi = jnp.zeros(sc.shape, jnp.int32)
        else:
            upd = sc > best
            best = jnp.where(upd, sc, best)
            gi = jnp.where(upd, g, gi)
    pin = []
    for k in range(EXPERTS_PER_GROUP):
        v = rows[(N_GROUPS - 1) * EXPERTS_PER_GROUP + k]
        for g in range(N_GROUPS - 2, -1, -1):
            v = jnp.where(gi == g, rows[g * EXPERTS_PER_GROUP + k], v)
        pin.append(v)
    v1 = pin[0]
    i1 = jnp.zeros(v1.shape, jnp.int32)
    for k in range(1, EXPERTS_PER_GROUP):
        upd = pin[k] > v1
        v1 = jnp.where(upd, pin[k], v1)
        i1 = jnp.where(upd, k, i1)
    v2 = jnp.full(v1.shape, -1.0, F32)
    i2 = jnp.zeros(v1.shape, jnp.int32)
    for k in range(EXPERTS_PER_GROUP):
        cand = jnp.where(i1 == k, -1.0, pin[k])
        upd = cand > v2
        v2 = jnp.where(upd, cand, v2)
        i2 = jnp.where(upd, k, i2)
    tot = v1 + v2
    base = gi * EXPERTS_PER_GROUP
    experts = jnp.concatenate([base + i1, base + i2], axis=0)
    gates = jnp.concatenate([v1 / tot, v2 / tot], axis=0)
    return experts, gates


def _post_attn_kernel(a0_ref, alo_ref, ahi_ref, w_ref, x_ref, g_ref, b_ref, rw_ref, rb_ref,
                      x1_ref, x1b_ref, ex_ref, gt_ref, h0_ref, h1_ref):
    t = POST_TM

    def project(a_ref, h_ref):
        h_ref[...] = jnp.dot(a_ref[...], w_ref[...], preferred_element_type=F32)

    def finish(h_ref, half):
        rows = slice(half * t, (half + 1) * t)
        y = _layer_norm(ALPHA * x_ref[rows, :] + h_ref[...], g_ref[...], b_ref[...])
        yb = y.astype(BF16)
        x1_ref[rows, :] = y
        x1b_ref[rows, :] = yb
        lt = _small_proj_t(y, yb, rw_ref, N_EXPERTS) + rb_ref[...]
        experts, gates = _route(lt)
        ex_ref[:, rows] = experts
        gt_ref[:, rows] = gates

    @pl.when(pl.program_id(0) == 0)
    def _():
        project(a0_ref, h0_ref)

    project(alo_ref, h1_ref)
    finish(h0_ref, 0)
    project(ahi_ref, h0_ref)
    finish(h1_ref, 1)


def _post_attn(attn, w_o, x, gain, bias, rw, rb):
    n, d = x.shape
    t = POST_TM
    last_tile = n // t - 1
    row = lambda i: (i, 0)
    const = lambda i: (0, 0)
    return pl.pallas_call(
        _post_attn_kernel,
        out_shape=(jax.ShapeDtypeStruct((n, d), F32),
                   jax.ShapeDtypeStruct((n, d), BF16),
                   jax.ShapeDtypeStruct((2, n), jnp.int32),
                   jax.ShapeDtypeStruct((2, n), F32)),
        grid=(n // (2 * t),),
        in_specs=[
            pl.BlockSpec((t, d), const),
            pl.BlockSpec((t, d), lambda i: (2 * i + 1, 0)),
            pl.BlockSpec((t, d), lambda i: (jnp.minimum(2 * i + 2, last_tile), 0)),
            pl.BlockSpec((d, d), const),
            pl.BlockSpec((2 * t, d), row),
            pl.BlockSpec((1, d), const),
            pl.BlockSpec((1, d), const),
            pl.BlockSpec((d, LANES), const),
            pl.BlockSpec((N_EXPERTS, 1), const),
        ],
        out_specs=(pl.BlockSpec((2 * t, d), row),
                   pl.BlockSpec((2 * t, d), row),
                   pl.BlockSpec((2, 2 * t), lambda i: (0, i)),
                   pl.BlockSpec((2, 2 * t), lambda i: (0, i))),
        scratch_shapes=[pltpu.VMEM((t, d), F32), pltpu.VMEM((t, d), F32)],
        compiler_params=_params(("arbitrary",)),
        name="out_proj_ln_router",
    )(attn, attn, attn, w_o, x, gain, bias, rw, rb)


def _moe_kernel(iblk, iex, ivalid, ifirst, starts, ends, x_ref, wg_ref, wu_ref, wd_ref, o_ref,
                acc_ref):
    w = pl.program_id(0)
    f = pl.program_id(1)

    @pl.when((ifirst[w] == 1) & (f == 0))
    def _():
        acc_ref[...] = jnp.zeros_like(acc_ref)

    @pl.when(ivalid[w] == 1)
    def _():
        e = iex[w]
        lo = starts[e]
        hi = ends[e]
        base = iblk[w] * MOE_BLK

        def swiglu(x, wg, wu, wd, keep):
            hg = jnp.dot(x, wg, preferred_element_type=F32)
            hu = jnp.dot(x, wu, preferred_element_type=F32)
            hid = hg * jax.nn.sigmoid(hg) * hu
            if keep is not None:
                hid = jnp.where(keep, hid, 0.0)
            return jnp.dot(hid.astype(BF16), wd, preferred_element_type=F32)

        whole = (lo <= base) & (hi >= base + MOE_BLK)

        @pl.when(whole)
        def _():
            acc_ref[...] += swiglu(x_ref[...], wg_ref[...].astype(BF16), wu_ref[...].astype(BF16),
                                   wd_ref[...].astype(BF16), None)

        @pl.when(jnp.logical_not(whole))
        def _():
            wg = wg_ref[...].astype(BF16)
            wu = wu_ref[...].astype(BF16)
            wd = wd_ref[...].astype(BF16)
            for c in range(MOE_BLK // MOE_CHUNK):
                r0 = base + c * MOE_CHUNK

                @pl.when((lo < r0 + MOE_CHUNK) & (hi > r0))
                def _():
                    rows = slice(c * MOE_CHUNK, (c + 1) * MOE_CHUNK)
                    row = r0 + lax.broadcasted_iota(jnp.int32, (MOE_CHUNK, 1), 0)
                    acc_ref[rows, :] += swiglu(x_ref[rows, :], wg, wu, wd, (row >= lo) & (row < hi))

        @pl.when(f == pl.num_programs(1) - 1)
        def _():
            o_ref[...] = acc_ref[...].astype(o_ref.dtype)


def _moe(xs, w_gate, w_up, w_down, layer, meta):
    a, d = xs.shape
    dff = w_gate.shape[-1]
    nf = dff // MOE_TF
    n_items = meta[0].shape[0]

    def fsel(w, f, ivalid):
        return jnp.where(ivalid[w] == 1, f, nf - 1)

    grid_spec = pltpu.PrefetchScalarGridSpec(
        num_scalar_prefetch=6,
        grid=(n_items, nf),
        in_specs=[
            pl.BlockSpec((MOE_BLK, d), lambda w, f, ib, ie, iv, i1, st, en: (ib[w], 0)),
            pl.BlockSpec((None, None, d, MOE_TF),
                         lambda w, f, ib, ie, iv, i1, st, en: (layer, ie[w], 0, fsel(w, f, iv))),
            pl.BlockSpec((None, None, d, MOE_TF),
                         lambda w, f, ib, ie, iv, i1, st, en: (layer, ie[w], 0, fsel(w, f, iv))),
            pl.BlockSpec((None, None, MOE_TF, d),
                         lambda w, f, ib, ie, iv, i1, st, en: (layer, ie[w], fsel(w, f, iv), 0)),
        ],
        out_specs=pl.BlockSpec((MOE_BLK, d), lambda w, f, ib, ie, iv, i1, st, en: (ib[w], 0)),
        scratch_shapes=[pltpu.VMEM((MOE_BLK, d), F32)],
    )
    return pl.pallas_call(
        _moe_kernel,
        out_shape=jax.ShapeDtypeStruct((a, d), BF16),
        grid_spec=grid_spec,
        compiler_params=_params(("arbitrary", "arbitrary")),
        name="moe_experts",
    )(*meta, xs, w_gate, w_up, w_down)


def _moe_plan(experts):
    n = experts.shape[1]
    a = 2 * n
    nblk = a // MOE_BLK
    n_items = nblk + N_EXPERTS - 1
    e_flat = experts.reshape(-1)
    iota = jnp.arange(a, dtype=jnp.int32)
    e_sorted, order = lax.sort((e_flat, iota), num_keys=1, is_stable=True)
    tok_sorted = order % n
    _, pos = lax.sort((order, iota), num_keys=1)
    pos = pos.reshape(2, n)
    eids = jnp.arange(N_EXPERTS, dtype=jnp.int32)
    starts = jnp.searchsorted(e_sorted, eids, side="left").astype(jnp.int32)
    ends = jnp.searchsorted(e_sorted, eids, side="right").astype(jnp.int32)
    e_lo = e_sorted[0::MOE_BLK]
    e_hi = e_sorted[MOE_BLK - 1::MOE_BLK]
    per_blk = e_hi - e_lo + 1
    iend = jnp.cumsum(per_blk).astype(jnp.int32)
    istart = iend - per_blk
    total = iend[-1]
    w = jnp.arange(n_items, dtype=jnp.int32)
    blk = jnp.clip(jnp.searchsorted(iend, w, side="right"), 0, nblk - 1).astype(jnp.int32)
    ex = e_lo[blk] + (w - istart[blk])
    valid = w < total
    blk = jnp.where(valid, blk, nblk - 1)
    ex = jnp.where(valid, ex, e_hi[nblk - 1])
    first = valid & (w == istart[blk])
    meta = (blk, ex.astype(jnp.int32), valid.astype(jnp.int32), first.astype(jnp.int32),
            starts, ends)
    return meta, tok_sorted, pos


def _combine_ln_kernel(x_ref, y0_ref, y1_ref, gt_ref, g_ref, b_ref, o_ref, *maybe_ob_ref):
    gt = gt_ref[...]
    y = y0_ref[...].astype(F32) * gt[:, 0:1] + y1_ref[...].astype(F32) * gt[:, 1:2]
    out = _layer_norm(ALPHA * x_ref[...] + y, g_ref[...], b_ref[...])
    o_ref[...] = out
    for ob_ref in maybe_ob_ref:
        ob_ref[...] = out.astype(BF16)


def _combine_ln(x, y0, y1, gates_t, gain, bias, *, with_bf16):
    n, d = x.shape
    tm = LN_TM
    row = lambda i: (i, 0)
    const = lambda i: (0, 0)
    dtypes = (F32, BF16) if with_bf16 else (F32,)
    return pl.pallas_call(
        _combine_ln_kernel,
        out_shape=tuple(jax.ShapeDtypeStruct((n, d), t) for t in dtypes),
        grid=(n // tm,),
        in_specs=[
            pl.BlockSpec((tm, d), row),
            pl.BlockSpec((tm, d), row),
            pl.BlockSpec((tm, d), row),
            pl.BlockSpec((tm, 2), row),
            pl.BlockSpec((1, d), const),
            pl.BlockSpec((1, d), const),
        ],
        out_specs=tuple(pl.BlockSpec((tm, d), row) for _ in dtypes),
        compiler_params=_params(("arbitrary",)),
        name="moe_combine_ln",
    )(x, y0, y1, gates_t, gain, bias)


def _fg_kernel(x_ref, w_ref, b_ref, tri_ref, qf_ref, kf_ref, c_ref, carry_ref, koff_ref):
    i = pl.program_id(1)

    @pl.when(i == 0)
    def _():
        carry_ref[...] = jnp.zeros_like(carry_ref)

    @pl.when(i % (FOX_TK // FOX_T) == 0)
    def _():
        koff_ref[...] = jnp.zeros_like(koff_ref)

    def split3(v):
        hi = v.astype(BF16)
        r1 = v - hi.astype(F32)
        mid = r1.astype(BF16)
        lo = (r1 - mid.astype(F32)).astype(BF16)
        return hi, mid, lo

    def columns(v, first_lane):
        hi, mid, lo = (p.astype(F32) for p in split3(v))
        return (pltpu.roll(hi, first_lane, 1) + pltpu.roll(mid, first_lane + B_HEADS, 1)
                + pltpu.roll(lo, first_lane + 2 * B_HEADS, 1))

    y = x_ref[...]
    yb = y.astype(BF16)
    ylo = (y - yb.astype(F32)).astype(BF16)
    w = w_ref[...]
    wh = w.astype(BF16)
    wl = (w - wh.astype(F32)).astype(BF16)
    z = (jnp.dot(yb, wh, preferred_element_type=F32) + jnp.dot(ylo, wh, preferred_element_type=F32)
         + jnp.dot(yb, wl, preferred_element_type=F32)) + b_ref[...]
    head_lane = lax.broadcasted_iota(jnp.int32, (FOX_T, LANES), 1) < B_HEADS
    lf = jnp.where(head_lane, (jnp.minimum(z, 0.0) - jnp.log1p(jnp.exp(-jnp.abs(z)))) * LOG2E, 0.0)
    parts = jnp.concatenate(split3(lf), axis=1)
    cs3 = jnp.dot(tri_ref[...], parts, preferred_element_type=F32)
    cs = cs3[:, 0:LANES] + cs3[:, LANES:2 * LANES] + cs3[:, 2 * LANES:3 * LANES]
    c_ref[...] = carry_ref[...]
    carry_ref[...] = carry_ref[...] + cs[FOX_T - 1:FOX_T, :]
    csk = cs + koff_ref[...]
    koff_ref[...] = koff_ref[...] + cs[FOX_T - 1:FOX_T, :]
    qf_ref[...] = columns(cs, 0).astype(BF16)
    kf_ref[...] = columns(-csk, FOX_KLANE).astype(BF16)


def _forget_columns(x, fw, fb):
    b, s, d = x.shape
    t = FOX_T
    nblk = s // t
    tri = (jnp.arange(t)[:, None] >= jnp.arange(t)[None, :]).astype(BF16)
    const = lambda bb, i: (0, 0)
    cols = jax.ShapeDtypeStruct((b, s, LANES), BF16)
    col_spec = pl.BlockSpec((None, t, LANES), lambda bb, i: (bb, i, 0))
    return pl.pallas_call(
        _fg_kernel,
        out_shape=(cols, cols, jax.ShapeDtypeStruct((b, nblk, 1, LANES), F32)),
        grid=(b, nblk),
        in_specs=[
            pl.BlockSpec((None, t, d), lambda bb, i: (bb, i, 0)),
            pl.BlockSpec((d, LANES), const),
            pl.BlockSpec((1, LANES), const),
            pl.BlockSpec((t, t), const),
        ],
        out_specs=(col_spec, col_spec,
                   pl.BlockSpec((None, None, 1, LANES), lambda bb, i: (bb, i, 0, 0))),
        scratch_shapes=[pltpu.VMEM((1, LANES), F32), pltpu.VMEM((1, LANES), F32)],
        compiler_params=_params(("arbitrary", "arbitrary")),
        name="forget_cumsum",
    )(x, fw, fb, tri)


def _fox_kernel(c_tab, q_ref, qf_ref, k_ref, kf_ref, v_ref, tri_ref, o_ref,
                qa_ref, kfh_ref, sa_ref, sb_ref, m_ref, acc_ref):
    h = pl.program_id(1)
    i = pl.program_id(2)
    tq, tk = FOX_T, FOX_TK
    per_tile = tk // tq
    lane = lax.broadcasted_iota(jnp.int32, (tq, LANES), 1)
    mine = (lane & (B_HEADS - 1)) == h
    q_side = lane < FOX_KLANE
    part_lane = (lane & (FOX_KLANE - 1)) < 3 * B_HEADS

    @pl.when(i == 0)
    def _():
        k_ones = jnp.where(mine & part_lane, 1.0, 0.0).astype(BF16)

        def fill(r, carry):
            rows = pl.ds(pl.multiple_of(r * tq, tq), tq)
            kfh_ref[rows, :] = jnp.where(q_side, k_ones, kf_ref[rows, :])
            return carry

        lax.fori_loop(0, kf_ref.shape[0] // tq, fill, 0)
    cbase = (pl.program_id(0) * B_HEADS + pl.program_id(1)) * pl.num_programs(2)
    c_q = jnp.full((1, LANES), c_tab[cbase + i], F32)
    qa_ref[:, 0:B_HEAD_DIM] = q_ref[...]
    q_ones = jnp.where(mine & part_lane, 1.0, 0.0).astype(BF16)
    qa_ref[:, B_HEAD_DIM:2 * B_HEAD_DIM] = jnp.where(q_side, qf_ref[...], q_ones)
    m_ref[...] = jnp.full(m_ref.shape, NEG, F32)
    acc_ref[...] = jnp.zeros_like(acc_ref)
    def ones_col(rows):
        return jnp.where(lax.broadcasted_iota(jnp.int32, (rows, LANES), 1) == 0, 1.0, 0.0).astype(BF16)

    def scores(jt, dst_ref):
        ks = pl.multiple_of(jt * tk, tk)
        ka = jnp.concatenate([k_ref[pl.ds(ks, tk), :], kfh_ref[pl.ds(ks, tk), :]], axis=1)
        dst_ref[...] = lax.dot_general(qa_ref[...], ka, (((1,), (1,)), ((), ())),
                                       preferred_element_type=F32)

    def absorb(src_ref, jt, own_blocks=None):
        ks = pl.multiple_of(jt * tk, tk)
        width = tk if own_blocks is None else own_blocks * tq
        if own_blocks is None:
            s = src_ref[...]
        else:
            diag = src_ref[:, width - tq:width] + tri_ref[...]
            s = diag if own_blocks == 1 else jnp.concatenate([src_ref[:, 0:width - tq], diag], axis=1)
        shift = c_q - jnp.full((1, LANES), c_tab[cbase + per_tile * jt], F32)
        m_prev = m_ref[...]
        m_new = jnp.maximum(m_prev, jnp.max(s, axis=-1, keepdims=True) + shift)
        p = jnp.exp2(s - jnp.concatenate([m_new - shift] * (width // LANES), axis=1))
        scale = jnp.exp2(m_prev - m_new)
        va = jnp.concatenate([v_ref[pl.ds(ks, width), :], ones_col(width)], axis=1)
        acc_ref[...] = jnp.concatenate([scale, scale], axis=1) * acc_ref[...] + jnp.dot(
            p.astype(BF16), va, preferred_element_type=F32)
        m_ref[...] = m_new

    def absorb_last(src_ref):
        own = i - per_tile * n_full
        for blocks in range(1, per_tile + 1):
            @pl.when(own == blocks - 1)
            def _(blocks=blocks):
                absorb(src_ref, n_full, blocks)

    n_full = i // per_tile
    scores(0, sa_ref)

    def pair(first):
        scores(first + 1, sb_ref)
        absorb(sa_ref, first)
        scores(first + 2, sa_ref)
        absorb(sb_ref, first + 1)

    def run(first, count):
        for t in range(0, count, 2):
            pair(first + t)

    done = 0
    for length in FOX_RUNS:
        @pl.when((n_full & length) != 0)
        def _(done=done, length=length):
            run(done, length)
        done = done + (n_full & length)

    @pl.when((n_full & 1) == 1)
    def _():
        scores(n_full, sb_ref)
        absorb(sa_ref, n_full - 1)
        absorb_last(sb_ref)

    @pl.when((n_full & 1) == 0)
    def _():
        absorb_last(sa_ref)

    acc = acc_ref[...]
    o_ref[...] = (acc[:, 0:B_HEAD_DIM] / acc[:, B_HEAD_DIM:B_HEAD_DIM + 1]).astype(o_ref.dtype)


def _fox(qkv, qf, kf, c_tab):
    b, s, d3 = qkv.shape
    d = d3 // 3
    t = FOX_T
    hd = B_HEAD_DIM
    causal = jnp.where(jnp.arange(t)[None, :] <= jnp.arange(t)[:, None], 0.0, NEG).astype(F32)
    grid_spec = pltpu.PrefetchScalarGridSpec(
        num_scalar_prefetch=1,
        grid=(b, B_HEADS, s // t),
        in_specs=[
            pl.BlockSpec((None, t, hd), lambda bb, h, i, c: (bb, i, h)),
            pl.BlockSpec((None, t, LANES), lambda bb, h, i, c: (bb, i, 0)),
            pl.BlockSpec((None, s, hd), lambda bb, h, i, c: (bb, 0, B_HEADS + h)),
            pl.BlockSpec((None, s, LANES), lambda bb, h, i, c: (bb, 0, 0)),
            pl.BlockSpec((None, s, hd), lambda bb, h, i, c: (bb, 0, 2 * B_HEADS + h)),
            pl.BlockSpec((t, t), lambda bb, h, i, c: (0, 0)),
        ],
        out_specs=pl.BlockSpec((None, t, hd), lambda bb, h, i, c: (bb, i, h)),
        scratch_shapes=[
            pltpu.VMEM((t, 2 * hd), BF16),
            pltpu.VMEM((s, LANES), BF16),
            pltpu.VMEM((t, FOX_TK), F32),
            pltpu.VMEM((t, FOX_TK), F32),
            pltpu.VMEM((t, LANES), F32),
            pltpu.VMEM((t, 2 * hd), F32),
        ],
    )
    return pl.pallas_call(
        _fox_kernel,
        out_shape=jax.ShapeDtypeStruct((b, s, d), BF16),
        grid_spec=grid_spec,
        compiler_params=_params(("arbitrary", "arbitrary", "arbitrary")),
        name="mixer_b",
    )(c_tab, qkv, qf, qkv, kf, qkv, causal)


def _moe_layer(x1, x1b, experts, gates, w_gate, w_up, w_down, layer, gain, bias):
    meta, tok_sorted, pos = _moe_plan(experts)
    xs = x1b[tok_sorted]
    yb = _moe(xs, w_gate, w_up, w_down, layer, meta)
    return _combine_ln(x1, yb[pos[0]], yb[pos[1]], gates.T, gain, bias, with_bf16=layer < DEPTH - 1)


def kernel(x, a_w_qkv, a_rel_bias, a_w_o, kv_w, fg_w, fg_b, b_w_q, b_w_o, router_w, router_b,
           moe_w_gate, moe_w_up, moe_w_down, ln_gain, ln_bias):
    b, s, d = x.shape
    n = b * s
    assert MM_TM == PAD_ROWS and s % MM_TM == 0 and s % FOX_TK == 0 and FOX_TK % FOX_T == 0
    assert s % (A_TQ * A_SUB) == 0 and PAD_ROWS % A_TQ == 0 and A_SUB % A_UNROLL == 0 and A_SUB >= 2 * A_UNROLL and A_UNROLL % 2 == 0
    assert (2 * n) % MOE_BLK == 0 and s // FOX_TK <= 2 * FOX_RUNS[0]

    rw = _pad_small_weight(router_w)
    rb = router_b.astype(F32).reshape(N_EXPERTS, 1)
    gain = ln_gain.astype(F32).reshape(DEPTH, 2, 1, d)
    bias = ln_bias.astype(F32).reshape(DEPTH, 2, 1, d)
    wg, wu, wd = moe_w_gate, moe_w_up, moe_w_down

    qscale = jnp.concatenate([jnp.full((d,), LOG2E / math.sqrt(A_HEAD_DIM), F32),
                              jnp.ones((2 * d,), F32)])
    w_qkv = (a_w_qkv[0] * qscale[None, :]).astype(BF16)
    qkv = _mm(x, w_qkv, pad_blocks=1)
    attn = _attn_a(qkv, _rel_bias_table(a_rel_bias[0]), s)
    x1, x1b, experts, gates = _post_attn(attn.reshape(n, d), a_w_o[0].astype(BF16),
                                         x.reshape(n, d), gain[0, 0], bias[0, 0], rw, rb)
    x2, x2b = _moe_layer(x1, x1b, experts, gates, wg, wu, wd, 0, gain[0, 1], bias[0, 1])

    fb = jnp.zeros((1, LANES), F32).at[0, :B_HEADS].set(fg_b.astype(F32))
    qf, kf, carry_in = _forget_columns(x2.reshape(b, s, d), _pad_small_weight(fg_w), fb)
    c_tab = carry_in[:, :, 0, :B_HEADS].transpose(0, 2, 1).reshape(-1)

    w_q = (b_w_q[0] * (LOG2E / math.sqrt(B_HEAD_DIM))).astype(BF16)
    qkv_b = _mm(x2b.reshape(b, s, d), jnp.concatenate([w_q, kv_w.astype(BF16)], axis=1))
    attn = _fox(qkv_b, qf, kf, c_tab)
    x3, x3b, experts, gates = _post_attn(attn.reshape(n, d), b_w_o[0].astype(BF16), x2,
                                         gain[1, 0], bias[1, 0], rw, rb)
    (x4,) = _moe_layer(x3, x3b, experts, gates, wg, wu, wd, 1, gain[1, 1], bias[1, 1])
    return x4.reshape(b, s, d)
```

```python
import functools
import math

import jax
import jax.numpy as jnp
from jax import lax
from jax.experimental import pallas as pl
from jax.experimental.pallas import tpu as pltpu

F32 = jnp.float32
BF16 = jnp.bfloat16

CHUNK = 64
LEFT_CHUNKS = 8
A_HEADS = 32
A_HEAD_DIM = 64
REL_CLIP = 2 * CHUNK
B_HEADS = 16
B_HEAD_DIM = 128
N_EXPERTS = 16
N_GROUPS = 4
EXPERTS_PER_GROUP = 4
DEPTH = 2
ALPHA = (2.0 * DEPTH) ** 0.25
LN_EPS = 1e-5
NEG = -1e30
LOG2E = 1.4426950408889634

LANES = 128
VMEM_LIMIT = 62 * 1024 * 1024

MM_TM = 512
MM_TN = 2048
PAD_ROWS = LEFT_CHUNKS * CHUNK
A_TQ = 256
A_WIN = A_TQ + PAD_ROWS
A_SUB = 32
A_UNROLL = 8
A_TABLES = PAD_ROWS // A_TQ + 1
POST_TM = 256
MOE_BLK = 1024
MOE_CHUNK = 256
MOE_TF = 512
LN_TM = 512
FOX_T = 512
FOX_TK = 1024
FOX_KLANE = 64
FOX_RUNS = (8, 4, 2)


def _params(sem):
    return pltpu.CompilerParams(dimension_semantics=sem, vmem_limit_bytes=VMEM_LIMIT)


def _mm_kernel(x_ref, w_ref, o_ref, *, pad_blocks):
    def compute():
        o_ref[...] = jnp.dot(x_ref[...].astype(BF16), w_ref[...],
                             preferred_element_type=F32).astype(o_ref.dtype)

    if pad_blocks:
        i = pl.program_id(2)

        @pl.when(i < pad_blocks)
        def _():
            o_ref[...] = jnp.zeros_like(o_ref)

        pl.when(i >= pad_blocks)(compute)
    else:
        compute()


def _mm(x, w, *, pad_blocks=0):
    b, s, k = x.shape
    m = w.shape[1]
    tn = min(MM_TN, m)
    grid = (m // tn, b, s // MM_TM + pad_blocks)
    return pl.pallas_call(
        functools.partial(_mm_kernel, pad_blocks=pad_blocks),
        out_shape=jax.ShapeDtypeStruct((b, s + pad_blocks * MM_TM, m), BF16),
        grid=grid,
        in_specs=[
            pl.BlockSpec((None, MM_TM, k),
                         lambda j, bb, i: (bb, jnp.maximum(i - pad_blocks, 0), 0)),
            pl.BlockSpec((k, tn), lambda j, bb, i: (0, j)),
        ],
        out_specs=pl.BlockSpec((None, MM_TM, tn), lambda j, bb, i: (bb, i, j)),
        compiler_params=_params(("arbitrary", "arbitrary", "arbitrary")),
        name="dense_proj",
    )(x, w)


def _attn_a_kernel(q_ref, k_ref, v_ref, bias_ref, o_ref, sa_ref, sb_ref):
    i = pl.program_id(2)
    head0 = lax.broadcasted_iota(jnp.int32, (A_TQ, LANES), 1) < A_HEAD_DIM
    ones_col = jnp.where(lax.broadcasted_iota(jnp.int32, (A_WIN, LANES), 1) == 0, 1.0, 0.0).astype(BF16)

    def window(sb):
        g = i * A_SUB + sb
        return g, pl.multiple_of(g * A_TQ, A_TQ)

    def scores(sb, dst_ref):
        _, start = window(sb)
        q = q_ref[pl.ds(start + PAD_ROWS, A_TQ), :]
        zero = jnp.zeros_like(q)
        q2 = jnp.concatenate([jnp.where(head0, q, zero), jnp.where(head0, zero, q)], axis=0)
        dst_ref[...] = lax.dot_general(q2, k_ref[pl.ds(start, A_WIN), :], (((1,), (1,)), ((), ())),
                                       preferred_element_type=F32)

    def absorb(src_ref, sb):
        g, start = window(sb)
        s = src_ref[...] + bias_ref[jnp.minimum(g, A_TABLES - 1)]
        p = jnp.exp2(s - jnp.max(s, axis=-1, keepdims=True))
        va = jnp.concatenate([v_ref[pl.ds(start, A_WIN), :], ones_col], axis=1)
        o = jnp.dot(p.astype(BF16), va, preferred_element_type=F32)
        o = o[:, 0:LANES] / o[:, LANES:LANES + 1]
        rows = pl.ds(pl.multiple_of(sb * A_TQ, A_TQ), A_TQ)
        o_ref[rows, :] = jnp.where(head0, o[0:A_TQ], o[A_TQ:2 * A_TQ]).astype(o_ref.dtype)

    scores(0, sa_ref)

    def pair(first):
        scores(first + 1, sb_ref)
        absorb(sa_ref, first)
        scores(first + 2, sa_ref)
        absorb(sb_ref, first + 1)

    def octet(jj, carry):
        for t in range(0, A_UNROLL, 2):
            pair(A_UNROLL * jj + t)
        return carry

    lax.fori_loop(0, A_SUB // A_UNROLL - 1, octet, 0)
    for t in range(A_SUB - A_UNROLL, A_SUB - 2, 2):
        pair(t)
    scores(A_SUB - 1, sb_ref)
    absorb(sa_ref, A_SUB - 2)
    absorb(sb_ref, A_SUB - 1)


def _attn_a(qkv_pad, bias_tab, s):
    b = qkv_pad.shape[0]
    d = A_HEADS * A_HEAD_DIM
    pairs = d // LANES
    sp = s + PAD_ROWS
    rows = A_TQ * A_SUB
    return pl.pallas_call(
        _attn_a_kernel,
        out_shape=jax.ShapeDtypeStruct((b, s, d), BF16),
        grid=(pairs, b, s // rows),
        in_specs=[
            pl.BlockSpec((None, sp, LANES), lambda j, bb, i: (bb, 0, j)),
            pl.BlockSpec((None, sp, LANES), lambda j, bb, i: (bb, 0, pairs + j)),
            pl.BlockSpec((None, sp, LANES), lambda j, bb, i: (bb, 0, 2 * pairs + j)),
            pl.BlockSpec((None, A_TABLES, 2 * A_TQ, A_WIN), lambda j, bb, i: (j, 0, 0, 0)),
        ],
        out_specs=pl.BlockSpec((None, rows, LANES), lambda j, bb, i: (bb, i, j)),
        scratch_shapes=[pltpu.VMEM((2 * A_TQ, A_WIN), F32), pltpu.VMEM((2 * A_TQ, A_WIN), F32)],
        compiler_params=_params(("arbitrary", "arbitrary", "arbitrary")),
        name="mixer_a",
    )(qkv_pad, qkv_pad, qkv_pad, bias_tab)


def _rel_bias_table(rel_bias):
    qi = jnp.arange(A_TQ)[:, None]
    kj = jnp.arange(A_WIN)[None, :]
    dchunk = kj // CHUNK - qi // CHUNK
    inband = (dchunk >= 0) & (dchunk <= LEFT_CHUNKS)
    period = A_TQ + A_WIN
    u = jnp.arange(period)
    u = jnp.where(u < A_WIN, u, u - period)
    idx = jnp.clip(PAD_ROWS - u, -REL_CLIP, REL_CLIP) + REL_CLIP
    g = rel_bias.astype(F32)[:, idx] * LOG2E
    flat = jnp.tile(g, (1, A_TQ))[:, :A_TQ * (period - 1)]
    tab = flat.reshape(-1, A_TQ, period - 1)[:, :, :A_WIN]
    first_real = PAD_ROWS - A_TQ * jnp.arange(A_TABLES)[:, None, None]
    keep = inband[None] & (kj[None] >= first_real)
    pair_tab = tab.reshape(-1, 1, 2, A_TQ, A_WIN)
    tabs = jnp.where(keep[None, :, None], pair_tab, NEG)
    return tabs.reshape(-1, A_TABLES, 2 * A_TQ, A_WIN)


def _layer_norm(z, gain, bias):
    mu = jnp.mean(z, axis=-1, keepdims=True)
    zc = z - mu
    var = jnp.mean(zc * zc, axis=-1, keepdims=True)
    return zc * lax.rsqrt(var + LN_EPS) * gain + bias


def _small_proj_t(y, yb, w_ref, n_out):
    ylo = (y - yb.astype(F32)).astype(BF16)
    w = w_ref[...]
    wh = w.astype(BF16)
    wl = (w - wh.astype(F32)).astype(BF16)
    both = jnp.dot(yb, jnp.concatenate([wh, wl], axis=1), preferred_element_type=F32)
    acc = both[:, 0:LANES] + jnp.dot(ylo, wh, preferred_element_type=F32) + both[:, LANES:2 * LANES]
    return acc.T[:n_out, :]


def _pad_small_weight(w):
    k, n = w.shape
    return jnp.zeros((k, LANES), F32).at[:, :n].set(w.astype(F32))


def _route(lt):
    m = jnp.max(lt, axis=0, keepdims=True)
    e = jnp.exp(lt - m)
    p = e / jnp.sum(e, axis=0, keepdims=True)
    rows = [p[k:k + 1, :] for k in range(N_EXPERTS)]
    best = None
    gi = None
    for g in range(N_GROUPS):
        r = rows[EXPERTS_PER_GROUP * g:EXPERTS_PER_GROUP * (g + 1)]
        sc = None
        for a in range(EXPERTS_PER_GROUP):
            for c in range(a + 1, EXPERTS_PER_GROUP):
                pair = r[a] + r[c]
                sc = pair if sc is None else jnp.maximum(sc, pair)
        if g == 0:
            best = sc
            gi = jnp.zeros(sc.shape, jnp.int32)
        else:
            upd = sc > best
            best = jnp.where(upd, sc, best)
            gi = jnp.where(upd, g, gi)
    pin = []
    for k in range(EXPERTS_PER_GROUP):
        v = rows[(N_GROUPS - 1) * EXPERTS_PER_GROUP + k]
        for g in range(N_GROUPS - 2, -1, -1):
            v = jnp.where(gi == g, rows[g * EXPERTS_PER_GROUP + k], v)
        pin.append(v)
    v1 = pin[0]
    i1 = jnp.zeros(v1.shape, jnp.int32)
    for k in range(1, EXPERTS_PER_GROUP):
        upd = pin[k] > v1
        v1 = jnp.where(upd, pin[k], v1)
        i1 = jnp.where(upd, k, i1)
    v2 = jnp.full(v1.shape, -1.0, F32)
    i2 = jnp.zeros(v1.shape, jnp.int32)
    for k in range(EXPERTS_PER_GROUP):
        cand = jnp.where(i1 == k, -1.0, pin[k])
        upd = cand > v2
        v2 = jnp.where(upd, cand, v2)
        i2 = jnp.where(upd, k, i2)
    tot = v1 + v2
    base = gi * EXPERTS_PER_GROUP
    experts = jnp.concatenate([base + i1, base + i2], axis=0)
    gates = jnp.concatenate([v1 / tot, v2 / tot], axis=0)
    return experts, gates


def _post_attn_kernel(a0_ref, alo_ref, ahi_ref, w_ref, x_ref, g_ref, b_ref, rw_ref, rb_ref,
                      x1_ref, x1b_ref, ex_ref, gt_ref, h0_ref, h1_ref):
    t = POST_TM

    def project(a_ref, h_ref):
        h_ref[...] = jnp.dot(a_ref[...], w_ref[...], preferred_element_type=F32)

    def finish(h_ref, half):
        rows = slice(half * t, (half + 1) * t)
        y = _layer_norm(ALPHA * x_ref[rows, :] + h_ref[...], g_ref[...], b_ref[...])
        yb = y.astype(BF16)
        x1_ref[rows, :] = y
        x1b_ref[rows, :] = yb
        lt = _small_proj_t(y, yb, rw_ref, N_EXPERTS) + rb_ref[...]
        experts, gates = _route(lt)
        ex_ref[:, rows] = experts
        gt_ref[:, rows] = gates

    @pl.when(pl.program_id(0) == 0)
    def _():
        project(a0_ref, h0_ref)

    project(alo_ref, h1_ref)
    finish(h0_ref, 0)
    project(ahi_ref, h0_ref)
    finish(h1_ref, 1)


def _post_attn(attn, w_o, x, gain, bias, rw, rb):
    n, d = x.shape
    t = POST_TM
    last_tile = n // t - 1
    row = lambda i: (i, 0)
    const = lambda i: (0, 0)
    return pl.pallas_call(
        _post_attn_kernel,
        out_shape=(jax.ShapeDtypeStruct((n, d), F32),
                   jax.ShapeDtypeStruct((n, d), BF16),
                   jax.ShapeDtypeStruct((2, n), jnp.int32),
                   jax.ShapeDtypeStruct((2, n), F32)),
        grid=(n // (2 * t),),
        in_specs=[
            pl.BlockSpec((t, d), const),
            pl.BlockSpec((t, d), lambda i: (2 * i + 1, 0)),
            pl.BlockSpec((t, d), lambda i: (jnp.minimum(2 * i + 2, last_tile), 0)),
            pl.BlockSpec((d, d), const),
            pl.BlockSpec((2 * t, d), row),
            pl.BlockSpec((1, d), const),
            pl.BlockSpec((1, d), const),
            pl.BlockSpec((d, LANES), const),
            pl.BlockSpec((N_EXPERTS, 1), const),
        ],
        out_specs=(pl.BlockSpec((2 * t, d), row),
                   pl.BlockSpec((2 * t, d), row),
                   pl.BlockSpec((2, 2 * t), lambda i: (0, i)),
                   pl.BlockSpec((2, 2 * t), lambda i: (0, i))),
        scratch_shapes=[pltpu.VMEM((t, d), F32), pltpu.VMEM((t, d), F32)],
        compiler_params=_params(("arbitrary",)),
        name="out_proj_ln_router",
    )(attn, attn, attn, w_o, x, gain, bias, rw, rb)


def _moe_kernel(iblk, iex, ivalid, ifirst, starts, ends, x_ref, wg_ref, wu_ref, wd_ref, o_ref,
                acc_ref):
    w = pl.program_id(0)
    f = pl.program_id(1)

    @pl.when((ifirst[w] == 1) & (f == 0))
    def _():
        acc_ref[...] = jnp.zeros_like(acc_ref)

    @pl.when(ivalid[w] == 1)
    def _():
        e = iex[w]
        lo = starts[e]
        hi = ends[e]
        base = iblk[w] * MOE_BLK

        def swiglu(x, wg, wu, wd, keep):
            hg = jnp.dot(x, wg, preferred_element_type=F32)
            hu = jnp.dot(x, wu, preferred_element_type=F32)
            hid = hg * jax.nn.sigmoid(hg) * hu
            if keep is not None:
                hid = jnp.where(keep, hid, 0.0)
            return jnp.dot(hid.astype(BF16), wd, preferred_element_type=F32)

        whole = (lo <= base) & (hi >= base + MOE_BLK)

        @pl.when(whole)
        def _():
            acc_ref[...] += swiglu(x_ref[...], wg_ref[...].astype(BF16), wu_ref[...].astype(BF16),
                                   wd_ref[...].astype(BF16), None)

        @pl.when(jnp.logical_not(whole))
        def _():
            wg = wg_ref[...].astype(BF16)
            wu = wu_ref[...].astype(BF16)
            wd = wd_ref[...].astype(BF16)
            for c in range(MOE_BLK // MOE_CHUNK):
                r0 = base + c * MOE_CHUNK

                @pl.when((lo < r0 + MOE_CHUNK) & (hi > r0))
                def _():
                    rows = slice(c * MOE_CHUNK, (c + 1) * MOE_CHUNK)
                    row = r0 + lax.broadcasted_iota(jnp.int32, (MOE_CHUNK, 1), 0)
                    acc_ref[rows, :] += swiglu(x_ref[rows, :], wg, wu, wd, (row >= lo) & (row < hi))

        @pl.when(f == pl.num_programs(1) - 1)
        def _():
            o_ref[...] = acc_ref[...].astype(o_ref.dtype)


def _moe(xs, w_gate, w_up, w_down, layer, meta):
    a, d = xs.shape
    dff = w_gate.shape[-1]
    nf = dff // MOE_TF
    n_items = meta[0].shape[0]

    def fsel(w, f, ivalid):
        return jnp.where(ivalid[w] == 1, f, nf - 1)

    grid_spec = pltpu.PrefetchScalarGridSpec(
        num_scalar_prefetch=6,
        grid=(n_items, nf),
        in_specs=[
            pl.BlockSpec((MOE_BLK, d), lambda w, f, ib, ie, iv, i1, st, en: (ib[w], 0)),
            pl.BlockSpec((None, None, d, MOE_TF),
                         lambda w, f, ib, ie, iv, i1, st, en: (layer, ie[w], 0, fsel(w, f, iv))),
            pl.BlockSpec((None, None, d, MOE_TF),
                         lambda w, f, ib, ie, iv, i1, st, en: (layer, ie[w], 0, fsel(w, f, iv))),
            pl.BlockSpec((None, None, MOE_TF, d),
                         lambda w, f, ib, ie, iv, i1, st, en: (layer, ie[w], fsel(w, f, iv), 0)),
        ],
        out_specs=pl.BlockSpec((MOE_BLK, d), lambda w, f, ib, ie, iv, i1, st, en: (ib[w], 0)),
        scratch_shapes=[pltpu.VMEM((MOE_BLK, d), F32)],
    )
    return pl.pallas_call(
        _moe_kernel,
        out_shape=jax.ShapeDtypeStruct((a, d), BF16),
        grid_spec=grid_spec,
        compiler_params=_params(("arbitrary", "arbitrary")),
        name="moe_experts",
    )(*meta, xs, w_gate, w_up, w_down)


def _moe_plan(experts):
    n = experts.shape[1]
    a = 2 * n
    nblk = a // MOE_BLK
    n_items = nblk + N_EXPERTS - 1
    e_flat = experts.reshape(-1)
    iota = jnp.arange(a, dtype=jnp.int32)
    e_sorted, order = lax.sort((e_flat, iota), num_keys=1, is_stable=True)
    tok_sorted = order % n
    _, pos = lax.sort((order, iota), num_keys=1)
    pos = pos.reshape(2, n)
    eids = jnp.arange(N_EXPERTS, dtype=jnp.int32)
    starts = jnp.sum((e_sorted[None, :] < eids[:, None]).astype(jnp.int32), axis=1)
    ends = jnp.concatenate([starts[1:], jnp.full((1,), a, jnp.int32)])
    e_lo = e_sorted[0::MOE_BLK]
    e_hi = e_sorted[MOE_BLK - 1::MOE_BLK]
    per_blk = e_hi - e_lo + 1
    iend = jnp.cumsum(per_blk).astype(jnp.int32)
    istart = iend - per_blk
    total = iend[-1]
    w = jnp.arange(n_items, dtype=jnp.int32)
    blk = jnp.minimum(jnp.sum((w[:, None] >= iend[None, :]).astype(jnp.int32), axis=1), nblk - 1)
    ex = e_lo[blk] + (w - istart[blk])
    valid = w < total
    blk = jnp.where(valid, blk, nblk - 1)
    ex = jnp.where(valid, ex, e_hi[nblk - 1])
    first = valid & (w == istart[blk])
    meta = (blk, ex.astype(jnp.int32), valid.astype(jnp.int32), first.astype(jnp.int32),
            starts, ends)
    return meta, tok_sorted, pos


def _combine_ln_kernel(x_ref, y_ref, gt_ref, g_ref, b_ref, o_ref, *maybe_ob_ref):
    gt = gt_ref[...]
    y = y_ref[0].astype(F32) * gt[:, 0:1] + y_ref[1].astype(F32) * gt[:, 1:2]
    out = _layer_norm(ALPHA * x_ref[...] + y, g_ref[...], b_ref[...])
    o_ref[...] = out
    for ob_ref in maybe_ob_ref:
        ob_ref[...] = out.astype(BF16)


def _combine_ln(x, y, gates_t, gain, bias, *, with_bf16):
    n, d = x.shape
    tm = LN_TM
    row = lambda i: (i, 0)
    const = lambda i: (0, 0)
    dtypes = (F32, BF16) if with_bf16 else (F32,)
    return pl.pallas_call(
        _combine_ln_kernel,
        out_shape=tuple(jax.ShapeDtypeStruct((n, d), t) for t in dtypes),
        grid=(n // tm,),
        in_specs=[
            pl.BlockSpec((tm, d), row),
            pl.BlockSpec((2, tm, d), lambda i: (0, i, 0)),
            pl.BlockSpec((tm, 2), row),
            pl.BlockSpec((1, d), const),
            pl.BlockSpec((1, d), const),
        ],
        out_specs=tuple(pl.BlockSpec((tm, d), row) for _ in dtypes),
        compiler_params=_params(("arbitrary",)),
        name="moe_combine_ln",
    )(x, y, gates_t, gain, bias)


def _fg_kernel(x_ref, w_ref, b_ref, tri_ref, qf_ref, kf_ref, c_ref, carry_ref, koff_ref):
    i = pl.program_id(1)

    @pl.when(i == 0)
    def _():
        carry_ref[...] = jnp.zeros_like(carry_ref)

    @pl.when(i % (FOX_TK // FOX_T) == 0)
    def _():
        koff_ref[...] = jnp.zeros_like(koff_ref)

    def split3(v):
        hi = v.astype(BF16)
        r1 = v - hi.astype(F32)
        mid = r1.astype(BF16)
        lo = (r1 - mid.astype(F32)).astype(BF16)
        return hi, mid, lo

    def columns(v, first_lane):
        hi, mid, lo = (p.astype(F32) for p in split3(v))
        return (pltpu.roll(hi, first_lane, 1) + pltpu.roll(mid, first_lane + B_HEADS, 1)
                + pltpu.roll(lo, first_lane + 2 * B_HEADS, 1))

    y = x_ref[...]
    yb = y.astype(BF16)
    ylo = (y - yb.astype(F32)).astype(BF16)
    w = w_ref[...]
    wh = w.astype(BF16)
    wl = (w - wh.astype(F32)).astype(BF16)
    z = (jnp.dot(yb, wh, preferred_element_type=F32) + jnp.dot(ylo, wh, preferred_element_type=F32)
         + jnp.dot(yb, wl, preferred_element_type=F32)) + b_ref[...]
    head_lane = lax.broadcasted_iota(jnp.int32, (FOX_T, LANES), 1) < B_HEADS
    lf = jnp.where(head_lane, (jnp.minimum(z, 0.0) - jnp.log1p(jnp.exp(-jnp.abs(z)))) * LOG2E, 0.0)
    parts = jnp.concatenate(split3(lf), axis=1)
    cs3 = jnp.dot(tri_ref[...], parts, preferred_element_type=F32)
    cs = cs3[:, 0:LANES] + cs3[:, LANES:2 * LANES] + cs3[:, 2 * LANES:3 * LANES]
    c_ref[...] = carry_ref[...]
    carry_ref[...] = carry_ref[...] + cs[FOX_T - 1:FOX_T, :]
    csk = cs + koff_ref[...]
    koff_ref[...] = koff_ref[...] + cs[FOX_T - 1:FOX_T, :]
    qf_ref[...] = columns(cs, 0).astype(BF16)
    kf_ref[...] = columns(-csk, FOX_KLANE).astype(BF16)


def _forget_columns(x, fw, fb):
    b, s, d = x.shape
    t = FOX_T
    nblk = s // t
    tri = (jnp.arange(t)[:, None] >= jnp.arange(t)[None, :]).astype(BF16)
    const = lambda bb, i: (0, 0)
    cols = jax.ShapeDtypeStruct((b, s, LANES), BF16)
    col_spec = pl.BlockSpec((None, t, LANES), lambda bb, i: (bb, i, 0))
    return pl.pallas_call(
        _fg_kernel,
        out_shape=(cols, cols, jax.ShapeDtypeStruct((b, nblk, 1, LANES), F32)),
        grid=(b, nblk),
        in_specs=[
            pl.BlockSpec((None, t, d), lambda bb, i: (bb, i, 0)),
            pl.BlockSpec((d, LANES), const),
            pl.BlockSpec((1, LANES), const),
            pl.BlockSpec((t, t), const),
        ],
        out_specs=(col_spec, col_spec,
                   pl.BlockSpec((None, None, 1, LANES), lambda bb, i: (bb, i, 0, 0))),
        scratch_shapes=[pltpu.VMEM((1, LANES), F32), pltpu.VMEM((1, LANES), F32)],
        compiler_params=_params(("arbitrary", "arbitrary")),
        name="forget_cumsum",
    )(x, fw, fb, tri)


def _fox_kernel(c_tab, q_ref, qf_ref, k_ref, kf_ref, v_ref, tri_ref, o_ref,
                qa_ref, kfh_ref, sa_ref, sb_ref, m_ref, acc_ref):
    h = pl.program_id(1)
    i = pl.program_id(2)
    tq, tk = FOX_T, FOX_TK
    per_tile = tk // tq
    lane = lax.broadcasted_iota(jnp.int32, (tq, LANES), 1)
    mine = (lane & (B_HEADS - 1)) == h
    q_side = lane < FOX_KLANE
    part_lane = (lane & (FOX_KLANE - 1)) < 3 * B_HEADS

    @pl.when(i == 0)
    def _():
        k_ones = jnp.where(mine & part_lane, 1.0, 0.0).astype(BF16)

        def fill(r, carry):
            rows = pl.ds(pl.multiple_of(r * tq, tq), tq)
            kfh_ref[rows, :] = jnp.where(q_side, k_ones, kf_ref[rows, :])
            return carry

        lax.fori_loop(0, kf_ref.shape[0] // tq, fill, 0)
    cbase = (pl.program_id(0) * B_HEADS + pl.program_id(1)) * pl.num_programs(2)
    c_q = jnp.full((1, LANES), c_tab[cbase + i], F32)
    qa_ref[:, 0:B_HEAD_DIM] = q_ref[...]
    q_ones = jnp.where(mine & part_lane, 1.0, 0.0).astype(BF16)
    qa_ref[:, B_HEAD_DIM:2 * B_HEAD_DIM] = jnp.where(q_side, qf_ref[...], q_ones)
    m_ref[...] = jnp.full(m_ref.shape, NEG, F32)
    acc_ref[...] = jnp.zeros_like(acc_ref)
    def ones_col(rows):
        return jnp.where(lax.broadcasted_iota(jnp.int32, (rows, LANES), 1) == 0, 1.0, 0.0).astype(BF16)

    def scores(jt, dst_ref):
        ks = pl.multiple_of(jt * tk, tk)
        ka = jnp.concatenate([k_ref[pl.ds(ks, tk), :], kfh_ref[pl.ds(ks, tk), :]], axis=1)
        dst_ref[...] = lax.dot_general(qa_ref[...], ka, (((1,), (1,)), ((), ())),
                                       preferred_element_type=F32)

    def absorb(src_ref, jt, own_blocks=None):
        ks = pl.multiple_of(jt * tk, tk)
        width = tk if own_blocks is None else own_blocks * tq
        if own_blocks is None:
            s = src_ref[...]
        else:
            diag = src_ref[:, width - tq:width] + tri_ref[...]
            s = diag if own_blocks == 1 else jnp.concatenate([src_ref[:, 0:width - tq], diag], axis=1)
        shift = c_q - jnp.full((1, LANES), c_tab[cbase + per_tile * jt], F32)
        m_prev = m_ref[...]
        m_new = jnp.maximum(m_prev, jnp.max(s, axis=-1, keepdims=True) + shift)
        p = jnp.exp2(s - jnp.concatenate([m_new - shift] * (width // LANES), axis=1))
        scale = jnp.exp2(m_prev - m_new)
        va = jnp.concatenate([v_ref[pl.ds(ks, width), :], ones_col(width)], axis=1)
        acc_ref[...] = jnp.concatenate([scale, scale], axis=1) * acc_ref[...] + jnp.dot(
            p.astype(BF16), va, preferred_element_type=F32)
        m_ref[...] = m_new

    def absorb_last(src_ref):
        own = i - per_tile * n_full
        for blocks in range(1, per_tile + 1):
            @pl.when(own == blocks - 1)
            def _(blocks=blocks):
                absorb(src_ref, n_full, blocks)

    n_full = i // per_tile
    scores(0, sa_ref)

    def pair(first):
        scores(first + 1, sb_ref)
        absorb(sa_ref, first)
        scores(first + 2, sa_ref)
        absorb(sb_ref, first + 1)

    def run(first, count):
        for t in range(0, count, 2):
            pair(first + t)

    done = 0
    for length in FOX_RUNS:
        @pl.when((n_full & length) != 0)
        def _(done=done, length=length):
            run(done, length)
        done = done + (n_full & length)

    @pl.when((n_full & 1) == 1)
    def _():
        scores(n_full, sb_ref)
        absorb(sa_ref, n_full - 1)
        absorb_last(sb_ref)

    @pl.when((n_full & 1) == 0)
    def _():
        absorb_last(sa_ref)

    acc = acc_ref[...]
    o_ref[...] = (acc[:, 0:B_HEAD_DIM] / acc[:, B_HEAD_DIM:B_HEAD_DIM + 1]).astype(o_ref.dtype)


def _fox(qkv, qf, kf, c_tab):
    b, s, d3 = qkv.shape
    d = d3 // 3
    t = FOX_T
    hd = B_HEAD_DIM
    causal = jnp.where(jnp.arange(t)[None, :] <= jnp.arange(t)[:, None], 0.0, NEG).astype(F32)
    grid_spec = pltpu.PrefetchScalarGridSpec(
        num_scalar_prefetch=1,
        grid=(b, B_HEADS, s // t),
        in_specs=[
            pl.BlockSpec((None, t, hd), lambda bb, h, i, c: (bb, i, h)),
            pl.BlockSpec((None, t, LANES), lambda bb, h, i, c: (bb, i, 0)),
            pl.BlockSpec((None, s, hd), lambda bb, h, i, c: (bb, 0, B_HEADS + h)),
            pl.BlockSpec((None, s, LANES), lambda bb, h, i, c: (bb, 0, 0)),
            pl.BlockSpec((None, s, hd), lambda bb, h, i, c: (bb, 0, 2 * B_HEADS + h)),
            pl.BlockSpec((t, t), lambda bb, h, i, c: (0, 0)),
        ],
        out_specs=pl.BlockSpec((None, t, hd), lambda bb, h, i, c: (bb, i, h)),
        scratch_shapes=[
            pltpu.VMEM((t, 2 * hd), BF16),
            pltpu.VMEM((s, LANES), BF16),
            pltpu.VMEM((t, FOX_TK), F32),
            pltpu.VMEM((t, FOX_TK), F32),
            pltpu.VMEM((t, LANES), F32),
            pltpu.VMEM((t, 2 * hd), F32),
        ],
    )
    return pl.pallas_call(
        _fox_kernel,
        out_shape=jax.ShapeDtypeStruct((b, s, d), BF16),
        grid_spec=grid_spec,
        compiler_params=_params(("arbitrary", "arbitrary", "arbitrary")),
        name="mixer_b",
    )(c_tab, qkv, qf, qkv, kf, qkv, causal)


def _moe_layer(x1, x1b, experts, gates, w_gate, w_up, w_down, layer, gain, bias):
    meta, tok_sorted, pos = _moe_plan(experts)
    xs = x1b[tok_sorted]
    yb = _moe(xs, w_gate, w_up, w_down, layer, meta)
    y = yb[pos.reshape(-1)].reshape(2, *x1.shape)
    return _combine_ln(x1, y, gates.T, gain, bias, with_bf16=layer < DEPTH - 1)


def kernel(x, a_w_qkv, a_rel_bias, a_w_o, kv_w, fg_w, fg_b, b_w_q, b_w_o, router_w, router_b,
           moe_w_gate, moe_w_up, moe_w_down, ln_gain, ln_bias):
    b, s, d = x.shape
    n = b * s
    assert MM_TM == PAD_ROWS and s % MM_TM == 0 and s % FOX_TK == 0 and FOX_TK % FOX_T == 0
    assert s % (A_TQ * A_SUB) == 0 and PAD_ROWS % A_TQ == 0 and A_SUB % A_UNROLL == 0 and A_SUB >= 2 * A_UNROLL and A_UNROLL % 2 == 0
    assert (2 * n) % MOE_BLK == 0 and s // FOX_TK <= 2 * FOX_RUNS[0]

    rw = _pad_small_weight(router_w)
    rb = router_b.astype(F32).reshape(N_EXPERTS, 1)
    gain = ln_gain.astype(F32).reshape(DEPTH, 2, 1, d)
    bias = ln_bias.astype(F32).reshape(DEPTH, 2, 1, d)
    wg, wu, wd = moe_w_gate, moe_w_up, moe_w_down

    qscale = jnp.concatenate([jnp.full((d,), LOG2E / math.sqrt(A_HEAD_DIM), F32),
                              jnp.ones((2 * d,), F32)])
    w_qkv = (a_w_qkv[0] * qscale[None, :]).astype(BF16)
    qkv = _mm(x, w_qkv, pad_blocks=1)
    attn = _attn_a(qkv, _rel_bias_table(a_rel_bias[0]), s)
    x1, x1b, experts, gates = _post_attn(attn.reshape(n, d), a_w_o[0].astype(BF16),
                                         x.reshape(n, d), gain[0, 0], bias[0, 0], rw, rb)
    x2, x2b = _moe_layer(x1, x1b, experts, gates, wg, wu, wd, 0, gain[0, 1], bias[0, 1])

    fb = jnp.zeros((1, LANES), F32).at[0, :B_HEADS].set(fg_b.astype(F32))
    qf, kf, carry_in = _forget_columns(x2.reshape(b, s, d), _pad_small_weight(fg_w), fb)
    c_tab = carry_in[:, :, 0, :B_HEADS].transpose(0, 2, 1).reshape(-1)

    w_q = (b_w_q[0] * (LOG2E / math.sqrt(B_HEAD_DIM))).astype(BF16)
    qkv_b = _mm(x2b.reshape(b, s, d), jnp.concatenate([w_q, kv_w.astype(BF16)], axis=1))
    attn = _fox(qkv_b, qf, kf, c_tab)
    x3, x3b, experts, gates = _post_attn(attn.reshape(n, d), b_w_o[0].astype(BF16), x2,
                                         gain[1, 0], bias[1, 0], rw, rb)
    (x4,) = _moe_layer(x3, x3b, experts, gates, wg, wu, wd, 1, gain[1, 1], bias[1, 1])
    return x4.reshape(b, s, d)
```
